```python
import math
import jax, jax.numpy as jnp
from jax import lax
import numpy as np

D_MODEL = 1024
BATCH = 32
SEQ = 256
DEPTH = 2
DEC_BATCH = 8
DEC_SEQ = 4096
PAST_LEN = 512

GRID_W = 64
POS_BASE = 10000.0
N_DIR = 2
N_HEADS = 4
HEAD_DK = D_MODEL // (2 * N_HEADS)
HEAD_DV = D_MODEL // N_HEADS
DK = N_HEADS * HEAD_DK
DV = N_HEADS * HEAD_DV
GATE_RANK = 16
GATE_TAU = 16.0
CHUNK = 64
S5_WIDTH = D_MODEL // 2
S5_GROUP = 16
S5_GROUPS = S5_WIDTH // S5_GROUP
S5_STATE = 64
D_FF = 4 * D_MODEL
N_MOD = 6
EPS = 1e-6
SPLITS = (DK, DK, DV, DV, N_DIR * GATE_RANK, S5_WIDTH, D_MODEL, D_MODEL)
SPLIT_IDX = tuple(int(i) for i in np.cumsum(SPLITS)[:-1])
D_IN = int(sum(SPLITS))

kernel_name = "hybrid_gla_s5_diffusion_step"


def rms_norm(x, g):
    x32 = x.astype(jnp.float32)
    y = x32 * lax.rsqrt(jnp.mean(jnp.square(x32), -1, keepdims=True) + EPS) * g.astype(jnp.float32)
    return y.astype(x.dtype)


def grid_pos_embed(n_tokens, dim):
    rows = n_tokens // GRID_W
    r = jnp.repeat(jnp.arange(rows, dtype=jnp.float32), GRID_W)
    col = jnp.tile(jnp.arange(GRID_W, dtype=jnp.float32), rows)
    quarter = dim // 4
    omega = 1.0 / (POS_BASE ** (jnp.arange(quarter, dtype=jnp.float32) / quarter))
    ar = r[:, None] * omega
    ac = col[:, None] * omega
    return jnp.concatenate([jnp.sin(ar), jnp.cos(ar), jnp.sin(ac), jnp.cos(ac)], axis=-1)


def gla_scan(q, k, v, log_a, s0):
    B, T, H, _ = q.shape
    dv = v.shape[-1]
    n = T // CHUNK

    def blocks(a):
        return a.reshape(B, n, CHUNK, H, a.shape[-1]).transpose(1, 0, 3, 2, 4)

    qb, kb, vb, gb = blocks(q), blocks(k), blocks(v), blocks(log_a)
    cum = jnp.cumsum(gb, axis=3)
    total = cum[:, :, :, -1:, :]
    q_dec = qb * jnp.exp(cum)
    k_in = kb * jnp.exp(-cum)
    k_out = kb * jnp.exp(total - cum)
    mask = jnp.tril(jnp.ones((CHUNK, CHUNK), dtype=bool))
    scores = jnp.where(mask, jnp.einsum('nbhid,nbhjd->nbhij', q_dec, k_in), 0.0)
    o_intra = jnp.einsum('nbhij,nbhjv->nbhiv', scores, vb)

    def step(s, inp):
        qd, ko, vv, tot = inp
        o = jnp.einsum('bhid,bhdv->bhiv', qd, s)
        s = jnp.exp(tot[:, :, 0, :])[..., None] * s + jnp.einsum('bhjd,bhjv->bhdv', ko, vv)
        return s, o

    s_final, o_inter = lax.scan(step, s0, (q_dec, k_out, vb, total))
    o = (o_intra + o_inter).transpose(1, 0, 3, 2, 4).reshape(B, T, H, dv)
    return o, s_final


def gla_mixer(q, k, v, r, glr, w_gate_up, b_gate, norm_g, s0):
    B, T, _ = q.shape
    dtype = q.dtype
    f32 = jnp.float32
    qh = q.astype(f32).reshape(B, T, N_HEADS, HEAD_DK) * (HEAD_DK ** -0.5)
    kh = k.astype(f32).reshape(B, T, N_HEADS, HEAD_DK)
    vh = v.astype(f32).reshape(B, T, N_HEADS, HEAD_DV)
    z = jnp.einsum('btnr,nrk->btnk', glr.astype(f32).reshape(B, T, N_DIR, GATE_RANK),
                   w_gate_up.astype(f32)) + b_gate.astype(f32)
    log_a = (jax.nn.log_sigmoid(z) / GATE_TAU).reshape(B, T, N_DIR, N_HEADS, HEAD_DK)
    s0 = s0.astype(f32)
    o_f, s_f = gla_scan(qh, kh, vh, log_a[:, :, 0], s0[:, 0])
    flip = lambda a: jnp.flip(a, axis=1)
    o_b, s_b = gla_scan(flip(qh), flip(kh), flip(vh), flip(log_a[:, :, 1]), s0[:, 1])
    o = o_f + flip(o_b)
    o = o * lax.rsqrt(jnp.mean(jnp.square(o), -1, keepdims=True) + EPS) * norm_g.astype(f32)
    o = o.reshape(B, T, DV) * jax.nn.silu(r.astype(f32))
    return o.astype(dtype), jnp.stack([s_f, s_b], axis=1)


def _linear_recurrence(earlier, later):
    a_e, b_e = earlier
    a_l, b_l = later
    return a_l * a_e, a_l * b_e + b_l


def s5_direction(u, lam_re, lam_im, log_step, b_re, b_im, c_re, c_im, s0_re, s0_im):
    T = u.shape[1]
    lam = lax.complex(lam_re, lam_im)
    lam_bar = jnp.exp(lam * jnp.exp(log_step)[:, None])
    b_bar = ((lam_bar - 1.0) / lam)[..., None] * lax.complex(b_re, b_im)
    bu = jnp.einsum('btgn,gpn->tbgp', u.astype(jnp.complex64), b_bar)
    bu = bu.at[0].add(lam_bar * lax.complex(s0_re, s0_im))
    a = jnp.broadcast_to(lam_bar, (T, 1) + lam_bar.shape)
    _, s = lax.associative_scan(_linear_recurrence, (a, bu), axis=0)
    y = jnp.einsum('tbgp,gnp->btgn', s, lax.complex(c_re, c_im)).real
    return y, s[-1].real, s[-1].imag


def s5_mixer(u, s5_params, d, w_glu, b_glu, s0_re, s0_im):
    B, T, _ = u.shape
    f32 = jnp.float32
    u32 = u.astype(f32)
    ug = u32.reshape(B, T, S5_GROUPS, S5_GROUP)
    prm = [p_.astype(f32) for p_ in s5_params]
    s0_re = s0_re.astype(f32)
    s0_im = s0_im.astype(f32)
    y_f, f_re, f_im = s5_direction(ug, *[p_[0] for p_ in prm], s0_re[:, 0], s0_im[:, 0])
    y_b, b_re, b_im = s5_direction(jnp.flip(ug, 1), *[p_[1] for p_ in prm], s0_re[:, 1], s0_im[:, 1])
    y = (y_f + jnp.flip(y_b, 1)).reshape(B, T, S5_WIDTH) + d.astype(f32) * u32
    y = jax.nn.gelu(y)
    y = y * jax.nn.sigmoid(y @ w_glu.astype(f32) + b_glu.astype(f32))
    return y.astype(u.dtype), jnp.stack([f_re, b_re], axis=1), jnp.stack([f_im, b_im], axis=1)


def trunk_layer(x, mod, gla_s0, s5_s0_re, s5_s0_im, p):
    shift1, scale1, gate1, shift2, scale2, gate2 = jnp.split(mod[:, None, :], N_MOD, axis=-1)
    h = rms_norm(x, p["norm1_g"]) * (1 + scale1) + shift1
    q, k, v, r, glr, u, ga, gb = jnp.split(h @ p["w_in"], SPLIT_IDX, axis=-1)
    o_gla, gla_state = gla_mixer(q, k, v, r, glr, p["w_gate_up"], p["b_gate"], p["gla_norm_g"], gla_s0)
    o_s5, s5_re, s5_im = s5_mixer(u, p["s5"], p["s5_d"], p["w_glu"], p["b_glu"], s5_s0_re, s5_s0_im)
    merged = (jax.nn.sigmoid(ga) * (o_gla @ p["w_proj_gla"])
              + jax.nn.sigmoid(gb) * (o_s5 @ p["w_proj_s5"]))
    x = x + gate1 * (merged @ p["w_out"])
    h2 = rms_norm(x, p["norm2_g"]) * (1 + scale2) + shift2
    x = x + gate2 * (jnp.square(jax.nn.relu(h2 @ p["w_ff1"])) @ p["w_ff2"])
    return x, gla_state, s5_re, s5_im


def setup_inputs(seed: int = 0) -> dict:
    key = jax.random.key(seed)
    ks = iter(jax.random.split(key, 40))
    f32 = jnp.float32

    def nrm(shape, scale):
        return scale * jax.random.normal(next(ks), shape, f32)

    n_idx = jnp.arange(S5_STATE, dtype=f32)
    s5_shape = (DEPTH, N_DIR, S5_GROUPS, S5_STATE)
    return {
        "x_prompt": nrm((BATCH, SEQ, D_MODEL), 1.0),
        "x_sample": nrm((DEC_BATCH, DEC_SEQ, D_MODEL), 1.0),
        "c": nrm((DEC_BATCH, D_MODEL), 1.0),
        "cache_gla_state": nrm((DEC_BATCH, DEPTH, N_DIR, N_HEADS, HEAD_DK, HEAD_DV), 1.0),
        "state_s5_re": nrm((DEC_BATCH, DEPTH, N_DIR, S5_GROUPS, S5_STATE), 0.5),
        "state_s5_im": nrm((DEC_BATCH, DEPTH, N_DIR, S5_GROUPS, S5_STATE), 0.5),
        "c_ctx": nrm((D_MODEL,), 1.0),
        "w_mod": nrm((DEPTH, D_MODEL, N_MOD * D_MODEL), 0.5 * D_MODEL ** -0.5),
        "b_mod": nrm((DEPTH, N_MOD * D_MODEL), 0.02),
        "norm1_g": 1.0 + nrm((DEPTH, D_MODEL), 0.02),
        "w_in": nrm((DEPTH, D_MODEL, D_IN), D_MODEL ** -0.5),
        "w_gate_up": nrm((DEPTH, N_DIR, GATE_RANK, DK), GATE_RANK ** -0.5),
        "b_gate": nrm((DEPTH, N_DIR, DK), 0.1),
        "gla_norm_g": 1.0 + nrm((DEPTH, HEAD_DV), 0.02),
        "w_proj_gla": nrm((DEPTH, DV, D_MODEL), DV ** -0.5),
        "s5_lam_re": -0.5 + nrm(s5_shape, 0.01),
        "s5_lam_im": math.pi * n_idx + nrm(s5_shape, 0.01),
        "s5_log_step": jax.random.uniform(next(ks), (DEPTH, N_DIR, S5_GROUPS), f32,
                                          math.log(1e-3), math.log(1e-1)),
        "s5_b_re": nrm((DEPTH, N_DIR, S5_GROUPS, S5_STATE, S5_GROUP), (2 * S5_GROUP) ** -0.5),
        "s5_b_im": nrm((DEPTH, N_DIR, S5_GROUPS, S5_STATE, S5_GROUP), (2 * S5_GROUP) ** -0.5),
        "s5_c_re": nrm((DEPTH, N_DIR, S5_GROUPS, S5_GROUP, S5_STATE), S5_STATE ** -0.5),
        "s5_c_im": nrm((DEPTH, N_DIR, S5_GROUPS, S5_GROUP, S5_STATE), S5_STATE ** -0.5),
        "s5_d": nrm((DEPTH, S5_WIDTH), 0.5),
        "w_glu": nrm((DEPTH, S5_WIDTH, S5_WIDTH), S5_WIDTH ** -0.5),
        "b_glu": nrm((DEPTH, S5_WIDTH), 0.02),
        "w_proj_s5": nrm((DEPTH, S5_WIDTH, D_MODEL), S5_WIDTH ** -0.5),
        "w_out": nrm((DEPTH, D_MODEL, D_MODEL), D_MODEL ** -0.5),
        "norm2_g": 1.0 + nrm((DEPTH, D_MODEL), 0.02),
        "w_ff1": nrm((DEPTH, D_MODEL, D_FF), D_MODEL ** -0.5),
        "w_ff2": nrm((DEPTH, D_FF, D_MODEL), D_FF ** -0.5),
        "final_g": 1.0 + nrm((D_MODEL,), 0.02),
    }


def reference(x_prompt, x_sample, c, cache_gla_state, state_s5_re, state_s5_im, c_ctx,
              w_mod, b_mod, norm1_g, w_in, w_gate_up, b_gate, gla_norm_g, w_proj_gla,
              s5_lam_re, s5_lam_im, s5_log_step, s5_b_re, s5_b_im, s5_c_re, s5_c_im, s5_d,
              w_glu, b_glu, w_proj_s5, w_out, norm2_g, w_ff1, w_ff2, final_g):
    layers = [dict(norm1_g=norm1_g[l], w_in=w_in[l], w_gate_up=w_gate_up[l], b_gate=b_gate[l],
                   gla_norm_g=gla_norm_g[l], w_proj_gla=w_proj_gla[l],
                   s5=(s5_lam_re[l], s5_lam_im[l], s5_log_step[l], s5_b_re[l], s5_b_im[l],
                       s5_c_re[l], s5_c_im[l]),
                   s5_d=s5_d[l], w_glu=w_glu[l], b_glu=b_glu[l], w_proj_s5=w_proj_s5[l],
                   w_out=w_out[l], norm2_g=norm2_g[l], w_ff1=w_ff1[l], w_ff2=w_ff2[l])
              for l in range(DEPTH)]

    nb = x_prompt.shape[0]
    gla0 = jnp.zeros((nb, N_DIR, N_HEADS, HEAD_DK, HEAD_DV), jnp.float32)
    s50 = jnp.zeros((nb, N_DIR, S5_GROUPS, S5_STATE), jnp.float32)
    xc = x_prompt
    gla_list, s5_re_list, s5_im_list = [], [], []
    for l in range(DEPTH):
        mod_ctx = (jax.nn.silu(c_ctx) @ w_mod[l] + b_mod[l])[None]
        xc, g_st, s_re, s_im = trunk_layer(xc, mod_ctx, gla0, s50, s50, layers[l])
        gla_list.append(g_st)
        s5_re_list.append(s_re)
        s5_im_list.append(s_im)
    y_prompt = rms_norm(xc, final_g)
    new_gla_state = jnp.stack(gla_list, axis=1)
    new_s5_re = jnp.stack(s5_re_list, axis=1)
    new_s5_im = jnp.stack(s5_im_list, axis=1)

    xs = x_sample + grid_pos_embed(x_sample.shape[1], D_MODEL).astype(x_sample.dtype)
    for l in range(DEPTH):
        mod = jax.nn.silu(c) @ w_mod[l] + b_mod[l]
        xs, _, _, _ = trunk_layer(xs, mod, cache_gla_state[:, l], state_s5_re[:, l],
                                  state_s5_im[:, l], layers[l])
    y_sample = rms_norm(xs, final_g)

    return (y_prompt, y_sample, new_gla_state, new_s5_re, new_s5_im)
```

```python
import functools
import math

import jax
import jax.numpy as jnp
import numpy as np
from jax import lax
from jax.experimental import pallas as pl
from jax.experimental.pallas import tpu as pltpu

N_DIR = 2
N_HEADS = 4
GATE_RANK = 16
GATE_TAU = 16.0
CHUNK = 64
S5_GROUP = 16
S5_STATE = 64
N_MOD = 6
EPS = 1e-6
GRID_W = 64
POS_BASE = 10000.0

LANES = 128
SUBLANES = 8
VMEM_LIMIT = 56 * 1024 * 1024

F32 = jnp.float32
BF16 = jnp.bfloat16


def _const_spec(shape):
    nd = len(shape)
    return pl.BlockSpec(shape, lambda *_: (0,) * nd, pipeline_mode=pl.Buffered(1))


def _dot(a, b):
    return jnp.dot(a, b, preferred_element_type=F32)


def _rms_mod(x, g, scale, shift):
    ms = jnp.mean(jnp.square(x), axis=-1, keepdims=True)
    return x * lax.rsqrt(ms + EPS) * g * (1.0 + scale) + shift


def _silu(x):
    return x * jax.nn.sigmoid(x)


def _mod_kernel(c_ref, w_ref, b_ref, o_ref):
    a = _silu(c_ref[...]).astype(BF16)
    o_ref[...] = _dot(a, w_ref[...].astype(BF16)) + b_ref[...]


def _modulation(conds, w_mod, b_mod):
    depth, d, dm = w_mod.shape
    rows = conds.shape[0]
    tn = 1024
    return pl.pallas_call(
        _mod_kernel,
        grid=(depth, dm // tn),
        in_specs=[
            pl.BlockSpec((rows, d), lambda l, j: (0, 0)),
            pl.BlockSpec((None, d, tn), lambda l, j: (l, 0, j)),
            pl.BlockSpec((None, 1, tn), lambda l, j: (l, 0, j)),
        ],
        out_specs=pl.BlockSpec((None, rows, tn), lambda l, j: (l, 0, j)),
        out_shape=jax.ShapeDtypeStruct((depth, rows, dm), F32),
        compiler_params=pltpu.CompilerParams(vmem_limit_bytes=VMEM_LIMIT),
        name="modulation",
    )(conds, w_mod, b_mod.reshape(depth, 1, dm))


def _inproj_kernel(*refs, d, add_pos):
    if add_pos:
        x_ref, pos_ref, mod_ref, g_ref = refs[:4]
        rest = refs[4:]
    else:
        x_ref, mod_ref, g_ref = refs[:3]
        rest = refs[3:]
    w_refs = rest[:8]
    out_refs = rest[8:]
    x = x_ref[...]
    if add_pos:
        x = x + pos_ref[...]
        out_refs[0][...] = x
        out_refs = out_refs[1:]
    shift = mod_ref[:, 0:d]
    scale = mod_ref[:, d:2 * d]
    h = _rms_mod(x, g_ref[...], scale, shift).astype(BF16)
    for w_ref, o_ref in zip(w_refs, out_refs):
        o_ref[...] = _dot(h, w_ref[...])


def _inproj(x, pos, mod, g, weights, tm):
    b, t, d = x.shape
    add_pos = pos is not None
    per_batch_mod = mod.shape[0] != 1
    widths = [w.shape[1] for w in weights]
    row_spec = lambda n: pl.BlockSpec((None, tm, n), lambda bi, i: (bi, i, 0))
    in_specs = [row_spec(d)]
    args = [x]
    if add_pos:
        in_specs.append(pl.BlockSpec((tm, d), lambda bi, i: (i, 0)))
        args.append(pos)
    mod_idx = (lambda bi, i: (bi, 0, 0)) if per_batch_mod else (lambda bi, i: (0, 0, 0))
    in_specs += [pl.BlockSpec((None, 1, mod.shape[2]), mod_idx), _const_spec((1, d))]
    args += [mod, g.reshape(1, d)]
    in_specs += [_const_spec(w.shape) for w in weights]
    args += list(weights)

    out_specs, out_shapes = [], []
    if add_pos:
        out_specs.append(row_spec(d))
        out_shapes.append(jax.ShapeDtypeStruct((b, t, d), F32))
    for idx, n in enumerate(widths):
        if idx == 5:
            out_specs.append(pl.BlockSpec((tm, n), lambda bi, i: (i, bi)))
            out_shapes.append(jax.ShapeDtypeStruct((t, b * n), F32))
        else:
            out_specs.append(row_spec(n))
            out_shapes.append(jax.ShapeDtypeStruct((b, t, n), F32))
    return pl.pallas_call(
        functools.partial(_inproj_kernel, d=d, add_pos=add_pos),
        grid=(b, t // tm),
        in_specs=in_specs,
        out_specs=out_specs,
        out_shape=out_shapes,
        compiler_params=pltpu.CompilerParams(
            dimension_semantics=("parallel", "parallel"), vmem_limit_bytes=VMEM_LIMIT),
        name="inproj",
    )(*args)


def _s5_kernel(*refs, tt, n_kb, zero_init, emit_state, scan_lanes):
    if zero_init:
        u_ref, lr_ref, li_ref, bre_ref, bim_ref, cre_ref, cim_ref = refs[:7]
        rest = refs[7:]
    else:
        u_ref, lr_ref, li_ref, bre_ref, bim_ref, cre_ref, cim_ref, s0r_ref, s0i_ref = refs[:9]
        rest = refs[9:]
    if emit_state:
        y_ref, fr_ref, fi_ref, sre, sim, car, cai = rest
    else:
        y_ref, sre, sim, car, cai = rest
    direction = pl.program_id(1)
    i = pl.program_id(2)
    rows = tt * SUBLANES
    kw = u_ref.shape[-1] // n_kb
    sw = sre.shape[-1] // n_kb

    @pl.when(i == 0)
    def _():
        if zero_init:
            car[...] = jnp.zeros_like(car)
            cai[...] = jnp.zeros_like(cai)
        else:
            car[...] = s0r_ref[...]
            cai[...] = s0i_ref[...]

    u = u_ref[...].reshape(rows, u_ref.shape[-1]).astype(BF16)
    for kb in range(n_kb):
        ub = u[:, kb * kw:(kb + 1) * kw]
        sre[:, kb * sw:(kb + 1) * sw] = _dot(ub, bre_ref[kb])
        sim[:, kb * sw:(kb + 1) * sw] = _dot(ub, bim_ref[kb])

    for lb in range(sre.shape[-1] // scan_lanes):
        ls = slice(lb * scan_lanes, (lb + 1) * scan_lanes)
        lr = lr_ref[:, ls]
        li = li_ref[:, ls]

        def step(j, carry, ls=ls, lr=lr, li=li):
            cr, ci = carry
            t = j + direction * (tt - 1 - 2 * j)
            r0 = pl.multiple_of(t * SUBLANES, SUBLANES)
            nr = lr * cr - li * ci + sre[pl.ds(r0, SUBLANES), ls]
            ni = lr * ci + li * cr + sim[pl.ds(r0, SUBLANES), ls]
            sre[pl.ds(r0, SUBLANES), ls] = nr
            sim[pl.ds(r0, SUBLANES), ls] = ni
            return nr, ni

        cr, ci = lax.fori_loop(0, tt, step, (car[:, ls], cai[:, ls]), unroll=8)
        car[:, ls] = cr
        cai[:, ls] = ci

    for kb in range(n_kb):
        y = (_dot(sre[:, kb * sw:(kb + 1) * sw].astype(BF16), cre_ref[kb])
             + _dot(sim[:, kb * sw:(kb + 1) * sw].astype(BF16), cim_ref[kb]))
        y_ref[:, :, kb * kw:(kb + 1) * kw] = y.reshape(tt, SUBLANES, kw)

    if emit_state:
        @pl.when(i == pl.num_programs(2) - 1)
        def _():
            fr_ref[...] = car[...]
            fi_ref[...] = cai[...]


def _s5_scan(u_tm, prm, s0, b, emit_state, tt):
    t = u_tm.shape[0]
    w = u_tm.shape[1] // b
    lr, li, bre, bim, cre, cim = prm
    n_kb = bre.shape[1]
    ns = lr.shape[-1]
    n_t = t // tt
    n_g = b // SUBLANES
    zero_init = s0 is None
    tile = lambda g, d, i: i + d * (n_t - 1 - 2 * i)
    dir_spec = lambda shape: pl.BlockSpec((None,) + shape, lambda g, d, i: (d,) + (0,) * len(shape),
                                          pipeline_mode=pl.Buffered(1))
    in_specs = [
        pl.BlockSpec((tt, SUBLANES, w), lambda g, d, i: (tile(g, d, i), g, 0)),
        dir_spec((SUBLANES, ns)), dir_spec((SUBLANES, ns)),
        dir_spec(bre.shape[1:]), dir_spec(bim.shape[1:]),
        dir_spec(cre.shape[1:]), dir_spec(cim.shape[1:]),
    ]
    args = [u_tm.reshape(t, b, w), lr, li, bre, bim, cre, cim]
    state_spec = pl.BlockSpec((None, SUBLANES, ns), lambda g, d, i: (d, g, 0))
    if not zero_init:
        in_specs += [state_spec, state_spec]
        args += list(s0)
    out_specs = [pl.BlockSpec((None, tt, SUBLANES, w), lambda g, d, i: (d, tile(g, d, i), g, 0))]
    out_shapes = [jax.ShapeDtypeStruct((N_DIR, t, b, w), F32)]
    if emit_state:
        out_specs += [state_spec, state_spec]
        out_shapes += [jax.ShapeDtypeStruct((N_DIR, b, ns), F32)] * 2
    res = pl.pallas_call(
        functools.partial(_s5_kernel, tt=tt, n_kb=n_kb, zero_init=zero_init,
                          emit_state=emit_state, scan_lanes=4 * LANES),
        grid=(n_g, N_DIR, n_t),
        in_specs=in_specs,
        out_specs=out_specs,
        out_shape=out_shapes,
        scratch_shapes=[pltpu.VMEM((tt * SUBLANES, ns), F32), pltpu.VMEM((tt * SUBLANES, ns), F32),
                        pltpu.VMEM((SUBLANES, ns), F32), pltpu.VMEM((SUBLANES, ns), F32)],
        compiler_params=pltpu.CompilerParams(
            dimension_semantics=("parallel", "parallel", "arbitrary"), vmem_limit_bytes=VMEM_LIMIT),
        name="s5_scan",
    )(*args)
    y = res[0].reshape(N_DIR, t, b * w)
    if emit_state:
        return y, res[1], res[2]
    return y


def _s5_params(lam_re, lam_im, log_step, b_re, b_im, c_re, c_im):
    n_dir, g, p = lam_re.shape
    n = b_re.shape[-1]
    gpb = LANES // n
    n_kb = g // gpb
    step = jnp.exp(log_step)[..., None]
    mag = jnp.exp(lam_re * step)
    lbr = mag * jnp.cos(lam_im * step)
    lbi = mag * jnp.sin(lam_im * step)
    den = lam_re * lam_re + lam_im * lam_im
    fr = ((lbr - 1.0) * lam_re + lbi * lam_im) / den
    fi = (lbi * lam_re - (lbr - 1.0) * lam_im) / den
    bbr = fr[..., None] * b_re - fi[..., None] * b_im
    bbi = fr[..., None] * b_im + fi[..., None] * b_re
    eye = jnp.eye(gpb, dtype=F32)

    def pack_b(m):
        m = m.reshape(n_dir, n_kb, gpb, p, n).transpose(0, 1, 2, 4, 3)
        m = m[:, :, :, :, None, :] * eye[None, None, :, None, :, None]
        return m.reshape(n_dir, n_kb, gpb * n, gpb * p).astype(BF16)

    def pack_c(m):
        m = m.reshape(n_dir, n_kb, gpb, n, p).transpose(0, 1, 2, 4, 3)
        m = m[:, :, :, :, None, :] * eye[None, None, :, None, :, None]
        return m.reshape(n_dir, n_kb, gpb * p, gpb * n).astype(BF16)

    bcast = lambda a: jnp.broadcast_to(a.reshape(n_dir, 1, g * p), (n_dir, SUBLANES, g * p))
    return (bcast(lbr), bcast(lbi), pack_b(bbr), pack_b(bbi), pack_c(c_re), pack_c(-c_im))


def _log_sigmoid(z):
    return -(jnp.maximum(-z, 0.0) + jnp.log1p(jnp.exp(-jnp.abs(z))))


def _gla_kernel(*refs, n_chunks, dk, dv, zero_init, emit_state):
    q_ref, k_ref, v_ref, glr_ref, wup_ref, bg_ref, m_ref = refs[:7]
    rest = refs[7:]
    if not zero_init:
        s0_ref = rest[0]
        rest = rest[1:]
    if emit_state:
        o_ref, sf_ref, st = rest
    else:
        o_ref, st = rest
    direction = pl.program_id(1)
    i = pl.program_id(2)
    scale = dk ** -0.5

    @pl.when(i == 0)
    def _():
        for h in range(N_HEADS):
            if zero_init:
                st[h] = jnp.zeros((dv, dk), F32)
            else:
                st[h] = s0_ref[h].T

    m = m_ref[...]
    causal = m > 0.0

    def chunk(cc, carry):
        c = cc + direction * (n_chunks - 1 - 2 * cc)
        rows = pl.ds(pl.multiple_of(c * CHUNK, CHUNK), CHUNK)
        z = _dot(glr_ref[rows, :].astype(BF16), wup_ref[...]) + bg_ref[...]
        g = _log_sigmoid(z) / GATE_TAU
        cum = jnp.dot(m, g, precision=lax.Precision.HIGHEST, preferred_element_type=F32)
        total = jnp.sum(g, axis=0, keepdims=True)
        q_dec = (q_ref[rows, :] * scale * jnp.exp(cum)).astype(BF16)
        kk = k_ref[rows, :]
        k_in = (kk * jnp.exp(-cum)).astype(BF16)
        k_out = (kk * jnp.exp(total - cum)).astype(BF16)
        decay = jnp.exp(total)
        for h in range(N_HEADS):
            ks = slice(h * dk, (h + 1) * dk)
            vs = slice(h * dv, (h + 1) * dv)
            qh = q_dec[:, ks]
            vh = v_ref[rows, vs].astype(BF16)
            sc = lax.dot_general(qh, k_in[:, ks], (((1,), (1,)), ((), ())),
                                 preferred_element_type=F32)
            sc = jnp.where(causal, sc, 0.0).astype(BF16)
            s_t = st[h]
            o = _dot(sc, vh) + lax.dot_general(qh, s_t.astype(BF16), (((1,), (1,)), ((), ())),
                                               preferred_element_type=F32)
            o_ref[rows, vs] = o
            upd = lax.dot_general(vh, k_out[:, ks], (((0,), (0,)), ((), ())),
                                  preferred_element_type=F32)
            st[h] = decay[:, ks] * s_t + upd
        return carry

    lax.fori_loop(0, n_chunks, chunk, 0)

    if emit_state:
        @pl.when(i == pl.num_programs(2) - 1)
        def _():
            for h in range(N_HEADS):
                sf_ref[h] = st[h].T


def _gla_scan(q, k, v, glr, wup, bg, masks, s0, emit_state, tt):
    b, t, dkk = q.shape
    dvv = v.shape[-1]
    dk, dv = dkk // N_HEADS, dvv // N_HEADS
    n_t = t // tt
    zero_init = s0 is None
    tile = lambda d, i: i + d * (n_t - 1 - 2 * i)
    row_spec = lambda n: pl.BlockSpec((None, tt, n), lambda bi, d, i: (bi, tile(d, i), 0))
    dir_spec = lambda shape: pl.BlockSpec((None,) + shape, lambda bi, d, i: (d,) + (0,) * len(shape))
    state_spec = pl.BlockSpec((None, None, N_HEADS, dk, dv), lambda bi, d, i: (bi, d, 0, 0, 0))
    in_specs = [row_spec(dkk), row_spec(dkk), row_spec(dvv), row_spec(glr.shape[-1]),
                dir_spec(wup.shape[1:]), dir_spec(bg.shape[1:]), dir_spec(masks.shape[1:])]
    args = [q, k, v, glr, wup, bg, masks]
    if not zero_init:
        in_specs.append(state_spec)
        args.append(s0)
    out_specs = [pl.BlockSpec((None, None, tt, dvv), lambda bi, d, i: (d, bi, tile(d, i), 0))]
    out_shapes = [jax.ShapeDtypeStruct((N_DIR, b, t, dvv), F32)]
    if emit_state:
        out_specs.append(state_spec)
        out_shapes.append(jax.ShapeDtypeStruct((b, N_DIR, N_HEADS, dk, dv), F32))
    res = pl.pallas_call(
        functools.partial(_gla_kernel, n_chunks=tt // CHUNK, dk=dk, dv=dv,
                          zero_init=zero_init, emit_state=emit_state),
        grid=(b, N_DIR, n_t),
        in_specs=in_specs,
        out_specs=out_specs,
        out_shape=out_shapes,
        scratch_shapes=[pltpu.VMEM((N_HEADS, dv, dk), F32)],
        compiler_params=pltpu.CompilerParams(
            dimension_semantics=("parallel", "parallel", "arbitrary"), vmem_limit_bytes=VMEM_LIMIT),
        name="gla_scan",
    )(*args)
    return res if emit_state else res[0]


def _gelu_tanh(x):
    return 0.5 * x * (1.0 + jnp.tanh(math.sqrt(2.0 / math.pi) * (x + 0.044715 * (x * x * x))))


def _mixout_kernel(x_ref, of_ref, ob_ref, r_ref, ga_ref, gb_ref, u_ref, yf_ref, yb_ref, mod_ref,
                   gn_ref, d_ref, wglu_ref, bglu_ref, wpg_ref, wps_ref, wout_ref, o_ref, *, d, dv):
    o = of_ref[...] + ob_ref[...]
    r = r_ref[...]
    gn = gn_ref[...]
    parts = []
    for h in range(N_HEADS):
        vs = slice(h * dv, (h + 1) * dv)
        oh = o[:, vs]
        ms = jnp.mean(jnp.square(oh), axis=-1, keepdims=True)
        parts.append((oh * lax.rsqrt(ms + EPS) * gn * _silu(r[:, vs])).astype(BF16))
    pg = _dot(jnp.concatenate(parts, axis=-1), wpg_ref[...])

    u = u_ref[...]
    y = _gelu_tanh(yf_ref[...] + yb_ref[...] + d_ref[...] * u)
    y = y * jax.nn.sigmoid(_dot(y.astype(BF16), wglu_ref[...]) + bglu_ref[...])
    ps = _dot(y.astype(BF16), wps_ref[...])

    merged = jax.nn.sigmoid(ga_ref[...]) * pg + jax.nn.sigmoid(gb_ref[...]) * ps
    gate = mod_ref[:, 2 * d:3 * d]
    o_ref[...] = x_ref[...] + gate * _dot(merged.astype(BF16), wout_ref[...])


def _mixout(x, o, r, ga, gb, u_tm, y, mod, gn, s5d, wglu, bglu, wpg, wps, wout, tm):
    b, t, d = x.shape
    dvv = r.shape[-1]
    w = u_tm.shape[1] // b
    per_batch_mod = mod.shape[0] != 1
    row_spec = lambda n: pl.BlockSpec((None, tm, n), lambda bi, i: (bi, i, 0))
    mod_idx = (lambda bi, i: (bi, 0, 0)) if per_batch_mod else (lambda bi, i: (0, 0, 0))
    in_specs = [
        row_spec(d),
        pl.BlockSpec((None, None, tm, dvv), lambda bi, i: (0, bi, i, 0)),
        pl.BlockSpec((None, None, tm, dvv), lambda bi, i: (1, bi, i, 0)),
        row_spec(dvv), row_spec(d), row_spec(d),
        pl.BlockSpec((tm, w), lambda bi, i: (i, bi)),
        pl.BlockSpec((None, tm, w), lambda bi, i: (0, i, bi)),
        pl.BlockSpec((None, tm, w), lambda bi, i: (1, i, bi)),
        pl.BlockSpec((None, 1, mod.shape[2]), mod_idx),
        _const_spec((1, dvv // N_HEADS)), _const_spec((1, w)),
        _const_spec(wglu.shape), _const_spec((1, w)),
        _const_spec(wpg.shape), _const_spec(wps.shape), _const_spec(wout.shape),
    ]
    return pl.pallas_call(
        functools.partial(_mixout_kernel, d=d, dv=dvv // N_HEADS),
        grid=(b, t // tm),
        in_specs=in_specs,
        out_specs=row_spec(d),
        out_shape=jax.ShapeDtypeStruct((b, t, d), F32),
        compiler_params=pltpu.CompilerParams(
            dimension_semantics=("parallel", "parallel"), vmem_limit_bytes=VMEM_LIMIT),
        name="mixout",
    )(x, o, o, r, ga, gb, u_tm, y, y, mod, gn.reshape(1, -1), s5d.reshape(1, -1),
      wglu, bglu.reshape(1, -1), wpg, wps, wout)


def _mlp_kernel(*refs, d, ff_block, final_norm):
    if final_norm:
        x_ref, mod_ref, g_ref, w1_ref, w2_ref, fg_ref, o_ref = refs
    else:
        x_ref, mod_ref, g_ref, w1_ref, w2_ref, o_ref = refs
    x = x_ref[...]
    shift = mod_ref[:, 3 * d:4 * d]
    scale = mod_ref[:, 4 * d:5 * d]
    gate = mod_ref[:, 5 * d:6 * d]
    h = _rms_mod(x, g_ref[...], scale, shift).astype(BF16)
    acc = jnp.zeros(x.shape, F32)
    for j in range(w1_ref.shape[1] // ff_block):
        cs = slice(j * ff_block, (j + 1) * ff_block)
        a = jnp.square(jnp.maximum(_dot(h, w1_ref[:, cs]), 0.0)).astype(BF16)
        acc = acc + _dot(a, w2_ref[cs, :])
    x = x + gate * acc
    if final_norm:
        ms = jnp.mean(jnp.square(x), axis=-1, keepdims=True)
        x = x * lax.rsqrt(ms + EPS) * fg_ref[...]
    o_ref[...] = x


def _mlp(x, mod, g, w1, w2, final_g, tm):
    b, t, d = x.shape
    per_batch_mod = mod.shape[0] != 1
    final_norm = final_g is not None
    row_spec = pl.BlockSpec((None, tm, d), lambda bi, i: (bi, i, 0))
    mod_idx = (lambda bi, i: (bi, 0, 0)) if per_batch_mod else (lambda bi, i: (0, 0, 0))
    in_specs = [row_spec, pl.BlockSpec((None, 1, mod.shape[2]), mod_idx), _const_spec((1, d)),
                _const_spec(w1.shape), _const_spec(w2.shape)]
    args = [x, mod, g.reshape(1, d), w1, w2]
    if final_norm:
        in_specs.append(_const_spec((1, d)))
        args.append(final_g.reshape(1, d))
    return pl.pallas_call(
        functools.partial(_mlp_kernel, d=d, ff_block=1024, final_norm=final_norm),
        grid=(b, t // tm),
        in_specs=in_specs,
        out_specs=row_spec,
        out_shape=jax.ShapeDtypeStruct((b, t, d), F32),
        compiler_params=pltpu.CompilerParams(
            dimension_semantics=("parallel", "parallel"), vmem_limit_bytes=VMEM_LIMIT),
        name="mlp",
    )(*args)


def _grid_pos_embed(n_tokens, dim):
    rows = n_tokens // GRID_W
    r = jnp.repeat(jnp.arange(rows, dtype=F32), GRID_W)
    col = jnp.tile(jnp.arange(GRID_W, dtype=F32), rows)
    quarter = dim // 4
    omega = 1.0 / (POS_BASE ** (jnp.arange(quarter, dtype=F32) / quarter))
    ar = r[:, None] * omega
    ac = col[:, None] * omega
    return jnp.concatenate([jnp.sin(ar), jnp.cos(ar), jnp.sin(ac), jnp.cos(ac)], axis=-1)


def _split_w_in(w_in, dk_all, dv_all, s5w, d):
    splits = (dk_all, dk_all, dv_all, dv_all, N_DIR * GATE_RANK, s5w, d, d)
    idx = np.cumsum((0,) + splits)
    parts = [w_in[:, idx[j]:idx[j + 1]] for j in range(len(splits))]
    parts[4] = jnp.pad(parts[4], ((0, 0), (0, LANES - splits[4])))
    return [p.astype(BF16) for p in parts]


def _stream(x, pos, mods, gla_s0, s5_s0, layers, final_g, emit_state, tm):
    b, t, d = x.shape
    gla_states, s5_re, s5_im = [], [], []
    for l, p in enumerate(layers):
        res = _inproj(x, pos if l == 0 else None, mods[l], p["norm1_g"], p["w_in"], tm)
        if l == 0 and pos is not None:
            x = res[0]
            res = res[1:]
        q, k, v, r, glr, u_tm, ga, gb = res
        s5_out = _s5_scan(u_tm, p["s5"], None if s5_s0 is None else s5_s0[l], b, emit_state,
                          tt=min(t, 64))
        gla_out = _gla_scan(q, k, v, glr, p["wup"], p["bg"], p["masks"],
                            None if gla_s0 is None else gla_s0[l], emit_state, tt=min(t, 256))
        if emit_state:
            y, f_re, f_im = s5_out
            o, g_st = gla_out
            gla_states.append(g_st)
            s5_re.append(f_re)
            s5_im.append(f_im)
        else:
            y, o = s5_out, gla_out
        x = _mixout(x, o, r, ga, gb, u_tm, y, mods[l], p["gla_norm_g"], p["s5_d"], p["w_glu"],
                    p["b_glu"], p["w_proj_gla"], p["w_proj_s5"], p["w_out"], tm)
        last = l == len(layers) - 1
        x = _mlp(x, mods[l], p["norm2_g"], p["w_ff1"], p["w_ff2"], final_g if last else None, tm)
    return x, gla_states, s5_re, s5_im


def kernel(x_prompt, x_sample, c, cache_gla_state, state_s5_re, state_s5_im, c_ctx, w_mod, b_mod,
           norm1_g, w_in, w_gate_up, b_gate, gla_norm_g, w_proj_gla, s5_lam_re, s5_lam_im,
           s5_log_step, s5_b_re, s5_b_im, s5_c_re, s5_c_im, s5_d, w_glu, b_glu, w_proj_s5, w_out,
           norm2_g, w_ff1, w_ff2, final_g):
    depth = w_in.shape[0]
    nb, seq, d = x_prompt.shape
    db, dseq, _ = x_sample.shape
    dk_all = w_gate_up.shape[-1]
    dv_all = w_proj_gla.shape[1]
    s5w = s5_d.shape[-1]
    n_groups, n_state = s5_lam_re.shape[2], s5_lam_re.shape[3]
    dk, dv = dk_all // N_HEADS, dv_all // N_HEADS
    assert nb % SUBLANES == 0 and db % SUBLANES == 0

    n_cond = -(-(db + 1) // SUBLANES) * SUBLANES
    conds = jnp.concatenate([c, c_ctx[None], jnp.zeros((n_cond - db - 1, d), F32)], axis=0)
    mod_all = _modulation(conds, w_mod, b_mod)
    mods_lat = [mod_all[l, :db].reshape(db, 1, N_MOD * d) for l in range(depth)]
    mods_ctx = [mod_all[l, db:db + 1].reshape(1, 1, N_MOD * d) for l in range(depth)]

    idx = jnp.arange(CHUNK)
    lower = (idx[None, :] <= idx[:, None]).astype(F32)
    masks = jnp.stack([lower, lower.T])
    layers = []
    for l in range(depth):
        wup = jnp.zeros((N_DIR, LANES, dk_all), F32)
        for dd in range(N_DIR):
            wup = wup.at[dd, dd * GATE_RANK:(dd + 1) * GATE_RANK].set(w_gate_up[l, dd])
        layers.append(dict(
            norm1_g=norm1_g[l], w_in=_split_w_in(w_in[l], dk_all, dv_all, s5w, d),
            wup=wup.astype(BF16), bg=b_gate[l].reshape(N_DIR, 1, dk_all), masks=masks,
            gla_norm_g=gla_norm_g[l], w_proj_gla=w_proj_gla[l].astype(BF16),
            s5=_s5_params(s5_lam_re[l], s5_lam_im[l], s5_log_step[l], s5_b_re[l], s5_b_im[l],
                          s5_c_re[l], s5_c_im[l]),
            s5_d=s5_d[l], w_glu=w_glu[l].astype(BF16), b_glu=b_glu[l],
            w_proj_s5=w_proj_s5[l].astype(BF16), w_out=w_out[l].astype(BF16),
            norm2_g=norm2_g[l], w_ff1=w_ff1[l].astype(BF16), w_ff2=w_ff2[l].astype(BF16)))

    y_prompt, gla_states, s5_re, s5_im = _stream(
        x_prompt, None, mods_ctx, None, None, layers, final_g, True, tm=min(seq, 256))
    new_gla_state = jnp.stack(gla_states, axis=1)
    to_state = lambda a: a.transpose(1, 0, 2).reshape(nb, N_DIR, n_groups, n_state)
    new_s5_re = jnp.stack([to_state(a) for a in s5_re], axis=1)
    new_s5_im = jnp.stack([to_state(a) for a in s5_im], axis=1)

    pos = _grid_pos_embed(dseq, d)
    gla_s0 = [cache_gla_state[:, l] for l in range(depth)]
    from_state = lambda a: a.reshape(db, N_DIR, n_groups * n_state).transpose(1, 0, 2)
    s5_s0 = [(from_state(state_s5_re[:, l]), from_state(state_s5_im[:, l])) for l in range(depth)]
    y_sample, _, _, _ = _stream(
        x_sample, pos, mods_lat, gla_s0, s5_s0, layers, final_g, False, tm=min(dseq, 256))

    return (y_prompt, y_sample, new_gla_state, new_s5_re, new_s5_im)
```

```python
import functools
import math

import jax
import jax.numpy as jnp
import numpy as np
from jax import lax
from jax.experimental import pallas as pl
from jax.experimental.pallas import tpu as pltpu

N_DIR = 2
N_HEADS = 4
GATE_RANK = 16
GATE_TAU = 16.0
CHUNK = 64
S5_GROUP = 16
S5_STATE = 64
N_MOD = 6
EPS = 1e-6
GRID_W = 64
POS_BASE = 10000.0

LANES = 128
SUBLANES = 8
VMEM_LIMIT = 56 * 1024 * 1024

F32 = jnp.float32
BF16 = jnp.bfloat16


def _const_spec(shape):
    nd = len(shape)
    return pl.BlockSpec(shape, lambda *_: (0,) * nd, pipeline_mode=pl.Buffered(1))


def _dot(a, b):
    return jnp.dot(a, b, preferred_element_type=F32)


def _rms_mod(x, g, scale, shift):
    ms = jnp.mean(jnp.square(x), axis=-1, keepdims=True)
    return x * lax.rsqrt(ms + EPS) * g * (1.0 + scale) + shift


def _silu(x):
    return x * jax.nn.sigmoid(x)


def _mod_kernel(c_ref, w_ref, b_ref, o_ref):
    a = _silu(c_ref[...]).astype(BF16)
    o_ref[...] = _dot(a, w_ref[...].astype(BF16)) + b_ref[...]


def _modulation(conds, w_mod, b_mod):
    depth, d, dm = w_mod.shape
    rows = conds.shape[0]
    tn = 1024
    return pl.pallas_call(
        _mod_kernel,
        grid=(depth, dm // tn),
        in_specs=[
            pl.BlockSpec((rows, d), lambda l, j: (0, 0)),
            pl.BlockSpec((None, d, tn), lambda l, j: (l, 0, j)),
            pl.BlockSpec((None, 1, tn), lambda l, j: (l, 0, j)),
        ],
        out_specs=pl.BlockSpec((None, rows, tn), lambda l, j: (l, 0, j)),
        out_shape=jax.ShapeDtypeStruct((depth, rows, dm), F32),
        compiler_params=pltpu.CompilerParams(vmem_limit_bytes=VMEM_LIMIT),
        name="modulation",
    )(conds, w_mod, b_mod.reshape(depth, 1, dm))


def _inproj_kernel(*refs, d, add_pos):
    if add_pos:
        x_ref, pos_ref, mod_ref, g_ref = refs[:4]
        rest = refs[4:]
    else:
        x_ref, mod_ref, g_ref = refs[:3]
        rest = refs[3:]
    w_refs = rest[:8]
    out_refs = rest[8:]
    x = x_ref[...]
    if add_pos:
        x = x + pos_ref[...]
        out_refs[0][...] = x
        out_refs = out_refs[1:]
    shift = mod_ref[:, 0:d]
    scale = mod_ref[:, d:2 * d]
    h = _rms_mod(x, g_ref[...], scale, shift).astype(BF16)
    for w_ref, o_ref in zip(w_refs, out_refs):
        o_ref[...] = _dot(h, w_ref[...])


def _inproj(x, pos, mod, g, weights, tm):
    b, t, d = x.shape
    add_pos = pos is not None
    per_batch_mod = mod.shape[0] != 1
    widths = [w.shape[1] for w in weights]
    row_spec = lambda n: pl.BlockSpec((None, tm, n), lambda bi, i: (bi, i, 0))
    in_specs = [row_spec(d)]
    args = [x]
    if add_pos:
        in_specs.append(pl.BlockSpec((tm, d), lambda bi, i: (i, 0)))
        args.append(pos)
    mod_idx = (lambda bi, i: (bi, 0, 0)) if per_batch_mod else (lambda bi, i: (0, 0, 0))
    in_specs += [pl.BlockSpec((None, 1, mod.shape[2]), mod_idx), _const_spec((1, d))]
    args += [mod, g.reshape(1, d)]
    in_specs += [_const_spec(w.shape) for w in weights]
    args += list(weights)

    out_specs, out_shapes = [], []
    if add_pos:
        out_specs.append(row_spec(d))
        out_shapes.append(jax.ShapeDtypeStruct((b, t, d), F32))
    for idx, n in enumerate(widths):
        if idx == 5:
            out_specs.append(pl.BlockSpec((tm, n), lambda bi, i: (i, bi)))
            out_shapes.append(jax.ShapeDtypeStruct((t, b * n), F32))
        else:
            out_specs.append(row_spec(n))
            out_shapes.append(jax.ShapeDtypeStruct((b, t, n), F32))
    return pl.pallas_call(
        functools.partial(_inproj_kernel, d=d, add_pos=add_pos),
        grid=(b, t // tm),
        in_specs=in_specs,
        out_specs=out_specs,
        out_shape=out_shapes,
        compiler_params=pltpu.CompilerParams(
            dimension_semantics=("parallel", "parallel"), vmem_limit_bytes=VMEM_LIMIT),
        name="inproj",
    )(*args)


def _s5_kernel(*refs, tt, n_kb, zero_init, emit_state, scan_lanes):
    if zero_init:
        u_ref, lr_ref, li_ref, bre_ref, bim_ref, cre_ref, cim_ref = refs[:7]
        rest = refs[7:]
    else:
        u_ref, lr_ref, li_ref, bre_ref, bim_ref, cre_ref, cim_ref, s0r_ref, s0i_ref = refs[:9]
        rest = refs[9:]
    if emit_state:
        y_ref, fr_ref, fi_ref, sre, sim, car, cai = rest
    else:
        y_ref, sre, sim, car, cai = rest
    direction = pl.program_id(1)
    i = pl.program_id(2)
    rows = tt * SUBLANES
    kw = u_ref.shape[-1] // n_kb
    sw = sre.shape[-1] // n_kb

    @pl.when(i == 0)
    def _():
        if zero_init:
            car[...] = jnp.zeros_like(car)
            cai[...] = jnp.zeros_like(cai)
        else:
            car[...] = s0r_ref[...]
            cai[...] = s0i_ref[...]

    u = u_ref[...].reshape(rows, u_ref.shape[-1]).astype(BF16)
    for kb in range(n_kb):
        ub = u[:, kb * kw:(kb + 1) * kw]
        sre[:, kb * sw:(kb + 1) * sw] = _dot(ub, bre_ref[kb])
        sim[:, kb * sw:(kb + 1) * sw] = _dot(ub, bim_ref[kb])

    for lb in range(sre.shape[-1] // scan_lanes):
        ls = slice(lb * scan_lanes, (lb + 1) * scan_lanes)
        lr = lr_ref[:, ls]
        li = li_ref[:, ls]

        def step(j, carry, ls=ls, lr=lr, li=li):
            cr, ci = carry
            t = j + direction * (tt - 1 - 2 * j)
            r0 = pl.multiple_of(t * SUBLANES, SUBLANES)
            nr = lr * cr - li * ci + sre[pl.ds(r0, SUBLANES), ls]
            ni = lr * ci + li * cr + sim[pl.ds(r0, SUBLANES), ls]
            sre[pl.ds(r0, SUBLANES), ls] = nr
            sim[pl.ds(r0, SUBLANES), ls] = ni
            return nr, ni

        cr, ci = lax.fori_loop(0, tt, step, (car[:, ls], cai[:, ls]), unroll=8)
        car[:, ls] = cr
        cai[:, ls] = ci

    for kb in range(n_kb):
        y = (_dot(sre[:, kb * sw:(kb + 1) * sw].astype(BF16), cre_ref[kb])
             + _dot(sim[:, kb * sw:(kb + 1) * sw].astype(BF16), cim_ref[kb]))
        y_ref[:, :, kb * kw:(kb + 1) * kw] = y.reshape(tt, SUBLANES, kw)

    if emit_state:
        @pl.when(i == pl.num_programs(2) - 1)
        def _():
            fr_ref[...] = car[...]
            fi_ref[...] = cai[...]


def _s5_scan(u_tm, prm, s0, b, emit_state, tt):
    t = u_tm.shape[0]
    w = u_tm.shape[1] // b
    lr, li, bre, bim, cre, cim = prm
    n_kb = bre.shape[1]
    ns = lr.shape[-1]
    n_t = t // tt
    n_g = b // SUBLANES
    zero_init = s0 is None
    tile = lambda g, d, i: i + d * (n_t - 1 - 2 * i)
    dir_spec = lambda shape: pl.BlockSpec((None,) + shape, lambda g, d, i: (d,) + (0,) * len(shape),
                                          pipeline_mode=pl.Buffered(1))
    in_specs = [
        pl.BlockSpec((tt, SUBLANES, w), lambda g, d, i: (tile(g, d, i), g, 0)),
        dir_spec((SUBLANES, ns)), dir_spec((SUBLANES, ns)),
        dir_spec(bre.shape[1:]), dir_spec(bim.shape[1:]),
        dir_spec(cre.shape[1:]), dir_spec(cim.shape[1:]),
    ]
    args = [u_tm.reshape(t, b, w), lr, li, bre, bim, cre, cim]
    state_spec = pl.BlockSpec((None, SUBLANES, ns), lambda g, d, i: (d, g, 0))
    if not zero_init:
        in_specs += [state_spec, state_spec]
        args += list(s0)
    out_specs = [pl.BlockSpec((None, tt, SUBLANES, w), lambda g, d, i: (d, tile(g, d, i), g, 0))]
    out_shapes = [jax.ShapeDtypeStruct((N_DIR, t, b, w), F32)]
    if emit_state:
        out_specs += [state_spec, state_spec]
        out_shapes += [jax.ShapeDtypeStruct((N_DIR, b, ns), F32)] * 2
    res = pl.pallas_call(
        functools.partial(_s5_kernel, tt=tt, n_kb=n_kb, zero_init=zero_init,
                          emit_state=emit_state, scan_lanes=4 * LANES),
        grid=(n_g, N_DIR, n_t),
        in_specs=in_specs,
        out_specs=out_specs,
        out_shape=out_shapes,
        scratch_shapes=[pltpu.VMEM((tt * SUBLANES, ns), F32), pltpu.VMEM((tt * SUBLANES, ns), F32),
                        pltpu.VMEM((SUBLANES, ns), F32), pltpu.VMEM((SUBLANES, ns), F32)],
        compiler_params=pltpu.CompilerParams(
            dimension_semantics=("parallel", "parallel", "arbitrary"), vmem_limit_bytes=VMEM_LIMIT),
        name="s5_scan",
    )(*args)
    y = res[0].reshape(N_DIR, t, b * w)
    if emit_state:
        return y, res[1], res[2]
    return y


def _s5_params(lam_re, lam_im, log_step, b_re, b_im, c_re, c_im):
    n_dir, g, p = lam_re.shape
    n = b_re.shape[-1]
    gpb = LANES // n
    n_kb = g // gpb
    step = jnp.exp(log_step)[..., None]
    mag = jnp.exp(lam_re * step)
    lbr = mag * jnp.cos(lam_im * step)
    lbi = mag * jnp.sin(lam_im * step)
    den = lam_re * lam_re + lam_im * lam_im
    fr = ((lbr - 1.0) * lam_re + lbi * lam_im) / den
    fi = (lbi * lam_re - (lbr - 1.0) * lam_im) / den
    bbr = fr[..., None] * b_re - fi[..., None] * b_im
    bbi = fr[..., None] * b_im + fi[..., None] * b_re
    eye = jnp.eye(gpb, dtype=F32)

    def pack_b(m):
        m = m.reshape(n_dir, n_kb, gpb, p, n).transpose(0, 1, 2, 4, 3)
        m = m[:, :, :, :, None, :] * eye[None, None, :, None, :, None]
        return m.reshape(n_dir, n_kb, gpb * n, gpb * p).astype(BF16)

    def pack_c(m):
        m = m.reshape(n_dir, n_kb, gpb, n, p).transpose(0, 1, 2, 4, 3)
        m = m[:, :, :, :, None, :] * eye[None, None, :, None, :, None]
        return m.reshape(n_dir, n_kb, gpb * p, gpb * n).astype(BF16)

    bcast = lambda a: jnp.broadcast_to(a.reshape(n_dir, 1, g * p), (n_dir, SUBLANES, g * p))
    return (bcast(lbr), bcast(lbi), pack_b(bbr), pack_b(bbi), pack_c(c_re), pack_c(-c_im))


def _log_sigmoid(z):
    return -(jnp.maximum(-z, 0.0) + jnp.log(1.0 + jnp.exp(-jnp.abs(z))))


def _split3(x):
    hi = x.astype(BF16)
    r1 = x - hi.astype(F32)
    mid = r1.astype(BF16)
    lo = (r1 - mid.astype(F32)).astype(BF16)
    return hi, mid, lo


def _gla_kernel(*refs, n_chunks, dk, dv, zero_init, emit_state):
    dir_refs = (refs[0:4], refs[4:8])
    wup_ref, bg_ref = refs[8:10]
    rest = refs[10:]
    if not zero_init:
        s0_ref = rest[0]
        rest = rest[1:]
    if emit_state:
        of_ref, ob_ref, sf_ref, st = rest
    else:
        of_ref, ob_ref, st = rest
    o_refs = (of_ref, ob_ref)
    i = pl.program_id(1)
    scale = dk ** -0.5

    @pl.when(i == 0)
    def _():
        for d in range(N_DIR):
            for h in range(N_HEADS):
                if zero_init:
                    st[d, h] = jnp.zeros((dv, dk), F32)
                else:
                    st[d, h] = s0_ref[d, h].T

    row = lax.broadcasted_iota(jnp.int32, (CHUNK, CHUNK), 0)
    col = lax.broadcasted_iota(jnp.int32, (CHUNK, CHUNK), 1)
    causal = (col <= row, col >= row)
    ones = tuple(jnp.where(c, 1.0, 0.0).astype(BF16) for c in causal)

    def chunk(d, c):
        q_ref, k_ref, v_ref, glr_ref = dir_refs[d]
        o_ref = o_refs[d]
        rows = pl.ds(pl.multiple_of(c * CHUNK, CHUNK), CHUNK)
        z = _dot(glr_ref[rows, :].astype(BF16), wup_ref[d]) + bg_ref[d]
        g = _log_sigmoid(z) / GATE_TAU
        cum = sum(_dot(ones[d], part) for part in _split3(g))
        total = jnp.sum(g, axis=0, keepdims=True)
        q_dec = (q_ref[rows, :] * scale * jnp.exp(cum)).astype(BF16)
        kk = k_ref[rows, :]
        k_in = (kk * jnp.exp(-cum)).astype(BF16)
        k_out = (kk * jnp.exp(total - cum)).astype(BF16)
        decay = jnp.exp(total)
        for h in range(N_HEADS):
            ks = slice(h * dk, (h + 1) * dk)
            vs = slice(h * dv, (h + 1) * dv)
            qh = q_dec[:, ks]
            vh = v_ref[rows, vs].astype(BF16)
            sc = lax.dot_general(qh, k_in[:, ks], (((1,), (1,)), ((), ())),
                                 preferred_element_type=F32)
            sc = jnp.where(causal[d], sc, 0.0).astype(BF16)
            s_t = st[d, h]
            o = _dot(sc, vh) + lax.dot_general(qh, s_t.astype(BF16), (((1,), (1,)), ((), ())),
                                               preferred_element_type=F32)
            o_ref[rows, vs] = o
            upd = lax.dot_general(vh, k_out[:, ks], (((0,), (0,)), ((), ())),
                                  preferred_element_type=F32)
            st[d, h] = decay[:, ks] * s_t + upd

    def body(cc, carry):
        chunk(0, cc)
        chunk(1, n_chunks - 1 - cc)
        return carry

    lax.fori_loop(0, n_chunks, body, 0)

    if emit_state:
        @pl.when(i == pl.num_programs(1) - 1)
        def _():
            for d in range(N_DIR):
                for h in range(N_HEADS):
                    sf_ref[d, h] = st[d, h].T


def _gla_scan(q, k, v, glr, wup, bg, s0, emit_state, tt):
    b, t, dkk = q.shape
    dvv = v.shape[-1]
    dk, dv = dkk // N_HEADS, dvv // N_HEADS
    n_t = t // tt
    zero_init = s0 is None
    fwd_spec = lambda n: pl.BlockSpec((None, tt, n), lambda bi, i: (bi, i, 0))
    bwd_spec = lambda n: pl.BlockSpec((None, tt, n), lambda bi, i: (bi, n_t - 1 - i, 0))
    state_spec = pl.BlockSpec((None, N_DIR, N_HEADS, dk, dv), lambda bi, i: (bi, 0, 0, 0, 0))
    widths = (dkk, dkk, dvv, glr.shape[-1])
    in_specs = ([fwd_spec(n) for n in widths] + [bwd_spec(n) for n in widths]
                + [_const_spec(wup.shape), _const_spec(bg.shape)])
    args = [q, k, v, glr, q, k, v, glr, wup, bg]
    if not zero_init:
        in_specs.append(state_spec)
        args.append(s0)
    out_specs = [fwd_spec(dvv), bwd_spec(dvv)]
    out_shapes = [jax.ShapeDtypeStruct((b, t, dvv), F32)] * 2
    if emit_state:
        out_specs.append(state_spec)
        out_shapes.append(jax.ShapeDtypeStruct((b, N_DIR, N_HEADS, dk, dv), F32))
    return pl.pallas_call(
        functools.partial(_gla_kernel, n_chunks=tt // CHUNK, dk=dk, dv=dv,
                          zero_init=zero_init, emit_state=emit_state),
        grid=(b, n_t),
        in_specs=in_specs,
        out_specs=out_specs,
        out_shape=out_shapes,
        scratch_shapes=[pltpu.VMEM((N_DIR, N_HEADS, dv, dk), F32)],
        compiler_params=pltpu.CompilerParams(
            dimension_semantics=("parallel", "arbitrary"), vmem_limit_bytes=VMEM_LIMIT),
        name="gla_scan",
    )(*args)


def _gelu_tanh(x):
    return 0.5 * x * (1.0 + jnp.tanh(math.sqrt(2.0 / math.pi) * (x + 0.044715 * (x * x * x))))


def _mixout_kernel(x_ref, of_ref, ob_ref, r_ref, ga_ref, gb_ref, u_ref, yf_ref, yb_ref, mod_ref,
                   gn_ref, d_ref, wglu_ref, bglu_ref, wpg_ref, wps_ref, wout_ref, o_ref, *, d, dv):
    o = of_ref[...] + ob_ref[...]
    r = r_ref[...]
    gn = gn_ref[...]
    parts = []
    for h in range(N_HEADS):
        vs = slice(h * dv, (h + 1) * dv)
        oh = o[:, vs]
        ms = jnp.mean(jnp.square(oh), axis=-1, keepdims=True)
        parts.append((oh * lax.rsqrt(ms + EPS) * gn * _silu(r[:, vs])).astype(BF16))
    pg = _dot(jnp.concatenate(parts, axis=-1), wpg_ref[...])

    u = u_ref[...]
    y = _gelu_tanh(yf_ref[...] + yb_ref[...] + d_ref[...] * u)
    y = y * jax.nn.sigmoid(_dot(y.astype(BF16), wglu_ref[...]) + bglu_ref[...])
    ps = _dot(y.astype(BF16), wps_ref[...])

    merged = jax.nn.sigmoid(ga_ref[...]) * pg + jax.nn.sigmoid(gb_ref[...]) * ps
    gate = mod_ref[:, 2 * d:3 * d]
    o_ref[...] = x_ref[...] + gate * _dot(merged.astype(BF16), wout_ref[...])


def _mixout(x, o_f, o_b, r, ga, gb, u_tm, y, mod, gn, s5d, wglu, bglu, wpg, wps, wout, tm):
    b, t, d = x.shape
    dvv = r.shape[-1]
    w = u_tm.shape[1] // b
    per_batch_mod = mod.shape[0] != 1
    row_spec = lambda n: pl.BlockSpec((None, tm, n), lambda bi, i: (bi, i, 0))
    mod_idx = (lambda bi, i: (bi, 0, 0)) if per_batch_mod else (lambda bi, i: (0, 0, 0))
    in_specs = [
        row_spec(d), row_spec(dvv), row_spec(dvv), row_spec(dvv), row_spec(d), row_spec(d),
        pl.BlockSpec((tm, w), lambda bi, i: (i, bi)),
        pl.BlockSpec((None, tm, w), lambda bi, i: (0, i, bi)),
        pl.BlockSpec((None, tm, w), lambda bi, i: (1, i, bi)),
        pl.BlockSpec((None, 1, mod.shape[2]), mod_idx),
        _const_spec((1, dvv // N_HEADS)), _const_spec((1, w)),
        _const_spec(wglu.shape), _const_spec((1, w)),
        _const_spec(wpg.shape), _const_spec(wps.shape), _const_spec(wout.shape),
    ]
    return pl.pallas_call(
        functools.partial(_mixout_kernel, d=d, dv=dvv // N_HEADS),
        grid=(b, t // tm),
        in_specs=in_specs,
        out_specs=row_spec(d),
        out_shape=jax.ShapeDtypeStruct((b, t, d), F32),
        compiler_params=pltpu.CompilerParams(
            dimension_semantics=("parallel", "parallel"), vmem_limit_bytes=VMEM_LIMIT),
        name="mixout",
    )(x, o_f, o_b, r, ga, gb, u_tm, y, y, mod, gn.reshape(1, -1), s5d.reshape(1, -1),
      wglu, bglu.reshape(1, -1), wpg, wps, wout)


def _mlp_kernel(*refs, d, ff_block, final_norm):
    if final_norm:
        x_ref, mod_ref, g_ref, w1_ref, w2_ref, fg_ref, o_ref = refs
    else:
        x_ref, mod_ref, g_ref, w1_ref, w2_ref, o_ref = refs
    x = x_ref[...]
    shift = mod_ref[:, 3 * d:4 * d]
    scale = mod_ref[:, 4 * d:5 * d]
    gate = mod_ref[:, 5 * d:6 * d]
    h = _rms_mod(x, g_ref[...], scale, shift).astype(BF16)
    acc = jnp.zeros(x.shape, F32)
    for j in range(w1_ref.shape[1] // ff_block):
        cs = slice(j * ff_block, (j + 1) * ff_block)
        a = jnp.square(jnp.maximum(_dot(h, w1_ref[:, cs]), 0.0)).astype(BF16)
        acc = acc + _dot(a, w2_ref[cs, :])
    x = x + gate * acc
    if final_norm:
        ms = jnp.mean(jnp.square(x), axis=-1, keepdims=True)
        x = x * lax.rsqrt(ms + EPS) * fg_ref[...]
    o_ref[...] = x


def _mlp(x, mod, g, w1, w2, final_g, tm):
    b, t, d = x.shape
    per_batch_mod = mod.shape[0] != 1
    final_norm = final_g is not None
    row_spec = pl.BlockSpec((None, tm, d), lambda bi, i: (bi, i, 0))
    mod_idx = (lambda bi, i: (bi, 0, 0)) if per_batch_mod else (lambda bi, i: (0, 0, 0))
    in_specs = [row_spec, pl.BlockSpec((None, 1, mod.shape[2]), mod_idx), _const_spec((1, d)),
                _const_spec(w1.shape), _const_spec(w2.shape)]
    args = [x, mod, g.reshape(1, d), w1, w2]
    if final_norm:
        in_specs.append(_const_spec((1, d)))
        args.append(final_g.reshape(1, d))
    return pl.pallas_call(
        functools.partial(_mlp_kernel, d=d, ff_block=1024, final_norm=final_norm),
        grid=(b, t // tm),
        in_specs=in_specs,
        out_specs=row_spec,
        out_shape=jax.ShapeDtypeStruct((b, t, d), F32),
        compiler_params=pltpu.CompilerParams(
            dimension_semantics=("parallel", "parallel"), vmem_limit_bytes=VMEM_LIMIT),
        name="mlp",
    )(*args)


def _grid_pos_embed(n_tokens, dim):
    rows = n_tokens // GRID_W
    r = jnp.repeat(jnp.arange(rows, dtype=F32), GRID_W)
    col = jnp.tile(jnp.arange(GRID_W, dtype=F32), rows)
    quarter = dim // 4
    omega = 1.0 / (POS_BASE ** (jnp.arange(quarter, dtype=F32) / quarter))
    ar = r[:, None] * omega
    ac = col[:, None] * omega
    return jnp.concatenate([jnp.sin(ar), jnp.cos(ar), jnp.sin(ac), jnp.cos(ac)], axis=-1)


def _split_w_in(w_in, dk_all, dv_all, s5w, d):
    splits = (dk_all, dk_all, dv_all, dv_all, N_DIR * GATE_RANK, s5w, d, d)
    idx = np.cumsum((0,) + splits)
    parts = [w_in[:, idx[j]:idx[j + 1]] for j in range(len(splits))]
    parts[4] = jnp.pad(parts[4], ((0, 0), (0, LANES - splits[4])))
    return [p.astype(BF16) for p in parts]


def _stream(x, pos, mods, gla_s0, s5_s0, layers, final_g, emit_state, tm):
    b, t, d = x.shape
    gla_states, s5_re, s5_im = [], [], []
    for l, p in enumerate(layers):
        res = _inproj(x, pos if l == 0 else None, mods[l], p["norm1_g"], p["w_in"], tm)
        if l == 0 and pos is not None:
            x = res[0]
            res = res[1:]
        q, k, v, r, glr, u_tm, ga, gb = res
        s5_out = _s5_scan(u_tm, p["s5"], None if s5_s0 is None else s5_s0[l], b, emit_state,
                          tt=min(t, 64))
        gla_out = _gla_scan(q, k, v, glr, p["wup"], p["bg"],
                            None if gla_s0 is None else gla_s0[l], emit_state, tt=min(t, 256))
        o_f, o_b = gla_out[:2]
        if emit_state:
            y, f_re, f_im = s5_out
            gla_states.append(gla_out[2])
            s5_re.append(f_re)
            s5_im.append(f_im)
        else:
            y = s5_out
        x = _mixout(x, o_f, o_b, r, ga, gb, u_tm, y, mods[l], p["gla_norm_g"], p["s5_d"], p["w_glu"],
                    p["b_glu"], p["w_proj_gla"], p["w_proj_s5"], p["w_out"], tm)
        last = l == len(layers) - 1
        x = _mlp(x, mods[l], p["norm2_g"], p["w_ff1"], p["w_ff2"], final_g if last else None, tm)
    return x, gla_states, s5_re, s5_im


def kernel(x_prompt, x_sample, c, cache_gla_state, state_s5_re, state_s5_im, c_ctx, w_mod, b_mod,
           norm1_g, w_in, w_gate_up, b_gate, gla_norm_g, w_proj_gla, s5_lam_re, s5_lam_im,
           s5_log_step, s5_b_re, s5_b_im, s5_c_re, s5_c_im, s5_d, w_glu, b_glu, w_proj_s5, w_out,
           norm2_g, w_ff1, w_ff2, final_g):
    depth = w_in.shape[0]
    nb, seq, d = x_prompt.shape
    db, dseq, _ = x_sample.shape
    dk_all = w_gate_up.shape[-1]
    dv_all = w_proj_gla.shape[1]
    s5w = s5_d.shape[-1]
    n_groups, n_state = s5_lam_re.shape[2], s5_lam_re.shape[3]
    dk, dv = dk_all // N_HEADS, dv_all // N_HEADS
    assert nb % SUBLANES == 0 and db % SUBLANES == 0

    n_cond = -(-(db + 1) // SUBLANES) * SUBLANES
    conds = jnp.concatenate([c, c_ctx[None], jnp.zeros((n_cond - db - 1, d), F32)], axis=0)
    mod_all = _modulation(conds, w_mod, b_mod)
    mods_lat = [mod_all[l, :db].reshape(db, 1, N_MOD * d) for l in range(depth)]
    mods_ctx = [mod_all[l, db:db + 1].reshape(1, 1, N_MOD * d) for l in range(depth)]

    layers = []
    for l in range(depth):
        wup = jnp.zeros((N_DIR, LANES, dk_all), F32)
        for dd in range(N_DIR):
            wup = wup.at[dd, dd * GATE_RANK:(dd + 1) * GATE_RANK].set(w_gate_up[l, dd])
        layers.append(dict(
            norm1_g=norm1_g[l], w_in=_split_w_in(w_in[l], dk_all, dv_all, s5w, d),
            wup=wup.astype(BF16), bg=b_gate[l].reshape(N_DIR, 1, dk_all),
            gla_norm_g=gla_norm_g[l], w_proj_gla=w_proj_gla[l].astype(BF16),
            s5=_s5_params(s5_lam_re[l], s5_lam_im[l], s5_log_step[l], s5_b_re[l], s5_b_im[l],
                          s5_c_re[l], s5_c_im[l]),
            s5_d=s5_d[l], w_glu=w_glu[l].astype(BF16), b_glu=b_glu[l],
            w_proj_s5=w_proj_s5[l].astype(BF16), w_out=w_out[l].astype(BF16),
            norm2_g=norm2_g[l], w_ff1=w_ff1[l].astype(BF16), w_ff2=w_ff2[l].astype(BF16)))

    y_prompt, gla_states, s5_re, s5_im = _stream(
        x_prompt, None, mods_ctx, None, None, layers, final_g, True, tm=min(seq, 256))
    new_gla_state = jnp.stack(gla_states, axis=1)
    to_state = lambda a: a.transpose(1, 0, 2).reshape(nb, N_DIR, n_groups, n_state)
    new_s5_re = jnp.stack([to_state(a) for a in s5_re], axis=1)
    new_s5_im = jnp.stack([to_state(a) for a in s5_im], axis=1)

    pos = _grid_pos_embed(dseq, d)
    gla_s0 = [cache_gla_state[:, l] for l in range(depth)]
    from_state = lambda a: a.reshape(db, N_DIR, n_groups * n_state).transpose(1, 0, 2)
    s5_s0 = [(from_state(state_s5_re[:, l]), from_state(state_s5_im[:, l])) for l in range(depth)]
    y_sample, _, _, _ = _stream(
        x_sample, pos, mods_lat, gla_s0, s5_s0, layers, final_g, False, tm=min(dseq, 256))

    return (y_prompt, y_sample, new_gla_state, new_s5_re, new_s5_im)
```

```python
import functools
import math

import jax
import jax.numpy as jnp
import numpy as np
from jax import lax
from jax.experimental import pallas as pl
from jax.experimental.pallas import tpu as pltpu

N_DIR = 2
N_HEADS = 4
GATE_RANK = 16
GATE_TAU = 16.0
CHUNK = 64
S5_GROUP = 16
S5_STATE = 64
N_MOD = 6
EPS = 1e-6
GRID_W = 64
POS_BASE = 10000.0

LANES = 128
SUBLANES = 8
VMEM_LIMIT = 56 * 1024 * 1024

F32 = jnp.float32
BF16 = jnp.bfloat16


def _const_spec(shape):
    nd = len(shape)
    return pl.BlockSpec(shape, lambda *_: (0,) * nd, pipeline_mode=pl.Buffered(1))


def _dot(a, b):
    return jnp.dot(a, b, preferred_element_type=F32)


def _rms_mod(x, g, scale, shift):
    ms = jnp.mean(jnp.square(x), axis=-1, keepdims=True)
    return x * lax.rsqrt(ms + EPS) * g * (1.0 + scale) + shift


def _silu(x):
    return x * jax.nn.sigmoid(x)


def _mod_kernel(c_ref, w_ref, b_ref, o_ref):
    a = _silu(c_ref[...]).astype(BF16)
    o_ref[...] = _dot(a, w_ref[...].astype(BF16)) + b_ref[...]


def _modulation(conds, w_mod, b_mod):
    depth, d, dm = w_mod.shape
    rows = conds.shape[0]
    tn = 1024
    return pl.pallas_call(
        _mod_kernel,
        grid=(depth, dm // tn),
        in_specs=[
            pl.BlockSpec((rows, d), lambda l, j: (0, 0)),
            pl.BlockSpec((None, d, tn), lambda l, j: (l, 0, j)),
            pl.BlockSpec((None, 1, tn), lambda l, j: (l, 0, j)),
        ],
        out_specs=pl.BlockSpec((None, rows, tn), lambda l, j: (l, 0, j)),
        out_shape=jax.ShapeDtypeStruct((depth, rows, dm), F32),
        compiler_params=pltpu.CompilerParams(vmem_limit_bytes=VMEM_LIMIT),
        name="modulation",
    )(conds, w_mod, b_mod.reshape(depth, 1, dm))


def _inproj_kernel(*refs, d, add_pos):
    if add_pos:
        x_ref, pos_ref, mod_ref, g_ref = refs[:4]
        rest = refs[4:]
    else:
        x_ref, mod_ref, g_ref = refs[:3]
        rest = refs[3:]
    w_refs = rest[:8]
    out_refs = rest[8:]
    x = x_ref[...]
    if add_pos:
        x = x + pos_ref[...]
        out_refs[0][...] = x
        out_refs = out_refs[1:]
    shift = mod_ref[:, 0:d]
    scale = mod_ref[:, d:2 * d]
    h = _rms_mod(x, g_ref[...], scale, shift).astype(BF16)
    for w_ref, o_ref in zip(w_refs, out_refs):
        o_ref[...] = _dot(h, w_ref[...])


def _inproj(x, pos, mod, g, weights, tm):
    b, t, d = x.shape
    add_pos = pos is not None
    per_batch_mod = mod.shape[0] != 1
    widths = [w.shape[1] for w in weights]
    row_spec = lambda n: pl.BlockSpec((None, tm, n), lambda bi, i: (bi, i, 0))
    in_specs = [row_spec(d)]
    args = [x]
    if add_pos:
        in_specs.append(pl.BlockSpec((tm, d), lambda bi, i: (i, 0)))
        args.append(pos)
    mod_idx = (lambda bi, i: (bi, 0, 0)) if per_batch_mod else (lambda bi, i: (0, 0, 0))
    in_specs += [pl.BlockSpec((None, 1, mod.shape[2]), mod_idx), _const_spec((1, d))]
    args += [mod, g.reshape(1, d)]
    in_specs += [_const_spec(w.shape) for w in weights]
    args += list(weights)

    out_specs, out_shapes = [], []
    if add_pos:
        out_specs.append(row_spec(d))
        out_shapes.append(jax.ShapeDtypeStruct((b, t, d), F32))
    for idx, n in enumerate(widths):
        if idx == 5:
            out_specs.append(pl.BlockSpec((tm, n), lambda bi, i: (i, bi)))
            out_shapes.append(jax.ShapeDtypeStruct((t, b * n), F32))
        else:
            out_specs.append(row_spec(n))
            out_shapes.append(jax.ShapeDtypeStruct((b, t, n), F32))
    return pl.pallas_call(
        functools.partial(_inproj_kernel, d=d, add_pos=add_pos),
        grid=(b, t // tm),
        in_specs=in_specs,
        out_specs=out_specs,
        out_shape=out_shapes,
        compiler_params=pltpu.CompilerParams(
            dimension_semantics=("parallel", "parallel"), vmem_limit_bytes=VMEM_LIMIT),
        name="inproj",
    )(*args)


def _s5_kernel(*refs, tt, n_kb, zero_init, emit_state, scan_lanes):
    if zero_init:
        u_ref, lr_ref, li_ref, bre_ref, bim_ref, cre_ref, cim_ref = refs[:7]
        rest = refs[7:]
    else:
        u_ref, lr_ref, li_ref, bre_ref, bim_ref, cre_ref, cim_ref, s0r_ref, s0i_ref = refs[:9]
        rest = refs[9:]
    if emit_state:
        y_ref, fr_ref, fi_ref, sre, sim, car, cai = rest
    else:
        y_ref, sre, sim, car, cai = rest
    direction = pl.program_id(1)
    i = pl.program_id(2)
    rows = tt * SUBLANES
    kw = u_ref.shape[-1] // n_kb
    sw = sre.shape[-1] // n_kb

    @pl.when(i == 0)
    def _():
        if zero_init:
            car[...] = jnp.zeros_like(car)
            cai[...] = jnp.zeros_like(cai)
        else:
            car[...] = s0r_ref[...]
            cai[...] = s0i_ref[...]

    u = u_ref[...].reshape(rows, u_ref.shape[-1]).astype(BF16)
    for kb in range(n_kb):
        ub = u[:, kb * kw:(kb + 1) * kw]
        sre[:, kb * sw:(kb + 1) * sw] = _dot(ub, bre_ref[kb])
        sim[:, kb * sw:(kb + 1) * sw] = _dot(ub, bim_ref[kb])

    for lb in range(sre.shape[-1] // scan_lanes):
        ls = slice(lb * scan_lanes, (lb + 1) * scan_lanes)
        lr = lr_ref[:, ls]
        li = li_ref[:, ls]

        def step(j, carry, ls=ls, lr=lr, li=li):
            cr, ci = carry
            t = j + direction * (tt - 1 - 2 * j)
            r0 = pl.multiple_of(t * SUBLANES, SUBLANES)
            nr = lr * cr - li * ci + sre[pl.ds(r0, SUBLANES), ls]
            ni = lr * ci + li * cr + sim[pl.ds(r0, SUBLANES), ls]
            sre[pl.ds(r0, SUBLANES), ls] = nr
            sim[pl.ds(r0, SUBLANES), ls] = ni
            return nr, ni

        cr, ci = lax.fori_loop(0, tt, step, (car[:, ls], cai[:, ls]), unroll=8)
        car[:, ls] = cr
        cai[:, ls] = ci

    for kb in range(n_kb):
        y = (_dot(sre[:, kb * sw:(kb + 1) * sw].astype(BF16), cre_ref[kb])
             + _dot(sim[:, kb * sw:(kb + 1) * sw].astype(BF16), cim_ref[kb]))
        y_ref[:, :, kb * kw:(kb + 1) * kw] = y.reshape(tt, SUBLANES, kw)

    if emit_state:
        @pl.when(i == pl.num_programs(2) - 1)
        def _():
            fr_ref[...] = car[...]
            fi_ref[...] = cai[...]


def _s5_scan(u_tm, prm, s0, b, emit_state, tt):
    t = u_tm.shape[0]
    w = u_tm.shape[1] // b
    lr, li, bre, bim, cre, cim = prm
    n_kb = bre.shape[1]
    ns = lr.shape[-1]
    n_t = t // tt
    n_g = b // SUBLANES
    zero_init = s0 is None
    tile = lambda g, d, i: i + d * (n_t - 1 - 2 * i)
    dir_spec = lambda shape: pl.BlockSpec((None,) + shape, lambda g, d, i: (d,) + (0,) * len(shape),
                                          pipeline_mode=pl.Buffered(1))
    in_specs = [
        pl.BlockSpec((tt, SUBLANES, w), lambda g, d, i: (tile(g, d, i), g, 0)),
        dir_spec((SUBLANES, ns)), dir_spec((SUBLANES, ns)),
        dir_spec(bre.shape[1:]), dir_spec(bim.shape[1:]),
        dir_spec(cre.shape[1:]), dir_spec(cim.shape[1:]),
    ]
    args = [u_tm.reshape(t, b, w), lr, li, bre, bim, cre, cim]
    state_spec = pl.BlockSpec((None, SUBLANES, ns), lambda g, d, i: (d, g, 0))
    if not zero_init:
        in_specs += [state_spec, state_spec]
        args += list(s0)
    out_specs = [pl.BlockSpec((None, tt, SUBLANES, w), lambda g, d, i: (d, tile(g, d, i), g, 0))]
    out_shapes = [jax.ShapeDtypeStruct((N_DIR, t, b, w), F32)]
    if emit_state:
        out_specs += [state_spec, state_spec]
        out_shapes += [jax.ShapeDtypeStruct((N_DIR, b, ns), F32)] * 2
    res = pl.pallas_call(
        functools.partial(_s5_kernel, tt=tt, n_kb=n_kb, zero_init=zero_init,
                          emit_state=emit_state, scan_lanes=4 * LANES),
        grid=(n_g, N_DIR, n_t),
        in_specs=in_specs,
        out_specs=out_specs,
        out_shape=out_shapes,
        scratch_shapes=[pltpu.VMEM((tt * SUBLANES, ns), F32), pltpu.VMEM((tt * SUBLANES, ns), F32),
                        pltpu.VMEM((SUBLANES, ns), F32), pltpu.VMEM((SUBLANES, ns), F32)],
        compiler_params=pltpu.CompilerParams(
            dimension_semantics=("parallel", "parallel", "arbitrary"), vmem_limit_bytes=VMEM_LIMIT),
        name="s5_scan",
    )(*args)
    y = res[0].reshape(N_DIR, t, b * w)
    if emit_state:
        return y, res[1], res[2]
    return y


def _s5_params(lam_re, lam_im, log_step, b_re, b_im, c_re, c_im):
    n_dir, g, p = lam_re.shape
    n = b_re.shape[-1]
    gpb = LANES // n
    n_kb = g // gpb
    step = jnp.exp(log_step)[..., None]
    mag = jnp.exp(lam_re * step)
    lbr = mag * jnp.cos(lam_im * step)
    lbi = mag * jnp.sin(lam_im * step)
    den = lam_re * lam_re + lam_im * lam_im
    fr = ((lbr - 1.0) * lam_re + lbi * lam_im) / den
    fi = (lbi * lam_re - (lbr - 1.0) * lam_im) / den
    bbr = fr[..., None] * b_re - fi[..., None] * b_im
    bbi = fr[..., None] * b_im + fi[..., None] * b_re
    eye = jnp.eye(gpb, dtype=F32)

    def pack_b(m):
        m = m.reshape(n_dir, n_kb, gpb, p, n).transpose(0, 1, 2, 4, 3)
        m = m[:, :, :, :, None, :] * eye[None, None, :, None, :, None]
        return m.reshape(n_dir, n_kb, gpb * n, gpb * p).astype(BF16)

    def pack_c(m):
        m = m.reshape(n_dir, n_kb, gpb, n, p).transpose(0, 1, 2, 4, 3)
        m = m[:, :, :, :, None, :] * eye[None, None, :, None, :, None]
        return m.reshape(n_dir, n_kb, gpb * p, gpb * n).astype(BF16)

    bcast = lambda a: jnp.broadcast_to(a.reshape(n_dir, 1, g * p), (n_dir, SUBLANES, g * p))
    return (bcast(lbr), bcast(lbi), pack_b(bbr), pack_b(bbi), pack_c(c_re), pack_c(-c_im))


def _log_sigmoid(z):
    return -(jnp.maximum(-z, 0.0) + jnp.log(1.0 + jnp.exp(-jnp.abs(z))))


def _split3(x):
    hi = x.astype(BF16)
    r1 = x - hi.astype(F32)
    mid = r1.astype(BF16)
    lo = (r1 - mid.astype(F32)).astype(BF16)
    return hi, mid, lo


def _gla_kernel(*refs, n_chunks, dk, dv, zero_init, emit_state):
    dir_refs = (refs[0:4], refs[4:8])
    wup_ref, bg_ref = refs[8:10]
    rest = refs[10:]
    if not zero_init:
        s0_ref = rest[0]
        rest = rest[1:]
    if emit_state:
        of_ref, ob_ref, sf_ref, st, prep, dec = rest
    else:
        of_ref, ob_ref, st, prep, dec = rest
    o_refs = (of_ref, ob_ref)
    i = pl.program_id(1)
    scale = dk ** -0.5

    @pl.when(i == 0)
    def _():
        for d in range(N_DIR):
            for h in range(N_HEADS):
                if zero_init:
                    st[d, h] = jnp.zeros((dv, dk), F32)
                else:
                    st[d, h] = s0_ref[d, h].T

    row = lax.broadcasted_iota(jnp.int32, (CHUNK, CHUNK), 0)
    col = lax.broadcasted_iota(jnp.int32, (CHUNK, CHUNK), 1)
    causal = (col <= row, col >= row)
    ones = tuple(jnp.where(c, 1.0, 0.0).astype(BF16) for c in causal)

    dims = (((1,), (1,)), ((), ()))
    chunk_of = lambda d, cc: cc if d == 0 else n_chunks - 1 - cc
    rows_of = lambda d, cc: pl.ds(chunk_of(d, cc) * CHUNK, CHUNK)

    def gate_z(d, cc):
        glr_ref = dir_refs[d][3]
        return _dot(glr_ref[rows_of(d, cc), :].astype(BF16), wup_ref[d]) + bg_ref[d]

    def gate_cum(d, g):
        return sum(_dot(ones[d], part) for part in _split3(g))

    def gate_store(slot, d, cc, g, cum):
        q_ref, k_ref = dir_refs[d][:2]
        rows = rows_of(d, cc)
        total = jnp.sum(g, axis=0, keepdims=True)
        prep[slot, d, 0] = (q_ref[rows, :] * scale * jnp.exp(cum)).astype(BF16)
        kk = k_ref[rows, :]
        prep[slot, d, 1] = (kk * jnp.exp(-cum)).astype(BF16)
        prep[slot, d, 2] = (kk * jnp.exp(total - cum)).astype(BF16)
        dec[slot, d] = jnp.broadcast_to(jnp.exp(total), (SUBLANES, total.shape[1]))

    pairs = [(d, h) for d in range(N_DIR) for h in range(N_HEADS)]
    ksl = lambda h: slice(h * dk, (h + 1) * dk)
    vsl = lambda h: slice(h * dv, (h + 1) * dv)

    for d in range(N_DIR):
        g0 = _log_sigmoid(gate_z(d, 0)) / GATE_TAU
        gate_store(0, d, 0, g0, gate_cum(d, g0))

    for cc in range(n_chunks):
        slot = cc % 2
        more = cc + 1 < n_chunks
        if more:
            z_next = [gate_z(d, cc + 1) for d in range(N_DIR)]
        sc = {}
        for d, h in pairs:
            s = lax.dot_general(prep[slot, d, 0, :, ksl(h)], prep[slot, d, 1, :, ksl(h)], dims,
                                preferred_element_type=F32)
            sc[d, h] = jnp.where(causal[d], s, 0.0).astype(BF16)
        if more:
            g_next = [_log_sigmoid(z) / GATE_TAU for z in z_next]
        for d, h in pairs:
            v_ref = dir_refs[d][2]
            rows = rows_of(d, cc)
            vh = v_ref[rows, vsl(h)].astype(BF16)
            o = _dot(sc[d, h], vh) + lax.dot_general(
                prep[slot, d, 0, :, ksl(h)], st[d, h].astype(BF16), dims,
                preferred_element_type=F32)
            o_refs[d][rows, vsl(h)] = o
        if more:
            cum_next = [gate_cum(d, g_next[d]) for d in range(N_DIR)]
        for d, h in pairs:
            v_ref = dir_refs[d][2]
            vh = v_ref[rows_of(d, cc), vsl(h)].astype(BF16)
            upd = lax.dot_general(vh, prep[slot, d, 2, :, ksl(h)], (((0,), (0,)), ((), ())),
                                  preferred_element_type=F32)
            st[d, h] = dec[slot, d, 0:1, ksl(h)] * st[d, h] + upd
        if more:
            for d in range(N_DIR):
                gate_store(1 - slot, d, cc + 1, g_next[d], cum_next[d])

    if emit_state:
        @pl.when(i == pl.num_programs(1) - 1)
        def _():
            for d in range(N_DIR):
                for h in range(N_HEADS):
                    sf_ref[d, h] = st[d, h].T


def _gla_scan(q, k, v, glr, wup, bg, s0, emit_state, tt):
    b, t, dkk = q.shape
    dvv = v.shape[-1]
    dk, dv = dkk // N_HEADS, dvv // N_HEADS
    n_t = t // tt
    zero_init = s0 is None
    fwd_spec = lambda n: pl.BlockSpec((None, tt, n), lambda bi, i: (bi, i, 0))
    bwd_spec = lambda n: pl.BlockSpec((None, tt, n), lambda bi, i: (bi, n_t - 1 - i, 0))
    state_spec = pl.BlockSpec((None, N_DIR, N_HEADS, dk, dv), lambda bi, i: (bi, 0, 0, 0, 0))
    widths = (dkk, dkk, dvv, glr.shape[-1])
    in_specs = ([fwd_spec(n) for n in widths] + [bwd_spec(n) for n in widths]
                + [_const_spec(wup.shape), _const_spec(bg.shape)])
    args = [q, k, v, glr, q, k, v, glr, wup, bg]
    if not zero_init:
        in_specs.append(state_spec)
        args.append(s0)
    out_specs = [fwd_spec(dvv), bwd_spec(dvv)]
    out_shapes = [jax.ShapeDtypeStruct((b, t, dvv), F32)] * 2
    if emit_state:
        out_specs.append(state_spec)
        out_shapes.append(jax.ShapeDtypeStruct((b, N_DIR, N_HEADS, dk, dv), F32))
    return pl.pallas_call(
        functools.partial(_gla_kernel, n_chunks=tt // CHUNK, dk=dk, dv=dv,
                          zero_init=zero_init, emit_state=emit_state),
        grid=(b, n_t),
        in_specs=in_specs,
        out_specs=out_specs,
        out_shape=out_shapes,
        scratch_shapes=[pltpu.VMEM((N_DIR, N_HEADS, dv, dk), F32),
                        pltpu.VMEM((2, N_DIR, 3, CHUNK, dkk), BF16),
                        pltpu.VMEM((2, N_DIR, SUBLANES, dkk), F32)],
        compiler_params=pltpu.CompilerParams(
            dimension_semantics=("parallel", "arbitrary"), vmem_limit_bytes=VMEM_LIMIT),
        name="gla_scan",
    )(*args)


def _gelu_tanh(x):
    return 0.5 * x * (1.0 + jnp.tanh(math.sqrt(2.0 / math.pi) * (x + 0.044715 * (x * x * x))))


def _mixout_kernel(x_ref, of_ref, ob_ref, r_ref, ga_ref, gb_ref, u_ref, yf_ref, yb_ref, mod_ref,
                   gn_ref, d_ref, wglu_ref, bglu_ref, wpg_ref, wps_ref, wout_ref, o_ref, *, d, dv):
    o = of_ref[...] + ob_ref[...]
    r = r_ref[...]
    gn = gn_ref[...]
    parts = []
    for h in range(N_HEADS):
        vs = slice(h * dv, (h + 1) * dv)
        oh = o[:, vs]
        ms = jnp.mean(jnp.square(oh), axis=-1, keepdims=True)
        parts.append((oh * lax.rsqrt(ms + EPS) * gn * _silu(r[:, vs])).astype(BF16))
    pg = _dot(jnp.concatenate(parts, axis=-1), wpg_ref[...])

    u = u_ref[...]
    y = _gelu_tanh(yf_ref[...] + yb_ref[...] + d_ref[...] * u)
    y = y * jax.nn.sigmoid(_dot(y.astype(BF16), wglu_ref[...]) + bglu_ref[...])
    ps = _dot(y.astype(BF16), wps_ref[...])

    merged = jax.nn.sigmoid(ga_ref[...]) * pg + jax.nn.sigmoid(gb_ref[...]) * ps
    gate = mod_ref[:, 2 * d:3 * d]
    o_ref[...] = x_ref[...] + gate * _dot(merged.astype(BF16), wout_ref[...])


def _mixout(x, o_f, o_b, r, ga, gb, u_tm, y, mod, gn, s5d, wglu, bglu, wpg, wps, wout, tm):
    b, t, d = x.shape
    dvv = r.shape[-1]
    w = u_tm.shape[1] // b
    per_batch_mod = mod.shape[0] != 1
    row_spec = lambda n: pl.BlockSpec((None, tm, n), lambda bi, i: (bi, i, 0))
    mod_idx = (lambda bi, i: (bi, 0, 0)) if per_batch_mod else (lambda bi, i: (0, 0, 0))
    in_specs = [
        row_spec(d), row_spec(dvv), row_spec(dvv), row_spec(dvv), row_spec(d), row_spec(d),
        pl.BlockSpec((tm, w), lambda bi, i: (i, bi)),
        pl.BlockSpec((None, tm, w), lambda bi, i: (0, i, bi)),
        pl.BlockSpec((None, tm, w), lambda bi, i: (1, i, bi)),
        pl.BlockSpec((None, 1, mod.shape[2]), mod_idx),
        _const_spec((1, dvv // N_HEADS)), _const_spec((1, w)),
        _const_spec(wglu.shape), _const_spec((1, w)),
        _const_spec(wpg.shape), _const_spec(wps.shape), _const_spec(wout.shape),
    ]
    return pl.pallas_call(
        functools.partial(_mixout_kernel, d=d, dv=dvv // N_HEADS),
        grid=(b, t // tm),
        in_specs=in_specs,
        out_specs=row_spec(d),
        out_shape=jax.ShapeDtypeStruct((b, t, d), F32),
        compiler_params=pltpu.CompilerParams(
            dimension_semantics=("parallel", "parallel"), vmem_limit_bytes=VMEM_LIMIT),
        name="mixout",
    )(x, o_f, o_b, r, ga, gb, u_tm, y, y, mod, gn.reshape(1, -1), s5d.reshape(1, -1),
      wglu, bglu.reshape(1, -1), wpg, wps, wout)


def _mlp_kernel(*refs, d, ff_block, final_norm):
    if final_norm:
        x_ref, mod_ref, g_ref, w1_ref, w2_ref, fg_ref, o_ref = refs
    else:
        x_ref, mod_ref, g_ref, w1_ref, w2_ref, o_ref = refs
    x = x_ref[...]
    shift = mod_ref[:, 3 * d:4 * d]
    scale = mod_ref[:, 4 * d:5 * d]
    gate = mod_ref[:, 5 * d:6 * d]
    h = _rms_mod(x, g_ref[...], scale, shift).astype(BF16)
    acc = jnp.zeros(x.shape, F32)
    for j in range(w1_ref.shape[1] // ff_block):
        cs = slice(j * ff_block, (j + 1) * ff_block)
        a = jnp.square(jnp.maximum(_dot(h, w1_ref[:, cs]), 0.0)).astype(BF16)
        acc = acc + _dot(a, w2_ref[cs, :])
    x = x + gate * acc
    if final_norm:
        ms = jnp.mean(jnp.square(x), axis=-1, keepdims=True)
        x = x * lax.rsqrt(ms + EPS) * fg_ref[...]
    o_ref[...] = x


def _mlp(x, mod, g, w1, w2, final_g, tm):
    b, t, d = x.shape
    per_batch_mod = mod.shape[0] != 1
    final_norm = final_g is not None
    row_spec = pl.BlockSpec((None, tm, d), lambda bi, i: (bi, i, 0))
    mod_idx = (lambda bi, i: (bi, 0, 0)) if per_batch_mod else (lambda bi, i: (0, 0, 0))
    in_specs = [row_spec, pl.BlockSpec((None, 1, mod.shape[2]), mod_idx), _const_spec((1, d)),
                _const_spec(w1.shape), _const_spec(w2.shape)]
    args = [x, mod, g.reshape(1, d), w1, w2]
    if final_norm:
        in_specs.append(_const_spec((1, d)))
        args.append(final_g.reshape(1, d))
    return pl.pallas_call(
        functools.partial(_mlp_kernel, d=d, ff_block=1024, final_norm=final_norm),
        grid=(b, t // tm),
        in_specs=in_specs,
        out_specs=row_spec,
        out_shape=jax.ShapeDtypeStruct((b, t, d), F32),
        compiler_params=pltpu.CompilerParams(
            dimension_semantics=("parallel", "parallel"), vmem_limit_bytes=VMEM_LIMIT),
        name="mlp",
    )(*args)


def _grid_pos_embed(n_tokens, dim):
    rows = n_tokens // GRID_W
    r = jnp.repeat(jnp.arange(rows, dtype=F32), GRID_W)
    col = jnp.tile(jnp.arange(GRID_W, dtype=F32), rows)
    quarter = dim // 4
    omega = 1.0 / (POS_BASE ** (jnp.arange(quarter, dtype=F32) / quarter))
    ar = r[:, None] * omega
    ac = col[:, None] * omega
    return jnp.concatenate([jnp.sin(ar), jnp.cos(ar), jnp.sin(ac), jnp.cos(ac)], axis=-1)


def _split_w_in(w_in, dk_all, dv_all, s5w, d):
    splits = (dk_all, dk_all, dv_all, dv_all, N_DIR * GATE_RANK, s5w, d, d)
    idx = np.cumsum((0,) + splits)
    parts = [w_in[:, idx[j]:idx[j + 1]] for j in range(len(splits))]
    parts[4] = jnp.pad(parts[4], ((0, 0), (0, LANES - splits[4])))
    return [p.astype(BF16) for p in parts]


def _stream(x, pos, mods, gla_s0, s5_s0, layers, final_g, emit_state, tm):
    b, t, d = x.shape
    gla_states, s5_re, s5_im = [], [], []
    for l, p in enumerate(layers):
        res = _inproj(x, pos if l == 0 else None, mods[l], p["norm1_g"], p["w_in"], tm)
        if l == 0 and pos is not None:
            x = res[0]
            res = res[1:]
        q, k, v, r, glr, u_tm, ga, gb = res
        s5_out = _s5_scan(u_tm, p["s5"], None if s5_s0 is None else s5_s0[l], b, emit_state,
                          tt=min(t, 64))
        gla_out = _gla_scan(q, k, v, glr, p["wup"], p["bg"],
                            None if gla_s0 is None else gla_s0[l], emit_state, tt=min(t, 256))
        o_f, o_b = gla_out[:2]
        if emit_state:
            y, f_re, f_im = s5_out
            gla_states.append(gla_out[2])
            s5_re.append(f_re)
            s5_im.append(f_im)
        else:
            y = s5_out
        x = _mixout(x, o_f, o_b, r, ga, gb, u_tm, y, mods[l], p["gla_norm_g"], p["s5_d"], p["w_glu"],
                    p["b_glu"], p["w_proj_gla"], p["w_proj_s5"], p["w_out"], tm)
        last = l == len(layers) - 1
        x = _mlp(x, mods[l], p["norm2_g"], p["w_ff1"], p["w_ff2"], final_g if last else None, tm)
    return x, gla_states, s5_re, s5_im


def kernel(x_prompt, x_sample, c, cache_gla_state, state_s5_re, state_s5_im, c_ctx, w_mod, b_mod,
           norm1_g, w_in, w_gate_up, b_gate, gla_norm_g, w_proj_gla, s5_lam_re, s5_lam_im,
           s5_log_step, s5_b_re, s5_b_im, s5_c_re, s5_c_im, s5_d, w_glu, b_glu, w_proj_s5, w_out,
           norm2_g, w_ff1, w_ff2, final_g):
    depth = w_in.shape[0]
    nb, seq, d = x_prompt.shape
    db, dseq, _ = x_sample.shape
    dk_all = w_gate_up.shape[-1]
    dv_all = w_proj_gla.shape[1]
    s5w = s5_d.shape[-1]
    n_groups, n_state = s5_lam_re.shape[2], s5_lam_re.shape[3]
    dk, dv = dk_all // N_HEADS, dv_all // N_HEADS
    assert nb % SUBLANES == 0 and db % SUBLANES == 0

    n_cond = -(-(db + 1) // SUBLANES) * SUBLANES
    conds = jnp.concatenate([c, c_ctx[None], jnp.zeros((n_cond - db - 1, d), F32)], axis=0)
    mod_all = _modulation(conds, w_mod, b_mod)
    mods_lat = [mod_all[l, :db].reshape(db, 1, N_MOD * d) for l in range(depth)]
    mods_ctx = [mod_all[l, db:db + 1].reshape(1, 1, N_MOD * d) for l in range(depth)]

    layers = []
    for l in range(depth):
        wup = jnp.zeros((N_DIR, LANES, dk_all), F32)
        for dd in range(N_DIR):
            wup = wup.at[dd, dd * GATE_RANK:(dd + 1) * GATE_RANK].set(w_gate_up[l, dd])
        layers.append(dict(
            norm1_g=norm1_g[l], w_in=_split_w_in(w_in[l], dk_all, dv_all, s5w, d),
            wup=wup.astype(BF16), bg=b_gate[l].reshape(N_DIR, 1, dk_all),
            gla_norm_g=gla_norm_g[l], w_proj_gla=w_proj_gla[l].astype(BF16),
            s5=_s5_params(s5_lam_re[l], s5_lam_im[l], s5_log_step[l], s5_b_re[l], s5_b_im[l],
                          s5_c_re[l], s5_c_im[l]),
            s5_d=s5_d[l], w_glu=w_glu[l].astype(BF16), b_glu=b_glu[l],
            w_proj_s5=w_proj_s5[l].astype(BF16), w_out=w_out[l].astype(BF16),
            norm2_g=norm2_g[l], w_ff1=w_ff1[l].astype(BF16), w_ff2=w_ff2[l].astype(BF16)))

    y_prompt, gla_states, s5_re, s5_im = _stream(
        x_prompt, None, mods_ctx, None, None, layers, final_g, True, tm=min(seq, 256))
    new_gla_state = jnp.stack(gla_states, axis=1)
    to_state = lambda a: a.transpose(1, 0, 2).reshape(nb, N_DIR, n_groups, n_state)
    new_s5_re = jnp.stack([to_state(a) for a in s5_re], axis=1)
    new_s5_im = jnp.stack([to_state(a) for a in s5_im], axis=1)

    pos = _grid_pos_embed(dseq, d)
    gla_s0 = [cache_gla_state[:, l] for l in range(depth)]
    from_state = lambda a: a.reshape(db, N_DIR, n_groups * n_state).transpose(1, 0, 2)
    s5_s0 = [(from_state(state_s5_re[:, l]), from_state(state_s5_im[:, l])) for l in range(depth)]
    y_sample, _, _, _ = _stream(
        x_sample, pos, mods_lat, gla_s0, s5_s0, layers, final_g, False, tm=min(dseq, 256))

    return (y_prompt, y_sample, new_gla_state, new_s5_re, new_s5_im)
```

```python
import functools
import math

import jax
import jax.numpy as jnp
import numpy as np
from jax import lax
from jax.experimental import pallas as pl
from jax.experimental.pallas import tpu as pltpu

N_DIR = 2
N_HEADS = 4
GATE_RANK = 16
GATE_TAU = 16.0
CHUNK = 64
S5_GROUP = 16
S5_STATE = 64
N_MOD = 6
EPS = 1e-6
GRID_W = 64
POS_BASE = 10000.0

LANES = 128
SUBLANES = 8
VMEM_LIMIT = 56 * 1024 * 1024

F32 = jnp.float32
BF16 = jnp.bfloat16


def _const_spec(shape):
    nd = len(shape)
    return pl.BlockSpec(shape, lambda *_: (0,) * nd, pipeline_mode=pl.Buffered(1))


def _dot(a, b):
    return jnp.dot(a, b, preferred_element_type=F32)


def _rms_mod(x, g, scale, shift):
    ms = jnp.mean(jnp.square(x), axis=-1, keepdims=True)
    return x * lax.rsqrt(ms + EPS) * g * (1.0 + scale) + shift


def _silu(x):
    return x * jax.nn.sigmoid(x)


def _mod_kernel(c_ref, w_ref, b_ref, o_ref):
    a = _silu(c_ref[...]).astype(BF16)
    o_ref[...] = _dot(a, w_ref[...].astype(BF16)) + b_ref[...]


def _modulation(conds, w_mod, b_mod):
    depth, d, dm = w_mod.shape
    rows = conds.shape[0]
    tn = 1024
    return pl.pallas_call(
        _mod_kernel,
        grid=(depth, dm // tn),
        in_specs=[
            pl.BlockSpec((rows, d), lambda l, j: (0, 0)),
            pl.BlockSpec((None, d, tn), lambda l, j: (l, 0, j)),
            pl.BlockSpec((None, 1, tn), lambda l, j: (l, 0, j)),
        ],
        out_specs=pl.BlockSpec((None, rows, tn), lambda l, j: (l, 0, j)),
        out_shape=jax.ShapeDtypeStruct((depth, rows, dm), F32),
        compiler_params=pltpu.CompilerParams(vmem_limit_bytes=VMEM_LIMIT),
        name="modulation",
    )(conds, w_mod, b_mod.reshape(depth, 1, dm))


def _inproj_kernel(*refs, d, add_pos):
    if add_pos:
        x_ref, pos_ref, mod_ref, g_ref = refs[:4]
        rest = refs[4:]
    else:
        x_ref, mod_ref, g_ref = refs[:3]
        rest = refs[3:]
    w_refs = rest[:8]
    out_refs = rest[8:]
    x = x_ref[...]
    if add_pos:
        x = x + pos_ref[...]
        out_refs[0][...] = x
        out_refs = out_refs[1:]
    shift = mod_ref[:, 0:d]
    scale = mod_ref[:, d:2 * d]
    h = _rms_mod(x, g_ref[...], scale, shift).astype(BF16)
    for w_ref, o_ref in zip(w_refs, out_refs):
        o_ref[...] = _dot(h, w_ref[...])


def _inproj(x, pos, mod, g, weights, tm):
    b, t, d = x.shape
    add_pos = pos is not None
    per_batch_mod = mod.shape[0] != 1
    widths = [w.shape[1] for w in weights]
    row_spec = lambda n: pl.BlockSpec((None, tm, n), lambda bi, i: (bi, i, 0))
    in_specs = [row_spec(d)]
    args = [x]
    if add_pos:
        in_specs.append(pl.BlockSpec((tm, d), lambda bi, i: (i, 0)))
        args.append(pos)
    mod_idx = (lambda bi, i: (bi, 0, 0)) if per_batch_mod else (lambda bi, i: (0, 0, 0))
    in_specs += [pl.BlockSpec((None, 1, mod.shape[2]), mod_idx), _const_spec((1, d))]
    args += [mod, g.reshape(1, d)]
    in_specs += [_const_spec(w.shape) for w in weights]
    args += list(weights)

    out_specs, out_shapes = [], []
    if add_pos:
        out_specs.append(row_spec(d))
        out_shapes.append(jax.ShapeDtypeStruct((b, t, d), F32))
    for idx, n in enumerate(widths):
        if idx == 5:
            out_specs.append(pl.BlockSpec((tm, n), lambda bi, i: (i, bi)))
            out_shapes.append(jax.ShapeDtypeStruct((t, b * n), F32))
        else:
            out_specs.append(row_spec(n))
            out_shapes.append(jax.ShapeDtypeStruct((b, t, n), F32))
    return pl.pallas_call(
        functools.partial(_inproj_kernel, d=d, add_pos=add_pos),
        grid=(b, t // tm),
        in_specs=in_specs,
        out_specs=out_specs,
        out_shape=out_shapes,
        compiler_params=pltpu.CompilerParams(
            dimension_semantics=("parallel", "parallel"), vmem_limit_bytes=VMEM_LIMIT),
        name="inproj",
    )(*args)


S5_BLOCK = 4


def _s5_kernel(*refs, tt, w, n_p, zero_init, emit_state):
    u_ref, l4r_ref, l4i_ref, wzr_ref, wzi_ref, wor_ref, woi_ref, wt_ref = refs[:8]
    rest = refs[8:]
    if not zero_init:
        s0r_ref, s0i_ref = rest[:2]
        rest = rest[2:]
    if emit_state:
        y_ref, fr_ref, fi_ref = rest[:3]
        rest = rest[3:]
    else:
        y_ref = rest[0]
        rest = rest[1:]
    usc, xsc, zre, zim, ypk, car, cai = rest
    direction = pl.program_id(1)
    i = pl.program_id(2)
    n_slab = w // LANES
    n_blk = tt // S5_BLOCK
    half = LANES // 2
    tw = 2 * LANES
    grp = S5_BLOCK * SUBLANES

    @pl.when(i == 0)
    def _():
        if zero_init:
            car[...] = jnp.zeros_like(car)
            cai[...] = jnp.zeros_like(cai)
        else:
            car[...] = s0r_ref[...]
            cai[...] = s0i_ref[...]

    for b in range(SUBLANES):
        for s in range(n_slab):
            usc[s, pl.ds(b, tt, stride=SUBLANES), :] = u_ref[:, b * w + s * LANES:b * w + (s + 1) * LANES]

    low = lax.broadcasted_iota(jnp.int32, (2 * SUBLANES, LANES), 1) < half
    swap = lambda a: pltpu.roll(a, half, 1)

    def pack(s):
        for r2 in range(n_blk // 2):
            a = [jnp.concatenate([usc[s, pl.ds(base + j * SUBLANES, SUBLANES), :]
                                  for base in (2 * r2 * grp, (2 * r2 + 1) * grp)], axis=0)
                 for j in range(S5_BLOCK)]
            asw = [swap(v) for v in a]
            for h in range(2):
                pick = lambda j, slot_half: a[j] if h == slot_half else asw[j]
                cols = [jnp.where(low, pick(2 * c, 0), pick(2 * c + 1, 1)) for c in range(2)]
                xsc[2 * s + h, pl.ds(r2 * 2 * SUBLANES, 2 * SUBLANES), :] = (
                    jnp.concatenate(cols, axis=1).astype(BF16))

    def block_inputs(p):
        x = xsc[p]
        zre[:, p * tw:(p + 1) * tw] = _dot(x, wzr_ref[p])
        zim[:, p * tw:(p + 1) * tw] = _dot(x, wzi_ref[p])

    def scan(s):
        ls = slice(s * 2 * tw, (s + 1) * 2 * tw)
        lr = l4r_ref[:, ls]
        li = l4i_ref[:, ls]
        cr, ci = car[:, ls], cai[:, ls]
        for j in range(n_blk):
            r = j + direction * (n_blk - 1 - 2 * j)
            r0 = pl.multiple_of(r * SUBLANES, SUBLANES)
            zr = zre[pl.ds(r0, SUBLANES), ls]
            zi = zim[pl.ds(r0, SUBLANES), ls]
            zre[pl.ds(r0, SUBLANES), ls] = cr
            zim[pl.ds(r0, SUBLANES), ls] = ci
            cr, ci = lr * cr - li * ci + zr, lr * ci + li * cr + zi
        car[:, ls] = cr
        cai[:, ls] = ci

    def block_outputs(p):
        ypk[p] = (_dot(zre[:, p * tw:(p + 1) * tw].astype(BF16), wor_ref[p])
                  + _dot(zim[:, p * tw:(p + 1) * tw].astype(BF16), woi_ref[p])
                  + _dot(xsc[p], wt_ref[p]))

    def unpack(s):
        for r2 in range(n_blk // 2):
            rows = pl.ds(r2 * 2 * SUBLANES, 2 * SUBLANES)
            for j in range(S5_BLOCK):
                c, slot_half = divmod(j, 2)
                left = ypk[2 * s, rows, c * LANES:(c + 1) * LANES]
                right = ypk[2 * s + 1, rows, c * LANES:(c + 1) * LANES]
                v = jnp.where(low, left if slot_half == 0 else swap(left),
                              right if slot_half == 1 else swap(right))
                usc[s, pl.ds(2 * r2 * grp + j * SUBLANES, SUBLANES), :] = v[:SUBLANES]
                usc[s, pl.ds((2 * r2 + 1) * grp + j * SUBLANES, SUBLANES), :] = v[SUBLANES:]

    for s in range(n_slab + 1):
        if s < n_slab:
            pack(s)
            block_inputs(2 * s)
            block_inputs(2 * s + 1)
        if s >= 1:
            block_outputs(2 * s - 2)
            block_outputs(2 * s - 1)
            unpack(s - 1)
        if s < n_slab:
            scan(s)

    for b in range(SUBLANES):
        for s in range(n_slab):
            y_ref[:, b * w + s * LANES:b * w + (s + 1) * LANES] = usc[s, pl.ds(b, tt, stride=SUBLANES), :]

    if emit_state:
        @pl.when(i == pl.num_programs(2) - 1)
        def _():
            fr_ref[...] = car[...]
            fi_ref[...] = cai[...]


def _s5_scan(u_tm, prm, s0, b, emit_state, tt):
    t = u_tm.shape[0]
    w = u_tm.shape[1] // b
    l4r, l4i = prm[:2]
    mats = prm[2:]
    n_p = mats[0].shape[1]
    ns = l4r.shape[-1]
    n_t = t // tt
    n_g = b // SUBLANES
    rows = tt // S5_BLOCK * SUBLANES
    zero_init = s0 is None
    tile = lambda g, d, i: i + d * (n_t - 1 - 2 * i)
    dir_spec = lambda shape: pl.BlockSpec((None,) + shape, lambda g, d, i: (d,) + (0,) * len(shape))
    in_specs = [pl.BlockSpec((tt, SUBLANES * w), lambda g, d, i: (tile(g, d, i), g)),
                dir_spec((SUBLANES, ns)), dir_spec((SUBLANES, ns))]
    in_specs += [dir_spec(m.shape[1:]) for m in mats]
    args = [u_tm, l4r, l4i] + list(mats)
    state_spec = pl.BlockSpec((None, SUBLANES, ns), lambda g, d, i: (d, g, 0))
    if not zero_init:
        in_specs += [state_spec, state_spec]
        args += list(s0)
    out_specs = [pl.BlockSpec((None, tt, SUBLANES * w), lambda g, d, i: (d, tile(g, d, i), g))]
    out_shapes = [jax.ShapeDtypeStruct((N_DIR, t, b * w), F32)]
    if emit_state:
        out_specs += [state_spec, state_spec]
        out_shapes += [jax.ShapeDtypeStruct((N_DIR, b, ns), F32)] * 2
    res = pl.pallas_call(
        functools.partial(_s5_kernel, tt=tt, w=w, n_p=n_p, zero_init=zero_init,
                          emit_state=emit_state),
        grid=(n_g, N_DIR, n_t),
        in_specs=in_specs,
        out_specs=out_specs,
        out_shape=out_shapes,
        scratch_shapes=[pltpu.VMEM((w // LANES, tt * SUBLANES, LANES), F32),
                        pltpu.VMEM((n_p, rows, 2 * LANES), BF16),
                        pltpu.VMEM((rows, ns), F32), pltpu.VMEM((rows, ns), F32),
                        pltpu.VMEM((n_p, rows, 2 * LANES), F32),
                        pltpu.VMEM((SUBLANES, ns), F32), pltpu.VMEM((SUBLANES, ns), F32)],
        compiler_params=pltpu.CompilerParams(
            dimension_semantics=("parallel", "parallel", "arbitrary"), vmem_limit_bytes=VMEM_LIMIT),
        name="s5_scan",
    )(*args)
    if emit_state:
        return res[0], res[1], res[2]
    return res[0]


def _s5_params(lam_re, lam_im, log_step, b_re, b_im, c_re, c_im):
    n_dir, g, p = lam_re.shape
    n = b_re.shape[-1]
    blk = S5_BLOCK
    gpt = 2 * LANES // p
    n_p = g // gpt
    assert gpt * n * blk == 2 * LANES and n_dir == N_DIR
    hi = lax.Precision.HIGHEST
    step = jnp.exp(log_step)[..., None]
    a = lam_re * step
    th = lam_im * step
    pw_re = [jnp.exp(k * a) * jnp.cos(k * th) for k in range(blk + 1)]
    pw_im = [jnp.exp(k * a) * jnp.sin(k * th) for k in range(blk + 1)]
    den = lam_re * lam_re + lam_im * lam_im
    fr = ((pw_re[1] - 1.0) * lam_re + pw_im[1] * lam_im) / den
    fi = (pw_im[1] * lam_re - (pw_re[1] - 1.0) * lam_im) / den
    bbr = fr[..., None] * b_re - fi[..., None] * b_im
    bbi = fr[..., None] * b_im + fi[..., None] * b_re
    eye = jnp.eye(gpt, dtype=F32)

    def tiles(m):
        r, c = m.shape[1:]
        m = m.reshape(n_p, gpt, r, c)
        m = m[:, :, :, None, :] * eye[None, :, None, :, None]
        return m.reshape(n_p, gpt * r, gpt * c)

    def lam_b(d, m):
        pr, pi = pw_re[m][d][..., None], pw_im[m][d][..., None]
        return pr * bbr[d] - pi * bbi[d], pr * bbi[d] + pi * bbr[d]

    out = {k: [] for k in ("wzr", "wzi", "wor", "woi", "wt")}
    for d in range(n_dir):
        pos = (lambda j: j) if d == 0 else (lambda j: blk - 1 - j)
        wz = [lam_b(d, blk - 1 - pos(j)) for j in range(blk)]
        out["wzr"].append(jnp.concatenate([tiles(m[0].transpose(0, 2, 1)) for m in wz], axis=1))
        out["wzi"].append(jnp.concatenate([tiles(m[1].transpose(0, 2, 1)) for m in wz], axis=1))
        wor, woi = [], []
        for j in range(blk):
            pr, pi = pw_re[pos(j) + 1][d][:, None, :], pw_im[pos(j) + 1][d][:, None, :]
            wor.append(tiles((c_re[d] * pr - c_im[d] * pi).transpose(0, 2, 1)))
            woi.append(tiles((-(c_re[d] * pi + c_im[d] * pr)).transpose(0, 2, 1)))
        out["wor"].append(jnp.concatenate(wor, axis=2))
        out["woi"].append(jnp.concatenate(woi, axis=2))
        kern = []
        for m in range(blk):
            lr, li = lam_b(d, m)
            kern.append(jnp.einsum("gnp,gpm->gmn", c_re[d], lr, precision=hi)
                        - jnp.einsum("gnp,gpm->gmn", c_im[d], li, precision=hi))
        zero = jnp.zeros_like(kern[0])
        wt = [[tiles(kern[pos(j) - pos(i)] if pos(i) <= pos(j) else zero) for j in range(blk)]
              for i in range(blk)]
        out["wt"].append(jnp.concatenate([jnp.concatenate(row, axis=2) for row in wt], axis=1))
    bcast = lambda v: jnp.broadcast_to(v.reshape(n_dir, 1, g * p), (n_dir, SUBLANES, g * p))
    mats = [jnp.stack(out[k]).astype(BF16) for k in ("wzr", "wzi", "wor", "woi", "wt")]
    return [bcast(pw_re[blk]), bcast(pw_im[blk])] + mats


def _log_sigmoid(z):
    return -(jnp.maximum(-z, 0.0) + jnp.log(1.0 + jnp.exp(-jnp.abs(z))))


def _split3(x):
    hi = x.astype(BF16)
    r1 = x - hi.astype(F32)
    mid = r1.astype(BF16)
    lo = (r1 - mid.astype(F32)).astype(BF16)
    return hi, mid, lo


def _gla_kernel(*refs, n_chunks, dk, dv, zero_init, emit_state):
    dir_refs = (refs[0:4], refs[4:8])
    wup_ref, bg_ref = refs[8:10]
    rest = refs[10:]
    if not zero_init:
        s0_ref = rest[0]
        rest = rest[1:]
    if emit_state:
        of_ref, ob_ref, sf_ref, st, prep, dec = rest
    else:
        of_ref, ob_ref, st, prep, dec = rest
    o_refs = (of_ref, ob_ref)
    i = pl.program_id(1)
    scale = dk ** -0.5

    @pl.when(i == 0)
    def _():
        for d in range(N_DIR):
            for h in range(N_HEADS):
                if zero_init:
                    st[d, h] = jnp.zeros((dv, dk), F32)
                else:
                    st[d, h] = s0_ref[d, h].T

    row = lax.broadcasted_iota(jnp.int32, (CHUNK, CHUNK), 0)
    col = lax.broadcasted_iota(jnp.int32, (CHUNK, CHUNK), 1)
    causal = (col <= row, col >= row)
    ones = tuple(jnp.where(c, 1.0, 0.0).astype(BF16) for c in causal)

    dims = (((1,), (1,)), ((), ()))
    chunk_of = lambda d, cc: cc if d == 0 else n_chunks - 1 - cc
    rows_of = lambda d, cc: pl.ds(chunk_of(d, cc) * CHUNK, CHUNK)

    def gate_z(d, cc):
        glr_ref = dir_refs[d][3]
        return _dot(glr_ref[rows_of(d, cc), :].astype(BF16), wup_ref[d]) + bg_ref[d]

    def gate_cum(d, g):
        return sum(_dot(ones[d], part) for part in _split3(g))

    def gate_store(slot, d, cc, g, cum):
        q_ref, k_ref = dir_refs[d][:2]
        rows = rows_of(d, cc)
        total = jnp.sum(g, axis=0, keepdims=True)
        prep[slot, d, 0] = (q_ref[rows, :] * scale * jnp.exp(cum)).astype(BF16)
        kk = k_ref[rows, :]
        prep[slot, d, 1] = (kk * jnp.exp(-cum)).astype(BF16)
        prep[slot, d, 2] = (kk * jnp.exp(total - cum)).astype(BF16)
        dec[slot, d] = jnp.broadcast_to(jnp.exp(total), (SUBLANES, total.shape[1]))

    pairs = [(d, h) for d in range(N_DIR) for h in range(N_HEADS)]
    ksl = lambda h: slice(h * dk, (h + 1) * dk)
    vsl = lambda h: slice(h * dv, (h + 1) * dv)

    for d in range(N_DIR):
        g0 = _log_sigmoid(gate_z(d, 0)) / GATE_TAU
        gate_store(0, d, 0, g0, gate_cum(d, g0))

    for cc in range(n_chunks):
        slot = cc % 2
        more = cc + 1 < n_chunks
        if more:
            z_next = [gate_z(d, cc + 1) for d in range(N_DIR)]
        sc = {}
        for d, h in pairs:
            s = lax.dot_general(prep[slot, d, 0, :, ksl(h)], prep[slot, d, 1, :, ksl(h)], dims,
                                preferred_element_type=F32)
            sc[d, h] = jnp.where(causal[d], s, 0.0).astype(BF16)
        if more:
            g_next = [_log_sigmoid(z) / GATE_TAU for z in z_next]
        for d, h in pairs:
            v_ref = dir_refs[d][2]
            rows = rows_of(d, cc)
            vh = v_ref[rows, vsl(h)].astype(BF16)
            o = _dot(sc[d, h], vh) + lax.dot_general(
                prep[slot, d, 0, :, ksl(h)], st[d, h].astype(BF16), dims,
                preferred_element_type=F32)
            o_refs[d][rows, vsl(h)] = o
        if more:
            cum_next = [gate_cum(d, g_next[d]) for d in range(N_DIR)]
        for d, h in pairs:
            v_ref = dir_refs[d][2]
            vh = v_ref[rows_of(d, cc), vsl(h)].astype(BF16)
            upd = lax.dot_general(vh, prep[slot, d, 2, :, ksl(h)], (((0,), (0,)), ((), ())),
                                  preferred_element_type=F32)
            st[d, h] = dec[slot, d, 0:1, ksl(h)] * st[d, h] + upd
        if more:
            for d in range(N_DIR):
                gate_store(1 - slot, d, cc + 1, g_next[d], cum_next[d])

    if emit_state:
        @pl.when(i == pl.num_programs(1) - 1)
        def _():
            for d in range(N_DIR):
                for h in range(N_HEADS):
                    sf_ref[d, h] = st[d, h].T


def _gla_scan(q, k, v, glr, wup, bg, s0, emit_state, tt):
    b, t, dkk = q.shape
    dvv = v.shape[-1]
    dk, dv = dkk // N_HEADS, dvv // N_HEADS
    n_t = t // tt
    zero_init = s0 is None
    fwd_spec = lambda n: pl.BlockSpec((None, tt, n), lambda bi, i: (bi, i, 0))
    bwd_spec = lambda n: pl.BlockSpec((None, tt, n), lambda bi, i: (bi, n_t - 1 - i, 0))
    state_spec = pl.BlockSpec((None, N_DIR, N_HEADS, dk, dv), lambda bi, i: (bi, 0, 0, 0, 0))
    widths = (dkk, dkk, dvv, glr.shape[-1])
    in_specs = ([fwd_spec(n) for n in widths] + [bwd_spec(n) for n in widths]
                + [_const_spec(wup.shape), _const_spec(bg.shape)])
    args = [q, k, v, glr, q, k, v, glr, wup, bg]
    if not zero_init:
        in_specs.append(state_spec)
        args.append(s0)
    out_specs = [fwd_spec(dvv), bwd_spec(dvv)]
    out_shapes = [jax.ShapeDtypeStruct((b, t, dvv), F32)] * 2
    if emit_state:
        out_specs.append(state_spec)
        out_shapes.append(jax.ShapeDtypeStruct((b, N_DIR, N_HEADS, dk, dv), F32))
    return pl.pallas_call(
        functools.partial(_gla_kernel, n_chunks=tt // CHUNK, dk=dk, dv=dv,
                          zero_init=zero_init, emit_state=emit_state),
        grid=(b, n_t),
        in_specs=in_specs,
        out_specs=out_specs,
        out_shape=out_shapes,
        scratch_shapes=[pltpu.VMEM((N_DIR, N_HEADS, dv, dk), F32),
                        pltpu.VMEM((2, N_DIR, 3, CHUNK, dkk), BF16),
                        pltpu.VMEM((2, N_DIR, SUBLANES, dkk), F32)],
        compiler_params=pltpu.CompilerParams(
            dimension_semantics=("parallel", "arbitrary"), vmem_limit_bytes=VMEM_LIMIT),
        name="gla_scan",
    )(*args)


def _gelu_tanh(x):
    return 0.5 * x * (1.0 + jnp.tanh(math.sqrt(2.0 / math.pi) * (x + 0.044715 * (x * x * x))))


def _mixout_kernel(x_ref, of_ref, ob_ref, r_ref, ga_ref, gb_ref, u_ref, yf_ref, yb_ref, mod_ref,
                   gn_ref, d_ref, wglu_ref, bglu_ref, wpg_ref, wps_ref, wout_ref, o_ref, *, d, dv):
    o = of_ref[...] + ob_ref[...]
    r = r_ref[...]
    gn = gn_ref[...]
    parts = []
    for h in range(N_HEADS):
        vs = slice(h * dv, (h + 1) * dv)
        oh = o[:, vs]
        ms = jnp.mean(jnp.square(oh), axis=-1, keepdims=True)
        parts.append((oh * lax.rsqrt(ms + EPS) * gn * _silu(r[:, vs])).astype(BF16))
    pg = _dot(jnp.concatenate(parts, axis=-1), wpg_ref[...])

    u = u_ref[...]
    y = _gelu_tanh(yf_ref[...] + yb_ref[...] + d_ref[...] * u)
    y = y * jax.nn.sigmoid(_dot(y.astype(BF16), wglu_ref[...]) + bglu_ref[...])
    ps = _dot(y.astype(BF16), wps_ref[...])

    merged = jax.nn.sigmoid(ga_ref[...]) * pg + jax.nn.sigmoid(gb_ref[...]) * ps
    gate = mod_ref[:, 2 * d:3 * d]
    o_ref[...] = x_ref[...] + gate * _dot(merged.astype(BF16), wout_ref[...])


def _mixout(x, o_f, o_b, r, ga, gb, u_tm, y, mod, gn, s5d, wglu, bglu, wpg, wps, wout, tm):
    b, t, d = x.shape
    dvv = r.shape[-1]
    w = u_tm.shape[1] // b
    per_batch_mod = mod.shape[0] != 1
    row_spec = lambda n: pl.BlockSpec((None, tm, n), lambda bi, i: (bi, i, 0))
    mod_idx = (lambda bi, i: (bi, 0, 0)) if per_batch_mod else (lambda bi, i: (0, 0, 0))
    in_specs = [
        row_spec(d), row_spec(dvv), row_spec(dvv), row_spec(dvv), row_spec(d), row_spec(d),
        pl.BlockSpec((tm, w), lambda bi, i: (i, bi)),
        pl.BlockSpec((None, tm, w), lambda bi, i: (0, i, bi)),
        pl.BlockSpec((None, tm, w), lambda bi, i: (1, i, bi)),
        pl.BlockSpec((None, 1, mod.shape[2]), mod_idx),
        _const_spec((1, dvv // N_HEADS)), _const_spec((1, w)),
        _const_spec(wglu.shape), _const_spec((1, w)),
        _const_spec(wpg.shape), _const_spec(wps.shape), _const_spec(wout.shape),
    ]
    return pl.pallas_call(
        functools.partial(_mixout_kernel, d=d, dv=dvv // N_HEADS),
        grid=(b, t // tm),
        in_specs=in_specs,
        out_specs=row_spec(d),
        out_shape=jax.ShapeDtypeStruct((b, t, d), F32),
        compiler_params=pltpu.CompilerParams(
            dimension_semantics=("parallel", "parallel"), vmem_limit_bytes=VMEM_LIMIT),
        name="mixout",
    )(x, o_f, o_b, r, ga, gb, u_tm, y, y, mod, gn.reshape(1, -1), s5d.reshape(1, -1),
      wglu, bglu.reshape(1, -1), wpg, wps, wout)


def _mlp_kernel(*refs, d, ff_block, final_norm):
    if final_norm:
        x_ref, mod_ref, g_ref, w1_ref, w2_ref, fg_ref, o_ref = refs
    else:
        x_ref, mod_ref, g_ref, w1_ref, w2_ref, o_ref = refs
    x = x_ref[...]
    shift = mod_ref[:, 3 * d:4 * d]
    scale = mod_ref[:, 4 * d:5 * d]
    gate = mod_ref[:, 5 * d:6 * d]
    h = _rms_mod(x, g_ref[...], scale, shift).astype(BF16)
    acc = jnp.zeros(x.shape, F32)
    for j in range(w1_ref.shape[1] // ff_block):
        cs = slice(j * ff_block, (j + 1) * ff_block)
        a = jnp.square(jnp.maximum(_dot(h, w1_ref[:, cs]), 0.0)).astype(BF16)
        acc = acc + _dot(a, w2_ref[cs, :])
    x = x + gate * acc
    if final_norm:
        ms = jnp.mean(jnp.square(x), axis=-1, keepdims=True)
        x = x * lax.rsqrt(ms + EPS) * fg_ref[...]
    o_ref[...] = x


def _mlp(x, mod, g, w1, w2, final_g, tm):
    b, t, d = x.shape
    per_batch_mod = mod.shape[0] != 1
    final_norm = final_g is not None
    row_spec = pl.BlockSpec((None, tm, d), lambda bi, i: (bi, i, 0))
    mod_idx = (lambda bi, i: (bi, 0, 0)) if per_batch_mod else (lambda bi, i: (0, 0, 0))
    in_specs = [row_spec, pl.BlockSpec((None, 1, mod.shape[2]), mod_idx), _const_spec((1, d)),
                _const_spec(w1.shape), _const_spec(w2.shape)]
    args = [x, mod, g.reshape(1, d), w1, w2]
    if final_norm:
        in_specs.append(_const_spec((1, d)))
        args.append(final_g.reshape(1, d))
    return pl.pallas_call(
        functools.partial(_mlp_kernel, d=d, ff_block=1024, final_norm=final_norm),
        grid=(b, t // tm),
        in_specs=in_specs,
        out_specs=row_spec,
        out_shape=jax.ShapeDtypeStruct((b, t, d), F32),
        compiler_params=pltpu.CompilerParams(
            dimension_semantics=("parallel", "parallel"), vmem_limit_bytes=VMEM_LIMIT),
        name="mlp",
    )(*args)


def _grid_pos_embed(n_tokens, dim):
    rows = n_tokens // GRID_W
    r = jnp.repeat(jnp.arange(rows, dtype=F32), GRID_W)
    col = jnp.tile(jnp.arange(GRID_W, dtype=F32), rows)
    quarter = dim // 4
    omega = 1.0 / (POS_BASE ** (jnp.arange(quarter, dtype=F32) / quarter))
    ar = r[:, None] * omega
    ac = col[:, None] * omega
    return jnp.concatenate([jnp.sin(ar), jnp.cos(ar), jnp.sin(ac), jnp.cos(ac)], axis=-1)


def _split_w_in(w_in, dk_all, dv_all, s5w, d):
    splits = (dk_all, dk_all, dv_all, dv_all, N_DIR * GATE_RANK, s5w, d, d)
    idx = np.cumsum((0,) + splits)
    parts = [w_in[:, idx[j]:idx[j + 1]] for j in range(len(splits))]
    parts[4] = jnp.pad(parts[4], ((0, 0), (0, LANES - splits[4])))
    return [p.astype(BF16) for p in parts]


def _stream(x, pos, mods, gla_s0, s5_s0, layers, final_g, emit_state, tm):
    b, t, d = x.shape
    gla_states, s5_re, s5_im = [], [], []
    for l, p in enumerate(layers):
        res = _inproj(x, pos if l == 0 else None, mods[l], p["norm1_g"], p["w_in"], tm)
        if l == 0 and pos is not None:
            x = res[0]
            res = res[1:]
        q, k, v, r, glr, u_tm, ga, gb = res
        s5_out = _s5_scan(u_tm, p["s5"], None if s5_s0 is None else s5_s0[l], b, emit_state,
                          tt=min(t, 128))
        gla_out = _gla_scan(q, k, v, glr, p["wup"], p["bg"],
                            None if gla_s0 is None else gla_s0[l], emit_state, tt=min(t, 256))
        o_f, o_b = gla_out[:2]
        if emit_state:
            y, f_re, f_im = s5_out
            gla_states.append(gla_out[2])
            s5_re.append(f_re)
            s5_im.append(f_im)
        else:
            y = s5_out
        x = _mixout(x, o_f, o_b, r, ga, gb, u_tm, y, mods[l], p["gla_norm_g"], p["s5_d"], p["w_glu"],
                    p["b_glu"], p["w_proj_gla"], p["w_proj_s5"], p["w_out"], tm)
        last = l == len(layers) - 1
        x = _mlp(x, mods[l], p["norm2_g"], p["w_ff1"], p["w_ff2"], final_g if last else None, tm)
    return x, gla_states, s5_re, s5_im


def kernel(x_prompt, x_sample, c, cache_gla_state, state_s5_re, state_s5_im, c_ctx, w_mod, b_mod,
           norm1_g, w_in, w_gate_up, b_gate, gla_norm_g, w_proj_gla, s5_lam_re, s5_lam_im,
           s5_log_step, s5_b_re, s5_b_im, s5_c_re, s5_c_im, s5_d, w_glu, b_glu, w_proj_s5, w_out,
           norm2_g, w_ff1, w_ff2, final_g):
    depth = w_in.shape[0]
    nb, seq, d = x_prompt.shape
    db, dseq, _ = x_sample.shape
    dk_all = w_gate_up.shape[-1]
    dv_all = w_proj_gla.shape[1]
    s5w = s5_d.shape[-1]
    n_groups, n_state = s5_lam_re.shape[2], s5_lam_re.shape[3]
    dk, dv = dk_all // N_HEADS, dv_all // N_HEADS
    assert nb % SUBLANES == 0 and db % SUBLANES == 0

    n_cond = -(-(db + 1) // SUBLANES) * SUBLANES
    conds = jnp.concatenate([c, c_ctx[None], jnp.zeros((n_cond - db - 1, d), F32)], axis=0)
    mod_all = _modulation(conds, w_mod, b_mod)
    mods_lat = [mod_all[l, :db].reshape(db, 1, N_MOD * d) for l in range(depth)]
    mods_ctx = [mod_all[l, db:db + 1].reshape(1, 1, N_MOD * d) for l in range(depth)]

    layers = []
    for l in range(depth):
        wup = jnp.zeros((N_DIR, LANES, dk_all), F32)
        for dd in range(N_DIR):
            wup = wup.at[dd, dd * GATE_RANK:(dd + 1) * GATE_RANK].set(w_gate_up[l, dd])
        layers.append(dict(
            norm1_g=norm1_g[l], w_in=_split_w_in(w_in[l], dk_all, dv_all, s5w, d),
            wup=wup.astype(BF16), bg=b_gate[l].reshape(N_DIR, 1, dk_all),
            gla_norm_g=gla_norm_g[l], w_proj_gla=w_proj_gla[l].astype(BF16),
            s5=_s5_params(s5_lam_re[l], s5_lam_im[l], s5_log_step[l], s5_b_re[l], s5_b_im[l],
                          s5_c_re[l], s5_c_im[l]),
            s5_d=s5_d[l], w_glu=w_glu[l].astype(BF16), b_glu=b_glu[l],
            w_proj_s5=w_proj_s5[l].astype(BF16), w_out=w_out[l].astype(BF16),
            norm2_g=norm2_g[l], w_ff1=w_ff1[l].astype(BF16), w_ff2=w_ff2[l].astype(BF16)))

    y_prompt, gla_states, s5_re, s5_im = _stream(
        x_prompt, None, mods_ctx, None, None, layers, final_g, True, tm=min(seq, 256))
    new_gla_state = jnp.stack(gla_states, axis=1)
    to_state = lambda a: a.transpose(1, 0, 2).reshape(nb, N_DIR, n_groups, n_state)
    new_s5_re = jnp.stack([to_state(a) for a in s5_re], axis=1)
    new_s5_im = jnp.stack([to_state(a) for a in s5_im], axis=1)

    pos = _grid_pos_embed(dseq, d)
    gla_s0 = [cache_gla_state[:, l] for l in range(depth)]
    from_state = lambda a: a.reshape(db, N_DIR, n_groups * n_state).transpose(1, 0, 2)
    s5_s0 = [(from_state(state_s5_re[:, l]), from_state(state_s5_im[:, l])) for l in range(depth)]
    y_sample, _, _, _ = _stream(
        x_sample, pos, mods_lat, gla_s0, s5_s0, layers, final_g, False, tm=min(dseq, 256))

    return (y_prompt, y_sample, new_gla_state, new_s5_re, new_s5_im)
```

```python
import functools
import math

import jax
import jax.numpy as jnp
import numpy as np
from jax import lax
from jax.experimental import pallas as pl
from jax.experimental.pallas import tpu as pltpu

N_DIR = 2
N_HEADS = 4
GATE_RANK = 16
GATE_TAU = 16.0
CHUNK = 64
S5_GROUP = 16
S5_STATE = 64
N_MOD = 6
EPS = 1e-6
GRID_W = 64
POS_BASE = 10000.0

LANES = 128
SUBLANES = 8
VMEM_LIMIT = 56 * 1024 * 1024

F32 = jnp.float32
BF16 = jnp.bfloat16


def _const_spec(shape):
    nd = len(shape)
    return pl.BlockSpec(shape, lambda *_: (0,) * nd, pipeline_mode=pl.Buffered(1))


def _dot(a, b):
    return jnp.dot(a, b, preferred_element_type=F32)


def _rms_mod(x, g, scale, shift):
    ms = jnp.mean(jnp.square(x), axis=-1, keepdims=True)
    return x * lax.rsqrt(ms + EPS) * g * (1.0 + scale) + shift


def _silu(x):
    return x * jax.nn.sigmoid(x)


def _mod_kernel(c_ref, w_ref, b_ref, o_ref):
    a = _silu(c_ref[...]).astype(BF16)
    o_ref[...] = _dot(a, w_ref[...].astype(BF16)) + b_ref[...]


def _modulation(conds, w_mod, b_mod):
    depth, d, dm = w_mod.shape
    rows = conds.shape[0]
    tn = 1024
    return pl.pallas_call(
        _mod_kernel,
        grid=(depth, dm // tn),
        in_specs=[
            pl.BlockSpec((rows, d), lambda l, j: (0, 0)),
            pl.BlockSpec((None, d, tn), lambda l, j: (l, 0, j)),
            pl.BlockSpec((None, 1, tn), lambda l, j: (l, 0, j)),
        ],
        out_specs=pl.BlockSpec((None, rows, tn), lambda l, j: (l, 0, j)),
        out_shape=jax.ShapeDtypeStruct((depth, rows, dm), F32),
        compiler_params=pltpu.CompilerParams(vmem_limit_bytes=VMEM_LIMIT),
        name="modulation",
    )(conds, w_mod, b_mod.reshape(depth, 1, dm))


def _inproj_kernel(*refs, d, add_pos):
    if add_pos:
        x_ref, pos_ref, mod_ref, g_ref = refs[:4]
        rest = refs[4:]
    else:
        x_ref, mod_ref, g_ref = refs[:3]
        rest = refs[3:]
    w_refs = rest[:8]
    out_refs = rest[8:]
    x = x_ref[...]
    if add_pos:
        x = x + pos_ref[...]
        out_refs[0][...] = x
        out_refs = out_refs[1:]
    shift = mod_ref[:, 0:d]
    scale = mod_ref[:, d:2 * d]
    h = _rms_mod(x, g_ref[...], scale, shift).astype(BF16)
    for w_ref, o_ref in zip(w_refs, out_refs):
        o_ref[...] = _dot(h, w_ref[...])


def _inproj(x, pos, mod, g, weights, tm):
    b, t, d = x.shape
    add_pos = pos is not None
    per_batch_mod = mod.shape[0] != 1
    widths = [w.shape[1] for w in weights]
    row_spec = lambda n: pl.BlockSpec((None, tm, n), lambda bi, i: (bi, i, 0))
    in_specs = [row_spec(d)]
    args = [x]
    if add_pos:
        in_specs.append(pl.BlockSpec((tm, d), lambda bi, i: (i, 0)))
        args.append(pos)
    mod_idx = (lambda bi, i: (bi, 0, 0)) if per_batch_mod else (lambda bi, i: (0, 0, 0))
    in_specs += [pl.BlockSpec((None, 1, mod.shape[2]), mod_idx), _const_spec((1, d))]
    args += [mod, g.reshape(1, d)]
    in_specs += [_const_spec(w.shape) for w in weights]
    args += list(weights)

    out_specs, out_shapes = [], []
    if add_pos:
        out_specs.append(row_spec(d))
        out_shapes.append(jax.ShapeDtypeStruct((b, t, d), F32))
    for idx, n in enumerate(widths):
        if idx == 5:
            out_specs.append(pl.BlockSpec((tm, n), lambda bi, i: (i, bi)))
            out_shapes.append(jax.ShapeDtypeStruct((t, b * n), F32))
        else:
            out_specs.append(row_spec(n))
            out_shapes.append(jax.ShapeDtypeStruct((b, t, n), F32))
    return pl.pallas_call(
        functools.partial(_inproj_kernel, d=d, add_pos=add_pos),
        grid=(b, t // tm),
        in_specs=in_specs,
        out_specs=out_specs,
        out_shape=out_shapes,
        compiler_params=pltpu.CompilerParams(
            dimension_semantics=("parallel", "parallel"), vmem_limit_bytes=VMEM_LIMIT),
        name="inproj",
    )(*args)


S5_BLOCK = 4


def _s5_kernel(*refs, tt, w, n_p, zero_init, emit_state):
    u_ref, l4r_ref, l4i_ref, wzr_ref, wzi_ref, wor_ref, woi_ref, wt_ref = refs[:8]
    rest = refs[8:]
    if not zero_init:
        s0r_ref, s0i_ref = rest[:2]
        rest = rest[2:]
    if emit_state:
        y_ref, fr_ref, fi_ref = rest[:3]
        rest = rest[3:]
    else:
        y_ref = rest[0]
        rest = rest[1:]
    usc, xsc, zre, zim, ypk, car, cai = rest
    direction = pl.program_id(1)
    i = pl.program_id(2)
    n_slab = w // LANES
    n_blk = tt // S5_BLOCK
    half = LANES // 2
    tw = 2 * LANES
    grp = S5_BLOCK * SUBLANES

    @pl.when(i == 0)
    def _():
        if zero_init:
            car[...] = jnp.zeros_like(car)
            cai[...] = jnp.zeros_like(cai)
        else:
            car[...] = s0r_ref[...]
            cai[...] = s0i_ref[...]

    for b in range(SUBLANES):
        for s in range(n_slab):
            usc[s, pl.ds(b, tt, stride=SUBLANES), :] = u_ref[:, b * w + s * LANES:b * w + (s + 1) * LANES]

    low = lax.broadcasted_iota(jnp.int32, (2 * SUBLANES, LANES), 1) < half
    swap = lambda a: pltpu.roll(a, half, 1)

    def pack(s):
        for r2 in range(n_blk // 2):
            a = [jnp.concatenate([usc[s, pl.ds(base + j * SUBLANES, SUBLANES), :]
                                  for base in (2 * r2 * grp, (2 * r2 + 1) * grp)], axis=0)
                 for j in range(S5_BLOCK)]
            asw = [swap(v) for v in a]
            for h in range(2):
                pick = lambda j, slot_half: a[j] if h == slot_half else asw[j]
                cols = [jnp.where(low, pick(2 * c, 0), pick(2 * c + 1, 1)) for c in range(2)]
                xsc[2 * s + h, pl.ds(r2 * 2 * SUBLANES, 2 * SUBLANES), :] = (
                    jnp.concatenate(cols, axis=1).astype(BF16))

    def block_inputs(p):
        x = xsc[p]
        zre[:, p * tw:(p + 1) * tw] = _dot(x, wzr_ref[p])
        zim[:, p * tw:(p + 1) * tw] = _dot(x, wzi_ref[p])

    def scan(s):
        ls = slice(s * 2 * tw, (s + 1) * 2 * tw)
        lr = l4r_ref[:, ls]
        li = l4i_ref[:, ls]
        cr, ci = car[:, ls], cai[:, ls]
        for j in range(n_blk):
            r = j + direction * (n_blk - 1 - 2 * j)
            r0 = pl.multiple_of(r * SUBLANES, SUBLANES)
            zr = zre[pl.ds(r0, SUBLANES), ls]
            zi = zim[pl.ds(r0, SUBLANES), ls]
            zre[pl.ds(r0, SUBLANES), ls] = cr
            zim[pl.ds(r0, SUBLANES), ls] = ci
            cr, ci = lr * cr - li * ci + zr, lr * ci + li * cr + zi
        car[:, ls] = cr
        cai[:, ls] = ci

    def block_outputs(p):
        ypk[p] = (_dot(zre[:, p * tw:(p + 1) * tw].astype(BF16), wor_ref[p])
                  + _dot(zim[:, p * tw:(p + 1) * tw].astype(BF16), woi_ref[p])
                  + _dot(xsc[p], wt_ref[p]))

    def unpack(s):
        for r2 in range(n_blk // 2):
            rows = pl.ds(r2 * 2 * SUBLANES, 2 * SUBLANES)
            for j in range(S5_BLOCK):
                c, slot_half = divmod(j, 2)
                left = ypk[2 * s, rows, c * LANES:(c + 1) * LANES]
                right = ypk[2 * s + 1, rows, c * LANES:(c + 1) * LANES]
                v = jnp.where(low, left if slot_half == 0 else swap(left),
                              right if slot_half == 1 else swap(right))
                usc[s, pl.ds(2 * r2 * grp + j * SUBLANES, SUBLANES), :] = v[:SUBLANES]
                usc[s, pl.ds((2 * r2 + 1) * grp + j * SUBLANES, SUBLANES), :] = v[SUBLANES:]

    for s in range(n_slab + 1):
        if s < n_slab:
            pack(s)
            block_inputs(2 * s)
            block_inputs(2 * s + 1)
        if s >= 1:
            block_outputs(2 * s - 2)
            block_outputs(2 * s - 1)
            unpack(s - 1)
        if s < n_slab:
            scan(s)

    for b in range(SUBLANES):
        for s in range(n_slab):
            y_ref[:, b * w + s * LANES:b * w + (s + 1) * LANES] = usc[s, pl.ds(b, tt, stride=SUBLANES), :]

    if emit_state:
        @pl.when(i == pl.num_programs(2) - 1)
        def _():
            fr_ref[...] = car[...]
            fi_ref[...] = cai[...]


def _s5_scan(u_tm, prm, s0, b, emit_state, tt):
    t = u_tm.shape[0]
    w = u_tm.shape[1] // b
    l4r, l4i = prm[:2]
    mats = prm[2:]
    n_p = mats[0].shape[1]
    ns = l4r.shape[-1]
    n_t = t // tt
    n_g = b // SUBLANES
    rows = tt // S5_BLOCK * SUBLANES
    zero_init = s0 is None
    tile = lambda g, d, i: i + d * (n_t - 1 - 2 * i)
    dir_spec = lambda shape: pl.BlockSpec((None,) + shape, lambda g, d, i: (d,) + (0,) * len(shape))
    in_specs = [pl.BlockSpec((tt, SUBLANES * w), lambda g, d, i: (tile(g, d, i), g)),
                dir_spec((SUBLANES, ns)), dir_spec((SUBLANES, ns))]
    in_specs += [dir_spec(m.shape[1:]) for m in mats]
    args = [u_tm, l4r, l4i] + list(mats)
    state_spec = pl.BlockSpec((None, SUBLANES, ns), lambda g, d, i: (d, g, 0))
    if not zero_init:
        in_specs += [state_spec, state_spec]
        args += list(s0)
    out_specs = [pl.BlockSpec((None, tt, SUBLANES * w), lambda g, d, i: (d, tile(g, d, i), g))]
    out_shapes = [jax.ShapeDtypeStruct((N_DIR, t, b * w), F32)]
    if emit_state:
        out_specs += [state_spec, state_spec]
        out_shapes += [jax.ShapeDtypeStruct((N_DIR, b, ns), F32)] * 2
    res = pl.pallas_call(
        functools.partial(_s5_kernel, tt=tt, w=w, n_p=n_p, zero_init=zero_init,
                          emit_state=emit_state),
        grid=(n_g, N_DIR, n_t),
        in_specs=in_specs,
        out_specs=out_specs,
        out_shape=out_shapes,
        scratch_shapes=[pltpu.VMEM((w // LANES, tt * SUBLANES, LANES), F32),
                        pltpu.VMEM((n_p, rows, 2 * LANES), BF16),
                        pltpu.VMEM((rows, ns), F32), pltpu.VMEM((rows, ns), F32),
                        pltpu.VMEM((n_p, rows, 2 * LANES), F32),
                        pltpu.VMEM((SUBLANES, ns), F32), pltpu.VMEM((SUBLANES, ns), F32)],
        compiler_params=pltpu.CompilerParams(
            dimension_semantics=("parallel", "parallel", "arbitrary"), vmem_limit_bytes=VMEM_LIMIT),
        name="s5_scan",
    )(*args)
    if emit_state:
        return res[0], res[1], res[2]
    return res[0]


def _s5_params(lam_re, lam_im, log_step, b_re, b_im, c_re, c_im):
    depth, n_dir, g, p = lam_re.shape
    n = b_re.shape[-1]
    blk = S5_BLOCK
    gpt = 2 * LANES // p
    n_p = g // gpt
    tw = 2 * LANES
    assert gpt * n * blk == tw and n_dir == N_DIR
    hi = lax.Precision.HIGHEST
    step = jnp.exp(log_step)[..., None]
    a = lam_re * step
    th = lam_im * step
    k = jnp.arange(blk + 1, dtype=F32).reshape(-1, 1, 1, 1, 1)
    pw_re = jnp.exp(k * a) * jnp.cos(k * th)
    pw_im = jnp.exp(k * a) * jnp.sin(k * th)
    den = lam_re * lam_re + lam_im * lam_im
    fr = ((pw_re[1] - 1.0) * lam_re + pw_im[1] * lam_im) / den
    fi = (pw_im[1] * lam_re - (pw_re[1] - 1.0) * lam_im) / den
    bbr = fr[..., None] * b_re - fi[..., None] * b_im
    bbi = fr[..., None] * b_im + fi[..., None] * b_re
    lb_re = pw_re[..., None] * bbr - pw_im[..., None] * bbi
    lb_im = pw_re[..., None] * bbi + pw_im[..., None] * bbr
    eye = jnp.eye(gpt, dtype=F32)
    offs = np.arange(blk)
    pos = np.stack([offs, blk - 1 - offs])

    def per_dir(arr, idx):
        return jnp.stack([arr[idx[d], :, d] for d in range(n_dir)], axis=idx.ndim)

    split = lambda m, ax: m.reshape(m.shape[:ax] + (n_p, gpt) + m.shape[ax + 1:])
    wz = [jnp.einsum("jldpgsn,gh->ldpjgnhs", split(per_dir(m, blk - 1 - pos), 3), eye)
          .reshape(depth, n_dir, n_p, tw, tw) for m in (lb_re, lb_im)]
    pr = per_dir(pw_re, pos + 1)[:, :, :, :, None, :]
    pi = per_dir(pw_im, pos + 1)[:, :, :, :, None, :]
    wo = [jnp.einsum("jldpgns,gh->ldpgsjhn", split(m, 3), eye).reshape(depth, n_dir, n_p, tw, tw)
          for m in (c_re * pr - c_im * pi, -(c_re * pi + c_im * pr))]
    kern = (jnp.einsum("ldgnp,mldgpq->mldgqn", c_re, lb_re[:blk], precision=hi)
            - jnp.einsum("ldgnp,mldgpq->mldgqn", c_im, lb_im[:blk], precision=hi))
    lag = pos[:, None, :] - pos[:, :, None]
    valid = jnp.asarray((lag >= 0).transpose(1, 2, 0), F32).reshape(blk, blk, 1, n_dir, 1, 1, 1)
    wt = jnp.einsum("ijldpgqn,gh->ldpigqjhn", split(per_dir(kern, np.maximum(lag, 0)) * valid, 4),
                    eye).reshape(depth, n_dir, n_p, tw, tw)
    bcast = lambda v: jnp.broadcast_to(v.reshape(depth, n_dir, 1, g * p),
                                       (depth, n_dir, SUBLANES, g * p))
    mats = [m.astype(BF16) for m in wz + wo + [wt]]
    return [[v[l] for v in [bcast(pw_re[blk]), bcast(pw_im[blk])] + mats] for l in range(depth)]


def _log_sigmoid(z):
    return -(jnp.maximum(-z, 0.0) + jnp.log(1.0 + jnp.exp(-jnp.abs(z))))


def _split3(x):
    hi = x.astype(BF16)
    r1 = x - hi.astype(F32)
    mid = r1.astype(BF16)
    lo = (r1 - mid.astype(F32)).astype(BF16)
    return hi, mid, lo


def _gla_kernel(*refs, n_chunks, dk, dv, zero_init, emit_state):
    dir_refs = (refs[0:4], refs[4:8])
    wup_ref, bg_ref = refs[8:10]
    rest = refs[10:]
    if not zero_init:
        s0_ref = rest[0]
        rest = rest[1:]
    if emit_state:
        of_ref, ob_ref, sf_ref, st, prep, dec = rest
    else:
        of_ref, ob_ref, st, prep, dec = rest
    o_refs = (of_ref, ob_ref)
    i = pl.program_id(1)
    scale = dk ** -0.5

    @pl.when(i == 0)
    def _():
        for d in range(N_DIR):
            for h in range(N_HEADS):
                if zero_init:
                    st[d, h] = jnp.zeros((dv, dk), F32)
                else:
                    st[d, h] = s0_ref[d, h].T

    row = lax.broadcasted_iota(jnp.int32, (CHUNK, CHUNK), 0)
    col = lax.broadcasted_iota(jnp.int32, (CHUNK, CHUNK), 1)
    causal = (col <= row, col >= row)
    ones = tuple(jnp.where(c, 1.0, 0.0).astype(BF16) for c in causal)

    dims = (((1,), (1,)), ((), ()))
    chunk_of = lambda d, cc: cc if d == 0 else n_chunks - 1 - cc
    rows_of = lambda d, cc: pl.ds(chunk_of(d, cc) * CHUNK, CHUNK)

    def gate_z(d, cc):
        glr_ref = dir_refs[d][3]
        return _dot(glr_ref[rows_of(d, cc), :].astype(BF16), wup_ref[d]) + bg_ref[d]

    def gate_cum(d, g):
        return sum(_dot(ones[d], part) for part in _split3(g))

    def gate_store(slot, d, cc, g, cum):
        q_ref, k_ref = dir_refs[d][:2]
        rows = rows_of(d, cc)
        total = jnp.sum(g, axis=0, keepdims=True)
        prep[slot, d, 0] = (q_ref[rows, :] * scale * jnp.exp(cum)).astype(BF16)
        kk = k_ref[rows, :]
        prep[slot, d, 1] = (kk * jnp.exp(-cum)).astype(BF16)
        prep[slot, d, 2] = (kk * jnp.exp(total - cum)).astype(BF16)
        dec[slot, d] = jnp.broadcast_to(jnp.exp(total), (SUBLANES, total.shape[1]))

    pairs = [(d, h) for d in range(N_DIR) for h in range(N_HEADS)]
    ksl = lambda h: slice(h * dk, (h + 1) * dk)
    vsl = lambda h: slice(h * dv, (h + 1) * dv)

    for d in range(N_DIR):
        g0 = _log_sigmoid(gate_z(d, 0)) / GATE_TAU
        gate_store(0, d, 0, g0, gate_cum(d, g0))

    for cc in range(n_chunks):
        slot = cc % 2
        more = cc + 1 < n_chunks
        if more:
            z_next = [gate_z(d, cc + 1) for d in range(N_DIR)]
        sc = {}
        for d, h in pairs:
            s = lax.dot_general(prep[slot, d, 0, :, ksl(h)], prep[slot, d, 1, :, ksl(h)], dims,
                                preferred_element_type=F32)
            sc[d, h] = jnp.where(causal[d], s, 0.0).astype(BF16)
        if more:
            g_next = [_log_sigmoid(z) / GATE_TAU for z in z_next]
        for d, h in pairs:
            v_ref = dir_refs[d][2]
            rows = rows_of(d, cc)
            vh = v_ref[rows, vsl(h)].astype(BF16)
            o = _dot(sc[d, h], vh) + lax.dot_general(
                prep[slot, d, 0, :, ksl(h)], st[d, h].astype(BF16), dims,
                preferred_element_type=F32)
            o_refs[d][rows, vsl(h)] = o
        if more:
            cum_next = [gate_cum(d, g_next[d]) for d in range(N_DIR)]
        for d, h in pairs:
            v_ref = dir_refs[d][2]
            vh = v_ref[rows_of(d, cc), vsl(h)].astype(BF16)
            upd = lax.dot_general(vh, prep[slot, d, 2, :, ksl(h)], (((0,), (0,)), ((), ())),
                                  preferred_element_type=F32)
            st[d, h] = dec[slot, d, 0:1, ksl(h)] * st[d, h] + upd
        if more:
            for d in range(N_DIR):
                gate_store(1 - slot, d, cc + 1, g_next[d], cum_next[d])

    if emit_state:
        @pl.when(i == pl.num_programs(1) - 1)
        def _():
            for d in range(N_DIR):
                for h in range(N_HEADS):
                    sf_ref[d, h] = st[d, h].T


def _gla_scan(q, k, v, glr, wup, bg, s0, emit_state, tt):
    b, t, dkk = q.shape
    dvv = v.shape[-1]
    dk, dv = dkk // N_HEADS, dvv // N_HEADS
    n_t = t // tt
    zero_init = s0 is None
    fwd_spec = lambda n: pl.BlockSpec((None, tt, n), lambda bi, i: (bi, i, 0))
    bwd_spec = lambda n: pl.BlockSpec((None, tt, n), lambda bi, i: (bi, n_t - 1 - i, 0))
    state_spec = pl.BlockSpec((None, N_DIR, N_HEADS, dk, dv), lambda bi, i: (bi, 0, 0, 0, 0))
    widths = (dkk, dkk, dvv, glr.shape[-1])
    in_specs = ([fwd_spec(n) for n in widths] + [bwd_spec(n) for n in widths]
                + [_const_spec(wup.shape), _const_spec(bg.shape)])
    args = [q, k, v, glr, q, k, v, glr, wup, bg]
    if not zero_init:
        in_specs.append(state_spec)
        args.append(s0)
    out_specs = [fwd_spec(dvv), bwd_spec(dvv)]
    out_shapes = [jax.ShapeDtypeStruct((b, t, dvv), F32)] * 2
    if emit_state:
        out_specs.append(state_spec)
        out_shapes.append(jax.ShapeDtypeStruct((b, N_DIR, N_HEADS, dk, dv), F32))
    return pl.pallas_call(
        functools.partial(_gla_kernel, n_chunks=tt // CHUNK, dk=dk, dv=dv,
                          zero_init=zero_init, emit_state=emit_state),
        grid=(b, n_t),
        in_specs=in_specs,
        out_specs=out_specs,
        out_shape=out_shapes,
        scratch_shapes=[pltpu.VMEM((N_DIR, N_HEADS, dv, dk), F32),
                        pltpu.VMEM((2, N_DIR, 3, CHUNK, dkk), BF16),
                        pltpu.VMEM((2, N_DIR, SUBLANES, dkk), F32)],
        compiler_params=pltpu.CompilerParams(
            dimension_semantics=("parallel", "arbitrary"), vmem_limit_bytes=VMEM_LIMIT),
        name="gla_scan",
    )(*args)


def _gelu_tanh(x):
    return 0.5 * x * (1.0 + jnp.tanh(math.sqrt(2.0 / math.pi) * (x + 0.044715 * (x * x * x))))


def _mixmlp_kernel(*refs, d, dv, ff_block, final_norm):
    (x_ref, of_ref, ob_ref, r_ref, ga_ref, gb_ref, u_ref, yf_ref, yb_ref, mod_ref, gn_ref, d_ref,
     wglu_ref, bglu_ref, wpg_ref, wps_ref, wout_ref, g2_ref, w1_ref, w2_ref) = refs[:20]
    if final_norm:
        fg_ref, o_ref = refs[20:]
    else:
        o_ref = refs[20]
    o = of_ref[...] + ob_ref[...]
    r = r_ref[...]
    gn = gn_ref[...]
    parts = []
    for h in range(N_HEADS):
        vs = slice(h * dv, (h + 1) * dv)
        oh = o[:, vs]
        ms = jnp.mean(jnp.square(oh), axis=-1, keepdims=True)
        parts.append((oh * lax.rsqrt(ms + EPS) * gn * _silu(r[:, vs])).astype(BF16))
    pg = _dot(jnp.concatenate(parts, axis=-1), wpg_ref[...])

    u = u_ref[...]
    y = _gelu_tanh(yf_ref[...] + yb_ref[...] + d_ref[...] * u)
    y = y * jax.nn.sigmoid(_dot(y.astype(BF16), wglu_ref[...]) + bglu_ref[...])
    ps = _dot(y.astype(BF16), wps_ref[...])

    merged = jax.nn.sigmoid(ga_ref[...]) * pg + jax.nn.sigmoid(gb_ref[...]) * ps
    gate = mod_ref[:, 2 * d:3 * d]
    x = x_ref[...] + gate * _dot(merged.astype(BF16), wout_ref[...])

    shift = mod_ref[:, 3 * d:4 * d]
    scale = mod_ref[:, 4 * d:5 * d]
    gate = mod_ref[:, 5 * d:6 * d]
    h = _rms_mod(x, g2_ref[...], scale, shift).astype(BF16)
    acc = jnp.zeros(x.shape, F32)
    for j in range(w1_ref.shape[1] // ff_block):
        cs = slice(j * ff_block, (j + 1) * ff_block)
        a = jnp.square(jnp.maximum(_dot(h, w1_ref[:, cs]), 0.0)).astype(BF16)
        acc = acc + _dot(a, w2_ref[cs, :])
    x = x + gate * acc
    if final_norm:
        ms = jnp.mean(jnp.square(x), axis=-1, keepdims=True)
        x = x * lax.rsqrt(ms + EPS) * fg_ref[...]
    o_ref[...] = x


def _mixmlp(x, o_f, o_b, r, ga, gb, u_tm, y, mod, p, final_g, tm):
    b, t, d = x.shape
    dvv = r.shape[-1]
    w = u_tm.shape[1] // b
    per_batch_mod = mod.shape[0] != 1
    final_norm = final_g is not None
    row_spec = lambda n: pl.BlockSpec((None, tm, n), lambda bi, i: (bi, i, 0))
    mod_idx = (lambda bi, i: (bi, 0, 0)) if per_batch_mod else (lambda bi, i: (0, 0, 0))
    weights = [p["w_glu"], p["b_glu"].reshape(1, -1), p["w_proj_gla"], p["w_proj_s5"], p["w_out"],
               p["norm2_g"].reshape(1, d), p["w_ff1"], p["w_ff2"]]
    if final_norm:
        weights.append(final_g.reshape(1, d))
    in_specs = [
        row_spec(d), row_spec(dvv), row_spec(dvv), row_spec(dvv), row_spec(d), row_spec(d),
        pl.BlockSpec((tm, w), lambda bi, i: (i, bi)),
        pl.BlockSpec((None, tm, w), lambda bi, i: (0, i, bi)),
        pl.BlockSpec((None, tm, w), lambda bi, i: (1, i, bi)),
        pl.BlockSpec((None, 1, mod.shape[2]), mod_idx),
        _const_spec((1, dvv // N_HEADS)), _const_spec((1, w)),
    ] + [_const_spec(a.shape) for a in weights]
    return pl.pallas_call(
        functools.partial(_mixmlp_kernel, d=d, dv=dvv // N_HEADS, ff_block=1024,
                          final_norm=final_norm),
        grid=(b, t // tm),
        in_specs=in_specs,
        out_specs=row_spec(d),
        out_shape=jax.ShapeDtypeStruct((b, t, d), F32),
        compiler_params=pltpu.CompilerParams(
            dimension_semantics=("parallel", "parallel"), vmem_limit_bytes=VMEM_LIMIT),
        name="mixmlp",
    )(x, o_f, o_b, r, ga, gb, u_tm, y, y, mod, p["gla_norm_g"].reshape(1, -1),
      p["s5_d"].reshape(1, -1), *weights)


def _grid_pos_embed(n_tokens, dim):
    rows = n_tokens // GRID_W
    r = jnp.repeat(jnp.arange(rows, dtype=F32), GRID_W)
    col = jnp.tile(jnp.arange(GRID_W, dtype=F32), rows)
    quarter = dim // 4
    omega = 1.0 / (POS_BASE ** (jnp.arange(quarter, dtype=F32) / quarter))
    ar = r[:, None] * omega
    ac = col[:, None] * omega
    return jnp.concatenate([jnp.sin(ar), jnp.cos(ar), jnp.sin(ac), jnp.cos(ac)], axis=-1)


def _split_w_in(w_in, dk_all, dv_all, s5w, d):
    splits = (dk_all, dk_all, dv_all, dv_all, N_DIR * GATE_RANK, s5w, d, d)
    idx = np.cumsum((0,) + splits)
    parts = [w_in[:, idx[j]:idx[j + 1]] for j in range(len(splits))]
    parts[4] = jnp.pad(parts[4], ((0, 0), (0, LANES - splits[4])))
    return [p.astype(BF16) for p in parts]


def _stream(x, pos, mods, gla_s0, s5_s0, layers, final_g, emit_state, tm):
    b, t, d = x.shape
    gla_states, s5_re, s5_im = [], [], []
    for l, p in enumerate(layers):
        res = _inproj(x, pos if l == 0 else None, mods[l], p["norm1_g"], p["w_in"], tm)
        if l == 0 and pos is not None:
            x = res[0]
            res = res[1:]
        q, k, v, r, glr, u_tm, ga, gb = res
        s5_out = _s5_scan(u_tm, p["s5"], None if s5_s0 is None else s5_s0[l], b, emit_state,
                          tt=min(t, 128))
        gla_out = _gla_scan(q, k, v, glr, p["wup"], p["bg"],
                            None if gla_s0 is None else gla_s0[l], emit_state, tt=min(t, 256))
        o_f, o_b = gla_out[:2]
        if emit_state:
            y, f_re, f_im = s5_out
            gla_states.append(gla_out[2])
            s5_re.append(f_re)
            s5_im.append(f_im)
        else:
            y = s5_out
        last = l == len(layers) - 1
        x = _mixmlp(x, o_f, o_b, r, ga, gb, u_tm, y, mods[l], p, final_g if last else None, tm)
    return x, gla_states, s5_re, s5_im


def kernel(x_prompt, x_sample, c, cache_gla_state, state_s5_re, state_s5_im, c_ctx, w_mod, b_mod,
           norm1_g, w_in, w_gate_up, b_gate, gla_norm_g, w_proj_gla, s5_lam_re, s5_lam_im,
           s5_log_step, s5_b_re, s5_b_im, s5_c_re, s5_c_im, s5_d, w_glu, b_glu, w_proj_s5, w_out,
           norm2_g, w_ff1, w_ff2, final_g):
    depth = w_in.shape[0]
    nb, seq, d = x_prompt.shape
    db, dseq, _ = x_sample.shape
    dk_all = w_gate_up.shape[-1]
    dv_all = w_proj_gla.shape[1]
    s5w = s5_d.shape[-1]
    n_groups, n_state = s5_lam_re.shape[2], s5_lam_re.shape[3]
    dk, dv = dk_all // N_HEADS, dv_all // N_HEADS
    assert nb % SUBLANES == 0 and db % SUBLANES == 0

    n_cond = -(-(db + 1) // SUBLANES) * SUBLANES
    conds = jnp.concatenate([c, c_ctx[None], jnp.zeros((n_cond - db - 1, d), F32)], axis=0)
    mod_all = _modulation(conds, w_mod, b_mod)
    mods_lat = [mod_all[l, :db].reshape(db, 1, N_MOD * d) for l in range(depth)]
    mods_ctx = [mod_all[l, db:db + 1].reshape(1, 1, N_MOD * d) for l in range(depth)]

    s5_prm = _s5_params(s5_lam_re, s5_lam_im, s5_log_step, s5_b_re, s5_b_im, s5_c_re, s5_c_im)
    wup = jnp.stack([jnp.pad(w_gate_up[:, dd], ((0, 0), (dd * GATE_RANK, LANES - (dd + 1) * GATE_RANK),
                                                (0, 0))) for dd in range(N_DIR)], axis=1).astype(BF16)
    layers = []
    for l in range(depth):
        layers.append(dict(
            norm1_g=norm1_g[l], w_in=_split_w_in(w_in[l], dk_all, dv_all, s5w, d),
            wup=wup[l], bg=b_gate[l].reshape(N_DIR, 1, dk_all),
            gla_norm_g=gla_norm_g[l], w_proj_gla=w_proj_gla[l].astype(BF16),
            s5=s5_prm[l],
            s5_d=s5_d[l], w_glu=w_glu[l].astype(BF16), b_glu=b_glu[l],
            w_proj_s5=w_proj_s5[l].astype(BF16), w_out=w_out[l].astype(BF16),
            norm2_g=norm2_g[l], w_ff1=w_ff1[l].astype(BF16), w_ff2=w_ff2[l].astype(BF16)))

    y_prompt, gla_states, s5_re, s5_im = _stream(
        x_prompt, None, mods_ctx, None, None, layers, final_g, True, tm=min(seq, 256))
    new_gla_state = jnp.stack(gla_states, axis=1)
    to_state = lambda a: a.transpose(1, 0, 2).reshape(nb, N_DIR, n_groups, n_state)
    new_s5_re = jnp.stack([to_state(a) for a in s5_re], axis=1)
    new_s5_im = jnp.stack([to_state(a) for a in s5_im], axis=1)

    pos = _grid_pos_embed(dseq, d)
    gla_s0 = [cache_gla_state[:, l] for l in range(depth)]
    from_state = lambda a: a.reshape(db, N_DIR, n_groups * n_state).transpose(1, 0, 2)
    s5_s0 = [(from_state(state_s5_re[:, l]), from_state(state_s5_im[:, l])) for l in range(depth)]
    y_sample, _, _, _ = _stream(
        x_sample, pos, mods_lat, gla_s0, s5_s0, layers, final_g, False, tm=min(dseq, 256))

    return (y_prompt, y_sample, new_gla_state, new_s5_re, new_s5_im)
```

```python
import functools
import math

import jax
import jax.numpy as jnp
import numpy as np
from jax import lax
from jax.experimental import pallas as pl
from jax.experimental.pallas import tpu as pltpu

N_DIR = 2
N_HEADS = 4
GATE_RANK = 16
GATE_TAU = 16.0
CHUNK = 64
S5_GROUP = 16
S5_STATE = 64
N_MOD = 6
EPS = 1e-6
GRID_W = 64
POS_BASE = 10000.0

LANES = 128
SUBLANES = 8
VMEM_LIMIT = 56 * 1024 * 1024

F32 = jnp.float32
BF16 = jnp.bfloat16


def _const_spec(shape):
    nd = len(shape)
    return pl.BlockSpec(shape, lambda *_: (0,) * nd, pipeline_mode=pl.Buffered(1))


def _dot(a, b):
    return jnp.dot(a, b, preferred_element_type=F32)


def _rms_mod(x, g, scale, shift):
    ms = jnp.mean(jnp.square(x), axis=-1, keepdims=True)
    return x * lax.rsqrt(ms + EPS) * g * (1.0 + scale) + shift


def _silu(x):
    return x * jax.nn.sigmoid(x)


def _mod_kernel(c_ref, w_ref, b_ref, o_ref):
    a = _silu(c_ref[...]).astype(BF16)
    o_ref[...] = _dot(a, w_ref[...].astype(BF16)) + b_ref[...]


def _modulation(conds, w_mod, b_mod):
    depth, d, dm = w_mod.shape
    rows = conds.shape[0]
    tn = 1024
    return pl.pallas_call(
        _mod_kernel,
        grid=(depth, dm // tn),
        in_specs=[
            pl.BlockSpec((rows, d), lambda l, j: (0, 0)),
            pl.BlockSpec((None, d, tn), lambda l, j: (l, 0, j)),
            pl.BlockSpec((None, 1, tn), lambda l, j: (l, 0, j)),
        ],
        out_specs=pl.BlockSpec((None, rows, tn), lambda l, j: (l, 0, j)),
        out_shape=jax.ShapeDtypeStruct((depth, rows, dm), F32),
        compiler_params=pltpu.CompilerParams(vmem_limit_bytes=VMEM_LIMIT),
        name="modulation",
    )(conds, w_mod, b_mod.reshape(depth, 1, dm))


def _inproj_kernel(*refs, d, add_pos):
    if add_pos:
        x_ref, pos_ref, mod_ref, g_ref = refs[:4]
        rest = refs[4:]
    else:
        x_ref, mod_ref, g_ref = refs[:3]
        rest = refs[3:]
    w_refs = rest[:8]
    out_refs = rest[8:]
    x = x_ref[...]
    if add_pos:
        x = x + pos_ref[...]
        out_refs[0][...] = x
        out_refs = out_refs[1:]
    shift = mod_ref[:, 0:d]
    scale = mod_ref[:, d:2 * d]
    h = _rms_mod(x, g_ref[...], scale, shift).astype(BF16)
    for w_ref, o_ref in zip(w_refs, out_refs):
        o_ref[...] = _dot(h, w_ref[...])


def _inproj(x, pos, mod, g, weights, tm):
    b, t, d = x.shape
    add_pos = pos is not None
    per_batch_mod = mod.shape[0] != 1
    widths = [w.shape[1] for w in weights]
    row_spec = lambda n: pl.BlockSpec((None, tm, n), lambda bi, i: (bi, i, 0))
    in_specs = [row_spec(d)]
    args = [x]
    if add_pos:
        in_specs.append(pl.BlockSpec((tm, d), lambda bi, i: (i, 0)))
        args.append(pos)
    mod_idx = (lambda bi, i: (bi, 0, 0)) if per_batch_mod else (lambda bi, i: (0, 0, 0))
    in_specs += [pl.BlockSpec((None, 1, mod.shape[2]), mod_idx), _const_spec((1, d))]
    args += [mod, g.reshape(1, d)]
    in_specs += [_const_spec(w.shape) for w in weights]
    args += list(weights)

    out_specs, out_shapes = [], []
    if add_pos:
        out_specs.append(row_spec(d))
        out_shapes.append(jax.ShapeDtypeStruct((b, t, d), F32))
    for idx, n in enumerate(widths):
        if idx == 5:
            out_specs.append(pl.BlockSpec((tm, n), lambda bi, i: (i, bi)))
            out_shapes.append(jax.ShapeDtypeStruct((t, b * n), F32))
        else:
            out_specs.append(row_spec(n))
            out_shapes.append(jax.ShapeDtypeStruct((b, t, n), F32))
    return pl.pallas_call(
        functools.partial(_inproj_kernel, d=d, add_pos=add_pos),
        grid=(b, t // tm),
        in_specs=in_specs,
        out_specs=out_specs,
        out_shape=out_shapes,
        compiler_params=pltpu.CompilerParams(
            dimension_semantics=("parallel", "parallel"), vmem_limit_bytes=VMEM_LIMIT),
        name="inproj",
    )(*args)


S5_BLOCK = 4


def _s5_kernel(*refs, tt, w, n_p, zero_init, emit_state):
    u_ref, l4r_ref, l4i_ref, wzr_ref, wzi_ref, wor_ref, woi_ref, wt_ref = refs[:8]
    rest = refs[8:]
    if not zero_init:
        s0r_ref, s0i_ref = rest[:2]
        rest = rest[2:]
    if emit_state:
        y_ref, fr_ref, fi_ref = rest[:3]
        rest = rest[3:]
    else:
        y_ref = rest[0]
        rest = rest[1:]
    usc, xsc, zre, zim, ypk, car, cai = rest
    direction = pl.program_id(1)
    i = pl.program_id(2)
    n_slab = w // LANES
    n_blk = tt // S5_BLOCK
    half = LANES // 2
    tw = 2 * LANES
    grp = S5_BLOCK * SUBLANES

    @pl.when(i == 0)
    def _():
        if zero_init:
            car[...] = jnp.zeros_like(car)
            cai[...] = jnp.zeros_like(cai)
        else:
            car[...] = s0r_ref[...]
            cai[...] = s0i_ref[...]

    for b in range(SUBLANES):
        for s in range(n_slab):
            usc[s, pl.ds(b, tt, stride=SUBLANES), :] = u_ref[:, b * w + s * LANES:b * w + (s + 1) * LANES]

    low = lax.broadcasted_iota(jnp.int32, (2 * SUBLANES, LANES), 1) < half
    swap = lambda a: pltpu.roll(a, half, 1)

    def pack(s):
        for r2 in range(n_blk // 2):
            a = [jnp.concatenate([usc[s, pl.ds(base + j * SUBLANES, SUBLANES), :]
                                  for base in (2 * r2 * grp, (2 * r2 + 1) * grp)], axis=0)
                 for j in range(S5_BLOCK)]
            asw = [swap(v) for v in a]
            for h in range(2):
                pick = lambda j, slot_half: a[j] if h == slot_half else asw[j]
                cols = [jnp.where(low, pick(2 * c, 0), pick(2 * c + 1, 1)) for c in range(2)]
                xsc[2 * s + h, pl.ds(r2 * 2 * SUBLANES, 2 * SUBLANES), :] = (
                    jnp.concatenate(cols, axis=1).astype(BF16))

    def block_inputs(p):
        x = xsc[p]
        zre[:, p * tw:(p + 1) * tw] = _dot(x, wzr_ref[p])
        zim[:, p * tw:(p + 1) * tw] = _dot(x, wzi_ref[p])

    def scan(s):
        ls = slice(s * 2 * tw, (s + 1) * 2 * tw)
        lr = l4r_ref[:, ls]
        li = l4i_ref[:, ls]
        cr, ci = car[:, ls], cai[:, ls]
        for j in range(n_blk):
            r = j + direction * (n_blk - 1 - 2 * j)
            r0 = pl.multiple_of(r * SUBLANES, SUBLANES)
            zr = zre[pl.ds(r0, SUBLANES), ls]
            zi = zim[pl.ds(r0, SUBLANES), ls]
            zre[pl.ds(r0, SUBLANES), ls] = cr
            zim[pl.ds(r0, SUBLANES), ls] = ci
            cr, ci = lr * cr - li * ci + zr, lr * ci + li * cr + zi
        car[:, ls] = cr
        cai[:, ls] = ci

    def block_outputs(p):
        ypk[p] = (_dot(zre[:, p * tw:(p + 1) * tw].astype(BF16), wor_ref[p])
                  + _dot(zim[:, p * tw:(p + 1) * tw].astype(BF16), woi_ref[p])
                  + _dot(xsc[p], wt_ref[p]))

    def unpack(s):
        for r2 in range(n_blk // 2):
            rows = pl.ds(r2 * 2 * SUBLANES, 2 * SUBLANES)
            for j in range(S5_BLOCK):
                c, slot_half = divmod(j, 2)
                left = ypk[2 * s, rows, c * LANES:(c + 1) * LANES]
                right = ypk[2 * s + 1, rows, c * LANES:(c + 1) * LANES]
                v = jnp.where(low, left if slot_half == 0 else swap(left),
                              right if slot_half == 1 else swap(right))
                usc[s, pl.ds(2 * r2 * grp + j * SUBLANES, SUBLANES), :] = v[:SUBLANES]
                usc[s, pl.ds((2 * r2 + 1) * grp + j * SUBLANES, SUBLANES), :] = v[SUBLANES:]

    for s in range(n_slab + 1):
        if s < n_slab:
            pack(s)
            block_inputs(2 * s)
            block_inputs(2 * s + 1)
        if s >= 1:
            block_outputs(2 * s - 2)
            block_outputs(2 * s - 1)
            unpack(s - 1)
        if s < n_slab:
            scan(s)

    for b in range(SUBLANES):
        for s in range(n_slab):
            y_ref[:, b * w + s * LANES:b * w + (s + 1) * LANES] = usc[s, pl.ds(b, tt, stride=SUBLANES), :]

    if emit_state:
        @pl.when(i == pl.num_programs(2) - 1)
        def _():
            fr_ref[...] = car[...]
            fi_ref[...] = cai[...]


def _s5_scan(u_tm, prm, layer, s0, b, emit_state, tt):
    t = u_tm.shape[0]
    w = u_tm.shape[1] // b
    l4r, l4i = prm[:2]
    mats = prm[2:]
    n_p = mats[0].shape[2]
    ns = l4r.shape[-1]
    n_t = t // tt
    n_g = b // SUBLANES
    rows = tt // S5_BLOCK * SUBLANES
    zero_init = s0 is None
    tile = lambda g, d, i: i + d * (n_t - 1 - 2 * i)
    dir_spec = lambda shape: pl.BlockSpec((None, None) + shape,
                                          lambda g, d, i: (layer, d) + (0,) * len(shape))
    in_specs = [pl.BlockSpec((tt, SUBLANES * w), lambda g, d, i: (tile(g, d, i), g)),
                dir_spec((SUBLANES, ns)), dir_spec((SUBLANES, ns))]
    in_specs += [dir_spec(m.shape[2:]) for m in mats]
    args = [u_tm, l4r, l4i] + list(mats)
    state_spec = pl.BlockSpec((None, SUBLANES, ns), lambda g, d, i: (d, g, 0))
    if not zero_init:
        in_specs += [state_spec, state_spec]
        args += list(s0)
    out_specs = [pl.BlockSpec((None, tt, SUBLANES * w), lambda g, d, i: (d, tile(g, d, i), g))]
    out_shapes = [jax.ShapeDtypeStruct((N_DIR, t, b * w), F32)]
    if emit_state:
        out_specs += [state_spec, state_spec]
        out_shapes += [jax.ShapeDtypeStruct((N_DIR, b, ns), F32)] * 2
    res = pl.pallas_call(
        functools.partial(_s5_kernel, tt=tt, w=w, n_p=n_p, zero_init=zero_init,
                          emit_state=emit_state),
        grid=(n_g, N_DIR, n_t),
        in_specs=in_specs,
        out_specs=out_specs,
        out_shape=out_shapes,
        scratch_shapes=[pltpu.VMEM((w // LANES, tt * SUBLANES, LANES), F32),
                        pltpu.VMEM((n_p, rows, 2 * LANES), BF16),
                        pltpu.VMEM((rows, ns), F32), pltpu.VMEM((rows, ns), F32),
                        pltpu.VMEM((n_p, rows, 2 * LANES), F32),
                        pltpu.VMEM((SUBLANES, ns), F32), pltpu.VMEM((SUBLANES, ns), F32)],
        compiler_params=pltpu.CompilerParams(
            dimension_semantics=("parallel", "parallel", "arbitrary"), vmem_limit_bytes=VMEM_LIMIT),
        name="s5_scan",
    )(*args)
    if emit_state:
        return res[0], res[1], res[2]
    return res[0]


def _s5_params(lam_re, lam_im, log_step, b_re, b_im, c_re, c_im):
    depth, n_dir, g, p = lam_re.shape
    n = b_re.shape[-1]
    blk = S5_BLOCK
    gpt = 2 * LANES // p
    n_p = g // gpt
    tw = 2 * LANES
    assert gpt * n * blk == tw and n_dir == N_DIR
    hi = lax.Precision.HIGHEST
    step = jnp.exp(log_step)[..., None]
    a = lam_re * step
    th = lam_im * step
    k = jnp.arange(blk + 1, dtype=F32).reshape(-1, 1, 1, 1, 1)
    pw_re = jnp.exp(k * a) * jnp.cos(k * th)
    pw_im = jnp.exp(k * a) * jnp.sin(k * th)
    den = lam_re * lam_re + lam_im * lam_im
    fr = ((pw_re[1] - 1.0) * lam_re + pw_im[1] * lam_im) / den
    fi = (pw_im[1] * lam_re - (pw_re[1] - 1.0) * lam_im) / den
    bbr = fr[..., None] * b_re - fi[..., None] * b_im
    bbi = fr[..., None] * b_im + fi[..., None] * b_re
    lb_re = pw_re[..., None] * bbr - pw_im[..., None] * bbi
    lb_im = pw_re[..., None] * bbi + pw_im[..., None] * bbr
    eye = jnp.eye(gpt, dtype=F32)
    offs = np.arange(blk)
    pos = np.stack([offs, blk - 1 - offs])

    def per_dir(arr, idx):
        return jnp.stack([arr[idx[d], :, d] for d in range(n_dir)], axis=idx.ndim)

    split = lambda m, ax: m.reshape(m.shape[:ax] + (n_p, gpt) + m.shape[ax + 1:])
    wz = [jnp.einsum("jldpgsn,gh->ldpjgnhs", split(per_dir(m, blk - 1 - pos), 3), eye)
          .reshape(depth, n_dir, n_p, tw, tw) for m in (lb_re, lb_im)]
    pr = per_dir(pw_re, pos + 1)[:, :, :, :, None, :]
    pi = per_dir(pw_im, pos + 1)[:, :, :, :, None, :]
    wo = [jnp.einsum("jldpgns,gh->ldpgsjhn", split(m, 3), eye).reshape(depth, n_dir, n_p, tw, tw)
          for m in (c_re * pr - c_im * pi, -(c_re * pi + c_im * pr))]
    kern = (jnp.einsum("ldgnp,mldgpq->mldgqn", c_re, lb_re[:blk], precision=hi)
            - jnp.einsum("ldgnp,mldgpq->mldgqn", c_im, lb_im[:blk], precision=hi))
    lag = pos[:, None, :] - pos[:, :, None]
    valid = jnp.asarray((lag >= 0).transpose(1, 2, 0), F32).reshape(blk, blk, 1, n_dir, 1, 1, 1)
    wt = jnp.einsum("ijldpgqn,gh->ldpigqjhn", split(per_dir(kern, np.maximum(lag, 0)) * valid, 4),
                    eye).reshape(depth, n_dir, n_p, tw, tw)
    bcast = lambda v: jnp.broadcast_to(v.reshape(depth, n_dir, 1, g * p),
                                       (depth, n_dir, SUBLANES, g * p))
    return [bcast(pw_re[blk]), bcast(pw_im[blk])] + [m.astype(BF16) for m in wz + wo + [wt]]


def _log_sigmoid(z):
    return -(jnp.maximum(-z, 0.0) + jnp.log(1.0 + jnp.exp(-jnp.abs(z))))


def _split2(x):
    hi = x.astype(BF16)
    return hi, (x - hi.astype(F32)).astype(BF16)


def _gla_kernel(*refs, n_chunks, dk, dv, zero_init, emit_state, aliased_states):
    dir_refs = (refs[0:4], refs[4:8])
    wup_ref, bg_ref = refs[8:10]
    rest = refs[10:]
    if not zero_init:
        s0_ref = rest[0]
        rest = rest[1:]
    if aliased_states:
        rest = rest[1:]
    if emit_state:
        of_ref, ob_ref, sf_ref, st, prep, dec = rest
    else:
        of_ref, ob_ref, st, prep, dec = rest
    o_refs = (of_ref, ob_ref)
    i = pl.program_id(1)
    scale = dk ** -0.5

    @pl.when(i == 0)
    def _():
        for d in range(N_DIR):
            for h in range(N_HEADS):
                if zero_init:
                    st[d, h] = jnp.zeros((dk, dv), F32)
                else:
                    st[d, h] = s0_ref[d, h]

    row = lax.broadcasted_iota(jnp.int32, (CHUNK, CHUNK), 0)
    col = lax.broadcasted_iota(jnp.int32, (CHUNK, CHUNK), 1)
    causal = (col <= row, col >= row)
    ones = tuple(jnp.where(c, 1.0, 0.0).astype(BF16) for c in causal)

    dims = (((1,), (1,)), ((), ()))
    chunk_of = lambda d, cc: cc if d == 0 else n_chunks - 1 - cc
    rows_of = lambda d, cc: pl.ds(chunk_of(d, cc) * CHUNK, CHUNK)

    def gate_z(d, cc):
        glr_ref = dir_refs[d][3]
        return _dot(glr_ref[rows_of(d, cc), :].astype(BF16), wup_ref[d]) + bg_ref[d]

    def gate_cum(d, g):
        return sum(_dot(ones[d], part) for part in _split2(g))

    def gate_store(slot, d, cc, g, cum):
        q_ref, k_ref = dir_refs[d][:2]
        rows = rows_of(d, cc)
        total = jnp.sum(g, axis=0, keepdims=True)
        prep[slot, d, 0] = (q_ref[rows, :] * scale * jnp.exp(cum)).astype(BF16)
        kk = k_ref[rows, :]
        prep[slot, d, 1] = (kk * jnp.exp(-cum)).astype(BF16)
        prep[slot, d, 2] = (kk * jnp.exp(total - cum)).astype(BF16)
        decay = jnp.exp(total)
        for h in range(N_HEADS):
            dec[slot, d, h] = jnp.broadcast_to(decay[:, ksl(h)], (dk, dk)).T

    pairs = [(d, h) for d in range(N_DIR) for h in range(N_HEADS)]
    ksl = lambda h: slice(h * dk, (h + 1) * dk)
    vsl = lambda h: slice(h * dv, (h + 1) * dv)

    for d in range(N_DIR):
        g0 = _log_sigmoid(gate_z(d, 0)) / GATE_TAU
        gate_store(0, d, 0, g0, gate_cum(d, g0))

    for cc in range(n_chunks):
        slot = cc % 2
        more = cc + 1 < n_chunks
        if more:
            z_next = [gate_z(d, cc + 1) for d in range(N_DIR)]
        sc = {}
        for d, h in pairs:
            s = lax.dot_general(prep[slot, d, 0, :, ksl(h)], prep[slot, d, 1, :, ksl(h)], dims,
                                preferred_element_type=F32)
            sc[d, h] = jnp.where(causal[d], s, 0.0).astype(BF16)
        if more:
            g_next = [_log_sigmoid(z) / GATE_TAU for z in z_next]
        for d, h in pairs:
            v_ref = dir_refs[d][2]
            rows = rows_of(d, cc)
            vh = v_ref[rows, vsl(h)].astype(BF16)
            o = _dot(sc[d, h], vh) + _dot(prep[slot, d, 0, :, ksl(h)], st[d, h].astype(BF16))
            o_refs[d][rows, vsl(h)] = o
        if more:
            cum_next = [gate_cum(d, g_next[d]) for d in range(N_DIR)]
        for d, h in pairs:
            v_ref = dir_refs[d][2]
            vh = v_ref[rows_of(d, cc), vsl(h)].astype(BF16)
            upd = lax.dot_general(prep[slot, d, 2, :, ksl(h)], vh, (((0,), (0,)), ((), ())),
                                  preferred_element_type=F32)
            decay = jnp.concatenate([dec[slot, d, h]] * (dv // dk), axis=1)
            st[d, h] = decay * st[d, h] + upd
        if more:
            for d in range(N_DIR):
                gate_store(1 - slot, d, cc + 1, g_next[d], cum_next[d])

    if emit_state:
        @pl.when(i == pl.num_programs(1) - 1)
        def _():
            for d in range(N_DIR):
                for h in range(N_HEADS):
                    sf_ref[d, h] = st[d, h]


def _gla_scan(q, k, v, glr, wup, bg, layer, depth, s0, states, emit_state, tt):
    b, t, dkk = q.shape
    dvv = v.shape[-1]
    dk, dv = dkk // N_HEADS, dvv // N_HEADS
    n_t = t // tt
    zero_init = s0 is None
    fwd_spec = lambda n: pl.BlockSpec((None, tt, n), lambda bi, i: (bi, i, 0))
    bwd_spec = lambda n: pl.BlockSpec((None, tt, n), lambda bi, i: (bi, n_t - 1 - i, 0))
    state_spec = pl.BlockSpec((None, None, N_DIR, N_HEADS, dk, dv), lambda bi, i: (bi, layer, 0, 0, 0, 0))
    widths = (dkk, dkk, dvv, glr.shape[-1])
    in_specs = ([fwd_spec(n) for n in widths] + [bwd_spec(n) for n in widths]
                + [_const_spec(wup.shape), _const_spec(bg.shape)])
    args = [q, k, v, glr, q, k, v, glr, wup, bg]
    if not zero_init:
        in_specs.append(state_spec)
        args.append(s0)
    out_specs = [fwd_spec(dvv), bwd_spec(dvv)]
    out_shapes = [jax.ShapeDtypeStruct((b, t, dvv), F32)] * 2
    aliases = {}
    if emit_state:
        out_specs.append(state_spec)
        out_shapes.append(jax.ShapeDtypeStruct((b, depth, N_DIR, N_HEADS, dk, dv), F32))
        if states is not None:
            in_specs.append(pl.BlockSpec(memory_space=pl.ANY))
            args.append(states)
            aliases = {len(args) - 1: 2}
    return pl.pallas_call(
        functools.partial(_gla_kernel, n_chunks=tt // CHUNK, dk=dk, dv=dv, zero_init=zero_init,
                          emit_state=emit_state, aliased_states=bool(aliases)),
        grid=(b, n_t),
        in_specs=in_specs,
        out_specs=out_specs,
        out_shape=out_shapes,
        input_output_aliases=aliases,
        scratch_shapes=[pltpu.VMEM((N_DIR, N_HEADS, dk, dv), F32),
                        pltpu.VMEM((2, N_DIR, 3, CHUNK, dkk), BF16),
                        pltpu.VMEM((2, N_DIR, N_HEADS, dk, dk), F32)],
        compiler_params=pltpu.CompilerParams(
            dimension_semantics=("parallel", "arbitrary"), vmem_limit_bytes=VMEM_LIMIT),
        name="gla_scan",
    )(*args)


def _gelu_tanh(x):
    return 0.5 * x * (1.0 + jnp.tanh(math.sqrt(2.0 / math.pi) * (x + 0.044715 * (x * x * x))))


def _mixmlp_kernel(*refs, d, dv, ff_block, final_norm):
    (x_ref, of_ref, ob_ref, r_ref, ga_ref, gb_ref, u_ref, yf_ref, yb_ref, mod_ref, gn_ref, d_ref,
     wglu_ref, bglu_ref, wpg_ref, wps_ref, wout_ref, g2_ref, w1_ref, w2_ref) = refs[:20]
    if final_norm:
        fg_ref, o_ref = refs[20:]
    else:
        o_ref = refs[20]
    o = of_ref[...] + ob_ref[...]
    r = r_ref[...]
    gn = gn_ref[...]
    parts = []
    for h in range(N_HEADS):
        vs = slice(h * dv, (h + 1) * dv)
        oh = o[:, vs]
        ms = jnp.mean(jnp.square(oh), axis=-1, keepdims=True)
        parts.append((oh * lax.rsqrt(ms + EPS) * gn * _silu(r[:, vs])).astype(BF16))
    pg = _dot(jnp.concatenate(parts, axis=-1), wpg_ref[...])

    u = u_ref[...]
    y = _gelu_tanh(yf_ref[...] + yb_ref[...] + d_ref[...] * u)
    y = y * jax.nn.sigmoid(_dot(y.astype(BF16), wglu_ref[...]) + bglu_ref[...])
    ps = _dot(y.astype(BF16), wps_ref[...])

    merged = jax.nn.sigmoid(ga_ref[...]) * pg + jax.nn.sigmoid(gb_ref[...]) * ps
    gate = mod_ref[:, 2 * d:3 * d]
    x = x_ref[...] + gate * _dot(merged.astype(BF16), wout_ref[...])

    shift = mod_ref[:, 3 * d:4 * d]
    scale = mod_ref[:, 4 * d:5 * d]
    gate = mod_ref[:, 5 * d:6 * d]
    h = _rms_mod(x, g2_ref[...], scale, shift).astype(BF16)
    acc = jnp.zeros(x.shape, F32)
    for j in range(w1_ref.shape[1] // ff_block):
        cs = slice(j * ff_block, (j + 1) * ff_block)
        a = jnp.square(jnp.maximum(_dot(h, w1_ref[:, cs]), 0.0)).astype(BF16)
        acc = acc + _dot(a, w2_ref[cs, :])
    x = x + gate * acc
    if final_norm:
        ms = jnp.mean(jnp.square(x), axis=-1, keepdims=True)
        x = x * lax.rsqrt(ms + EPS) * fg_ref[...]
    o_ref[...] = x


def _mixmlp(x, o_f, o_b, r, ga, gb, u_tm, y, mod, p, final_g, tm):
    b, t, d = x.shape
    dvv = r.shape[-1]
    w = u_tm.shape[1] // b
    per_batch_mod = mod.shape[0] != 1
    final_norm = final_g is not None
    row_spec = lambda n: pl.BlockSpec((None, tm, n), lambda bi, i: (bi, i, 0))
    mod_idx = (lambda bi, i: (bi, 0, 0)) if per_batch_mod else (lambda bi, i: (0, 0, 0))
    weights = [p["w_glu"], p["b_glu"].reshape(1, -1), p["w_proj_gla"], p["w_proj_s5"], p["w_out"],
               p["norm2_g"].reshape(1, d), p["w_ff1"], p["w_ff2"]]
    if final_norm:
        weights.append(final_g.reshape(1, d))
    in_specs = [
        row_spec(d), row_spec(dvv), row_spec(dvv), row_spec(dvv), row_spec(d), row_spec(d),
        pl.BlockSpec((tm, w), lambda bi, i: (i, bi)),
        pl.BlockSpec((None, tm, w), lambda bi, i: (0, i, bi)),
        pl.BlockSpec((None, tm, w), lambda bi, i: (1, i, bi)),
        pl.BlockSpec((None, 1, mod.shape[2]), mod_idx),
        _const_spec((1, dvv // N_HEADS)), _const_spec((1, w)),
    ] + [_const_spec(a.shape) for a in weights]
    return pl.pallas_call(
        functools.partial(_mixmlp_kernel, d=d, dv=dvv // N_HEADS, ff_block=1024,
                          final_norm=final_norm),
        grid=(b, t // tm),
        in_specs=in_specs,
        out_specs=row_spec(d),
        out_shape=jax.ShapeDtypeStruct((b, t, d), F32),
        compiler_params=pltpu.CompilerParams(
            dimension_semantics=("parallel", "parallel"), vmem_limit_bytes=VMEM_LIMIT),
        name="mixmlp",
    )(x, o_f, o_b, r, ga, gb, u_tm, y, y, mod, p["gla_norm_g"].reshape(1, -1),
      p["s5_d"].reshape(1, -1), *weights)


def _grid_pos_embed(n_tokens, dim):
    rows = n_tokens // GRID_W
    r = jnp.repeat(jnp.arange(rows, dtype=F32), GRID_W)
    col = jnp.tile(jnp.arange(GRID_W, dtype=F32), rows)
    quarter = dim // 4
    omega = 1.0 / (POS_BASE ** (jnp.arange(quarter, dtype=F32) / quarter))
    ar = r[:, None] * omega
    ac = col[:, None] * omega
    return jnp.concatenate([jnp.sin(ar), jnp.cos(ar), jnp.sin(ac), jnp.cos(ac)], axis=-1)


def _split_w_in(w_in, dk_all, dv_all, s5w, d):
    splits = (dk_all, dk_all, dv_all, dv_all, N_DIR * GATE_RANK, s5w, d, d)
    idx = np.cumsum((0,) + splits)
    parts = [w_in[:, idx[j]:idx[j + 1]] for j in range(len(splits))]
    parts[4] = jnp.pad(parts[4], ((0, 0), (0, LANES - splits[4])))
    return [p.astype(BF16) for p in parts]


def _stream(x, pos, mods, gla_s0, s5_s0, layers, s5_prm, final_g, emit_state, tm):
    b, t, d = x.shape
    gla_states, s5_re, s5_im = None, [], []
    for l, p in enumerate(layers):
        res = _inproj(x, pos if l == 0 else None, mods[l], p["norm1_g"], p["w_in"], tm)
        if l == 0 and pos is not None:
            x = res[0]
            res = res[1:]
        q, k, v, r, glr, u_tm, ga, gb = res
        s5_out = _s5_scan(u_tm, s5_prm, l, None if s5_s0 is None else s5_s0[l], b, emit_state,
                          tt=min(t, 128))
        gla_out = _gla_scan(q, k, v, glr, p["wup"], p["bg"], l, len(layers), gla_s0, gla_states,
                            emit_state, tt=min(t, 256))
        o_f, o_b = gla_out[:2]
        if emit_state:
            y, f_re, f_im = s5_out
            gla_states = gla_out[2]
            s5_re.append(f_re)
            s5_im.append(f_im)
        else:
            y = s5_out
        last = l == len(layers) - 1
        x = _mixmlp(x, o_f, o_b, r, ga, gb, u_tm, y, mods[l], p, final_g if last else None, tm)
    return x, gla_states, s5_re, s5_im


def kernel(x_prompt, x_sample, c, cache_gla_state, state_s5_re, state_s5_im, c_ctx, w_mod, b_mod,
           norm1_g, w_in, w_gate_up, b_gate, gla_norm_g, w_proj_gla, s5_lam_re, s5_lam_im,
           s5_log_step, s5_b_re, s5_b_im, s5_c_re, s5_c_im, s5_d, w_glu, b_glu, w_proj_s5, w_out,
           norm2_g, w_ff1, w_ff2, final_g):
    depth = w_in.shape[0]
    nb, seq, d = x_prompt.shape
    db, dseq, _ = x_sample.shape
    dk_all = w_gate_up.shape[-1]
    dv_all = w_proj_gla.shape[1]
    s5w = s5_d.shape[-1]
    n_groups, n_state = s5_lam_re.shape[2], s5_lam_re.shape[3]
    dk, dv = dk_all // N_HEADS, dv_all // N_HEADS
    assert nb % SUBLANES == 0 and db % SUBLANES == 0

    n_cond = -(-(db + 1) // SUBLANES) * SUBLANES
    conds = jnp.concatenate([c, c_ctx[None], jnp.zeros((n_cond - db - 1, d), F32)], axis=0)
    mod_all = _modulation(conds, w_mod, b_mod)
    mods_lat = [mod_all[l, :db].reshape(db, 1, N_MOD * d) for l in range(depth)]
    mods_ctx = [mod_all[l, db:db + 1].reshape(1, 1, N_MOD * d) for l in range(depth)]

    s5_prm = _s5_params(s5_lam_re, s5_lam_im, s5_log_step, s5_b_re, s5_b_im, s5_c_re, s5_c_im)
    wup = jnp.stack([jnp.pad(w_gate_up[:, dd], ((0, 0), (dd * GATE_RANK, LANES - (dd + 1) * GATE_RANK),
                                                (0, 0))) for dd in range(N_DIR)], axis=1).astype(BF16)
    layers = []
    for l in range(depth):
        layers.append(dict(
            norm1_g=norm1_g[l], w_in=_split_w_in(w_in[l], dk_all, dv_all, s5w, d),
            wup=wup[l], bg=b_gate[l].reshape(N_DIR, 1, dk_all),
            gla_norm_g=gla_norm_g[l], w_proj_gla=w_proj_gla[l].astype(BF16),
            s5_d=s5_d[l], w_glu=w_glu[l].astype(BF16), b_glu=b_glu[l],
            w_proj_s5=w_proj_s5[l].astype(BF16), w_out=w_out[l].astype(BF16),
            norm2_g=norm2_g[l], w_ff1=w_ff1[l].astype(BF16), w_ff2=w_ff2[l].astype(BF16)))

    y_prompt, new_gla_state, s5_re, s5_im = _stream(
        x_prompt, None, mods_ctx, None, None, layers, s5_prm, final_g, True, tm=min(seq, 256))
    to_state = lambda a: a.transpose(1, 0, 2).reshape(nb, N_DIR, n_groups, n_state)
    new_s5_re = jnp.stack([to_state(a) for a in s5_re], axis=1)
    new_s5_im = jnp.stack([to_state(a) for a in s5_im], axis=1)

    pos = _grid_pos_embed(dseq, d)
    from_state = lambda a: a.reshape(db, N_DIR, n_groups * n_state).transpose(1, 0, 2)
    s5_s0 = [(from_state(state_s5_re[:, l]), from_state(state_s5_im[:, l])) for l in range(depth)]
    y_sample, _, _, _ = _stream(
        x_sample, pos, mods_lat, cache_gla_state, s5_s0, layers, s5_prm, final_g, False,
        tm=min(dseq, 256))

    return (y_prompt, y_sample, new_gla_state, new_s5_re, new_s5_im)
```

```python
import functools
import math

import jax
import jax.numpy as jnp
import numpy as np
from jax import lax
from jax.experimental import pallas as pl
from jax.experimental.pallas import tpu as pltpu

N_DIR = 2
N_HEADS = 4
GATE_RANK = 16
GATE_TAU = 16.0
CHUNK = 64
S5_GROUP = 16
S5_STATE = 64
N_MOD = 6
EPS = 1e-6
GRID_W = 64
POS_BASE = 10000.0

LANES = 128
SUBLANES = 8
VMEM_LIMIT = 56 * 1024 * 1024

F32 = jnp.float32
BF16 = jnp.bfloat16


def _const_spec(shape):
    nd = len(shape)
    return pl.BlockSpec(shape, lambda *_: (0,) * nd, pipeline_mode=pl.Buffered(1))


def _dot(a, b):
    return jnp.dot(a, b, preferred_element_type=F32)


def _rms_mod(x, g, scale, shift):
    ms = jnp.mean(jnp.square(x), axis=-1, keepdims=True)
    return x * lax.rsqrt(ms + EPS) * g * (1.0 + scale) + shift


def _silu(x):
    return x * jax.nn.sigmoid(x)


def _mod_kernel(c_ref, w_ref, b_ref, o_ref):
    a = _silu(c_ref[...]).astype(BF16)
    o_ref[...] = _dot(a, w_ref[...].astype(BF16)) + b_ref[...]


def _modulation(conds, w_mod, b_mod):
    depth, d, dm = w_mod.shape
    rows = conds.shape[0]
    tn = 1024
    return pl.pallas_call(
        _mod_kernel,
        grid=(depth, dm // tn),
        in_specs=[
            pl.BlockSpec((rows, d), lambda l, j: (0, 0)),
            pl.BlockSpec((None, d, tn), lambda l, j: (l, 0, j)),
            pl.BlockSpec((None, 1, tn), lambda l, j: (l, 0, j)),
        ],
        out_specs=pl.BlockSpec((None, rows, tn), lambda l, j: (l, 0, j)),
        out_shape=jax.ShapeDtypeStruct((depth, rows, dm), F32),
        compiler_params=pltpu.CompilerParams(vmem_limit_bytes=VMEM_LIMIT),
        name="modulation",
    )(conds, w_mod, b_mod.reshape(depth, 1, dm))


def _inproj_kernel(*refs, d, add_pos, n_sub):
    if add_pos:
        x_ref, pos_ref, mod_ref, g_ref = refs[:4]
        rest = refs[4:]
    else:
        x_ref, mod_ref, g_ref = refs[:3]
        rest = refs[3:]
    w_refs = rest[:8]
    out_refs = rest[8:]
    shift = mod_ref[:, 0:d]
    scale = mod_ref[:, d:2 * d]
    sub = x_ref.shape[0] // n_sub
    for si in range(n_sub):
        rows = slice(si * sub, (si + 1) * sub)
        x = x_ref[rows, :]
        proj_refs = out_refs
        if add_pos:
            x = x + pos_ref[rows, :]
            out_refs[0][rows, :] = x
            proj_refs = out_refs[1:]
        h = _rms_mod(x, g_ref[...], scale, shift).astype(BF16)
        for w_ref, o_ref in zip(w_refs, proj_refs):
            o_ref[rows, :] = _dot(h, w_ref[...])


def _inproj(x, pos, mod, g, weights, tm):
    b, t, d = x.shape
    add_pos = pos is not None
    per_batch_mod = mod.shape[0] != 1
    widths = [w.shape[1] for w in weights]
    row_spec = lambda n: pl.BlockSpec((None, tm, n), lambda bi, i: (bi, i, 0))
    in_specs = [row_spec(d)]
    args = [x]
    if add_pos:
        in_specs.append(pl.BlockSpec((tm, d), lambda bi, i: (i, 0)))
        args.append(pos)
    mod_idx = (lambda bi, i: (bi, 0, 0)) if per_batch_mod else (lambda bi, i: (0, 0, 0))
    in_specs += [pl.BlockSpec((None, 1, mod.shape[2]), mod_idx), _const_spec((1, d))]
    args += [mod, g.reshape(1, d)]
    in_specs += [_const_spec(w.shape) for w in weights]
    args += list(weights)

    out_specs, out_shapes = [], []
    if add_pos:
        out_specs.append(row_spec(d))
        out_shapes.append(jax.ShapeDtypeStruct((b, t, d), F32))
    for idx, n in enumerate(widths):
        if idx == 5:
            out_specs.append(pl.BlockSpec((tm, n), lambda bi, i: (i, bi)))
            out_shapes.append(jax.ShapeDtypeStruct((t, b * n), F32))
        else:
            out_specs.append(row_spec(n))
            out_shapes.append(jax.ShapeDtypeStruct((b, t, n), F32))
    return pl.pallas_call(
        functools.partial(_inproj_kernel, d=d, add_pos=add_pos, n_sub=2),
        grid=(b, t // tm),
        in_specs=in_specs,
        out_specs=out_specs,
        out_shape=out_shapes,
        compiler_params=pltpu.CompilerParams(
            dimension_semantics=("parallel", "parallel"), vmem_limit_bytes=VMEM_LIMIT),
        name="inproj",
    )(*args)


S5_BLOCK = 4


def _s5_kernel(*refs, tt, w, n_p, zero_init, emit_state):
    u_ref, l4r_ref, l4i_ref, wzr_ref, wzi_ref, wor_ref, woi_ref, wt_ref = refs[:8]
    rest = refs[8:]
    if not zero_init:
        s0r_ref, s0i_ref = rest[:2]
        rest = rest[2:]
    if emit_state:
        y_ref, fr_ref, fi_ref = rest[:3]
        rest = rest[3:]
    else:
        y_ref = rest[0]
        rest = rest[1:]
    usc, xsc, zre, zim, ypk, car, cai = rest
    direction = pl.program_id(1)
    i = pl.program_id(2)
    n_slab = w // LANES
    n_blk = tt // S5_BLOCK
    half = LANES // 2
    tw = 2 * LANES
    grp = S5_BLOCK * SUBLANES

    @pl.when(i == 0)
    def _():
        if zero_init:
            car[...] = jnp.zeros_like(car)
            cai[...] = jnp.zeros_like(cai)
        else:
            car[...] = s0r_ref[...]
            cai[...] = s0i_ref[...]

    for b in range(SUBLANES):
        for s in range(n_slab):
            usc[s, pl.ds(b, tt, stride=SUBLANES), :] = u_ref[:, b * w + s * LANES:b * w + (s + 1) * LANES]

    low = lax.broadcasted_iota(jnp.int32, (2 * SUBLANES, LANES), 1) < half
    swap = lambda a: pltpu.roll(a, half, 1)

    def pack(s):
        for r2 in range(n_blk // 2):
            a = [jnp.concatenate([usc[s, pl.ds(base + j * SUBLANES, SUBLANES), :]
                                  for base in (2 * r2 * grp, (2 * r2 + 1) * grp)], axis=0)
                 for j in range(S5_BLOCK)]
            asw = [swap(v) for v in a]
            for h in range(2):
                pick = lambda j, slot_half: a[j] if h == slot_half else asw[j]
                cols = [jnp.where(low, pick(2 * c, 0), pick(2 * c + 1, 1)) for c in range(2)]
                xsc[2 * s + h, pl.ds(r2 * 2 * SUBLANES, 2 * SUBLANES), :] = (
                    jnp.concatenate(cols, axis=1).astype(BF16))

    def block_inputs(p):
        x = xsc[p]
        zre[:, p * tw:(p + 1) * tw] = _dot(x, wzr_ref[p])
        zim[:, p * tw:(p + 1) * tw] = _dot(x, wzi_ref[p])

    def scan(s):
        ls = slice(s * 2 * tw, (s + 1) * 2 * tw)
        lr = l4r_ref[:, ls]
        li = l4i_ref[:, ls]
        cr, ci = car[:, ls], cai[:, ls]
        for j in range(n_blk):
            r = j + direction * (n_blk - 1 - 2 * j)
            r0 = pl.multiple_of(r * SUBLANES, SUBLANES)
            zr = zre[pl.ds(r0, SUBLANES), ls]
            zi = zim[pl.ds(r0, SUBLANES), ls]
            zre[pl.ds(r0, SUBLANES), ls] = cr
            zim[pl.ds(r0, SUBLANES), ls] = ci
            cr, ci = lr * cr - li * ci + zr, lr * ci + li * cr + zi
        car[:, ls] = cr
        cai[:, ls] = ci

    def block_outputs(p):
        ypk[p] = (_dot(zre[:, p * tw:(p + 1) * tw].astype(BF16), wor_ref[p])
                  + _dot(zim[:, p * tw:(p + 1) * tw].astype(BF16), woi_ref[p])
                  + _dot(xsc[p], wt_ref[p]))

    def unpack(s):
        for r2 in range(n_blk // 2):
            rows = pl.ds(r2 * 2 * SUBLANES, 2 * SUBLANES)
            for j in range(S5_BLOCK):
                c, slot_half = divmod(j, 2)
                left = ypk[2 * s, rows, c * LANES:(c + 1) * LANES]
                right = ypk[2 * s + 1, rows, c * LANES:(c + 1) * LANES]
                v = jnp.where(low, left if slot_half == 0 else swap(left),
                              right if slot_half == 1 else swap(right))
                usc[s, pl.ds(2 * r2 * grp + j * SUBLANES, SUBLANES), :] = v[:SUBLANES]
                usc[s, pl.ds((2 * r2 + 1) * grp + j * SUBLANES, SUBLANES), :] = v[SUBLANES:]

    for s in range(n_slab + 1):
        if s < n_slab:
            pack(s)
            block_inputs(2 * s)
            block_inputs(2 * s + 1)
        if s >= 1:
            block_outputs(2 * s - 2)
            block_outputs(2 * s - 1)
            unpack(s - 1)
        if s < n_slab:
            scan(s)

    for b in range(SUBLANES):
        for s in range(n_slab):
            y_ref[:, b * w + s * LANES:b * w + (s + 1) * LANES] = usc[s, pl.ds(b, tt, stride=SUBLANES), :]

    if emit_state:
        @pl.when(i == pl.num_programs(2) - 1)
        def _():
            fr_ref[...] = car[...]
            fi_ref[...] = cai[...]


def _s5_scan(u_tm, prm, layer, s0, b, emit_state, tt):
    t = u_tm.shape[0]
    w = u_tm.shape[1] // b
    l4r, l4i = prm[:2]
    mats = prm[2:]
    n_p = mats[0].shape[2]
    ns = l4r.shape[-1]
    n_t = t // tt
    n_g = b // SUBLANES
    rows = tt // S5_BLOCK * SUBLANES
    zero_init = s0 is None
    tile = lambda g, d, i: i + d * (n_t - 1 - 2 * i)
    dir_spec = lambda shape: pl.BlockSpec((None, None) + shape,
                                          lambda g, d, i: (layer, d) + (0,) * len(shape))
    in_specs = [pl.BlockSpec((tt, SUBLANES * w), lambda g, d, i: (tile(g, d, i), g)),
                dir_spec((SUBLANES, ns)), dir_spec((SUBLANES, ns))]
    in_specs += [dir_spec(m.shape[2:]) for m in mats]
    args = [u_tm, l4r, l4i] + list(mats)
    state_spec = pl.BlockSpec((None, SUBLANES, ns), lambda g, d, i: (d, g, 0))
    if not zero_init:
        in_specs += [state_spec, state_spec]
        args += list(s0)
    out_specs = [pl.BlockSpec((None, tt, SUBLANES * w), lambda g, d, i: (d, tile(g, d, i), g))]
    out_shapes = [jax.ShapeDtypeStruct((N_DIR, t, b * w), F32)]
    if emit_state:
        out_specs += [state_spec, state_spec]
        out_shapes += [jax.ShapeDtypeStruct((N_DIR, b, ns), F32)] * 2
    res = pl.pallas_call(
        functools.partial(_s5_kernel, tt=tt, w=w, n_p=n_p, zero_init=zero_init,
                          emit_state=emit_state),
        grid=(n_g, N_DIR, n_t),
        in_specs=in_specs,
        out_specs=out_specs,
        out_shape=out_shapes,
        scratch_shapes=[pltpu.VMEM((w // LANES, tt * SUBLANES, LANES), F32),
                        pltpu.VMEM((n_p, rows, 2 * LANES), BF16),
                        pltpu.VMEM((rows, ns), F32), pltpu.VMEM((rows, ns), F32),
                        pltpu.VMEM((n_p, rows, 2 * LANES), F32),
                        pltpu.VMEM((SUBLANES, ns), F32), pltpu.VMEM((SUBLANES, ns), F32)],
        compiler_params=pltpu.CompilerParams(
            dimension_semantics=("parallel", "parallel", "arbitrary"), vmem_limit_bytes=VMEM_LIMIT),
        name="s5_scan",
    )(*args)
    if emit_state:
        return res[0], res[1], res[2]
    return res[0]


def _s5_params(lam_re, lam_im, log_step, b_re, b_im, c_re, c_im):
    depth, n_dir, g, p = lam_re.shape
    n = b_re.shape[-1]
    blk = S5_BLOCK
    gpt = 2 * LANES // p
    n_p = g // gpt
    tw = 2 * LANES
    assert gpt * n * blk == tw and n_dir == N_DIR
    hi = lax.Precision.HIGHEST
    step = jnp.exp(log_step)[..., None]
    a = lam_re * step
    th = lam_im * step
    k = jnp.arange(blk + 1, dtype=F32).reshape(-1, 1, 1, 1, 1)
    pw_re = jnp.exp(k * a) * jnp.cos(k * th)
    pw_im = jnp.exp(k * a) * jnp.sin(k * th)
    den = lam_re * lam_re + lam_im * lam_im
    fr = ((pw_re[1] - 1.0) * lam_re + pw_im[1] * lam_im) / den
    fi = (pw_im[1] * lam_re - (pw_re[1] - 1.0) * lam_im) / den
    bbr = fr[..., None] * b_re - fi[..., None] * b_im
    bbi = fr[..., None] * b_im + fi[..., None] * b_re
    lb_re = pw_re[..., None] * bbr - pw_im[..., None] * bbi
    lb_im = pw_re[..., None] * bbi + pw_im[..., None] * bbr
    eye = jnp.eye(gpt, dtype=BF16)
    offs = np.arange(blk)
    pos = np.stack([offs, blk - 1 - offs])

    def per_dir(arr, idx):
        return jnp.stack([arr[idx[d], :, d] for d in range(n_dir)], axis=idx.ndim)

    split = lambda m, ax: m.reshape(m.shape[:ax] + (n_p, gpt) + m.shape[ax + 1:])
    wz = [jnp.einsum("jldpgsn,gh->ldpjgnhs", split(per_dir(m, blk - 1 - pos), 3).astype(BF16), eye)
          .reshape(depth, n_dir, n_p, tw, tw) for m in (lb_re, lb_im)]
    pr = per_dir(pw_re, pos + 1)[:, :, :, :, None, :]
    pi = per_dir(pw_im, pos + 1)[:, :, :, :, None, :]
    wo = [jnp.einsum("jldpgns,gh->ldpgsjhn", split(m, 3).astype(BF16), eye).reshape(depth, n_dir, n_p, tw, tw)
          for m in (c_re * pr - c_im * pi, -(c_re * pi + c_im * pr))]
    kern = (jnp.einsum("ldgnp,mldgpq->mldgqn", c_re, lb_re[:blk], precision=hi)
            - jnp.einsum("ldgnp,mldgpq->mldgqn", c_im, lb_im[:blk], precision=hi))
    lag = pos[:, None, :] - pos[:, :, None]
    valid = jnp.asarray((lag >= 0).transpose(1, 2, 0), F32).reshape(blk, blk, 1, n_dir, 1, 1, 1)
    wt = jnp.einsum("ijldpgqn,gh->ldpigqjhn",
                    split(per_dir(kern, np.maximum(lag, 0)) * valid, 4).astype(BF16),
                    eye).reshape(depth, n_dir, n_p, tw, tw)
    bcast = lambda v: jnp.broadcast_to(v.reshape(depth, n_dir, 1, g * p),
                                       (depth, n_dir, SUBLANES, g * p))
    return [bcast(pw_re[blk]), bcast(pw_im[blk])] + wz + wo + [wt]


def _log_sigmoid(z):
    return -(jnp.maximum(-z, 0.0) + jnp.log(1.0 + jnp.exp(-jnp.abs(z))))


def _split2(x):
    hi = x.astype(BF16)
    return hi, (x - hi.astype(F32)).astype(BF16)


def _gla_kernel(*refs, n_chunks, dk, dv, zero_init, emit_state, aliased_states):
    dir_refs = (refs[0:4], refs[4:8])
    wup_ref, bg_ref = refs[8:10]
    rest = refs[10:]
    if not zero_init:
        s0_ref = rest[0]
        rest = rest[1:]
    if aliased_states:
        rest = rest[1:]
    if emit_state:
        of_ref, ob_ref, sf_ref, st, prep, dec = rest
    else:
        of_ref, ob_ref, st, prep, dec = rest
    o_refs = (of_ref, ob_ref)
    i = pl.program_id(1)
    scale = dk ** -0.5

    @pl.when(i == 0)
    def _():
        for d in range(N_DIR):
            for h in range(N_HEADS):
                if zero_init:
                    st[d, h] = jnp.zeros((dk, dv), F32)
                else:
                    st[d, h] = s0_ref[d, h]

    row = lax.broadcasted_iota(jnp.int32, (CHUNK, CHUNK), 0)
    col = lax.broadcasted_iota(jnp.int32, (CHUNK, CHUNK), 1)
    causal = (col <= row, col >= row)
    ones = tuple(jnp.where(c, 1.0, 0.0).astype(BF16) for c in causal)

    dims = (((1,), (1,)), ((), ()))
    chunk_of = lambda d, cc: cc if d == 0 else n_chunks - 1 - cc
    rows_of = lambda d, cc: pl.ds(chunk_of(d, cc) * CHUNK, CHUNK)

    def gate_z(d, cc):
        glr_ref = dir_refs[d][3]
        return _dot(glr_ref[rows_of(d, cc), :].astype(BF16), wup_ref[d]) + bg_ref[d]

    def gate_cum(d, g):
        return sum(_dot(ones[d], part) for part in _split2(g))

    def gate_store(slot, d, cc, g, cum):
        q_ref, k_ref = dir_refs[d][:2]
        rows = rows_of(d, cc)
        total = jnp.sum(g, axis=0, keepdims=True)
        prep[slot, d, 0] = (q_ref[rows, :] * scale * jnp.exp(cum)).astype(BF16)
        kk = k_ref[rows, :]
        prep[slot, d, 1] = (kk * jnp.exp(-cum)).astype(BF16)
        prep[slot, d, 2] = (kk * jnp.exp(total - cum)).astype(BF16)
        decay = jnp.exp(total)
        for h in range(N_HEADS):
            dec[slot, d, h] = jnp.broadcast_to(decay[:, ksl(h)], (dk, dk)).T

    pairs = [(d, h) for d in range(N_DIR) for h in range(N_HEADS)]
    ksl = lambda h: slice(h * dk, (h + 1) * dk)
    vsl = lambda h: slice(h * dv, (h + 1) * dv)

    for d in range(N_DIR):
        g0 = _log_sigmoid(gate_z(d, 0)) / GATE_TAU
        gate_store(0, d, 0, g0, gate_cum(d, g0))

    for cc in range(n_chunks):
        slot = cc % 2
        more = cc + 1 < n_chunks
        if more:
            z_next = [gate_z(d, cc + 1) for d in range(N_DIR)]
        sc = {}
        for d, h in pairs:
            s = lax.dot_general(prep[slot, d, 0, :, ksl(h)], prep[slot, d, 1, :, ksl(h)], dims,
                                preferred_element_type=F32)
            sc[d, h] = jnp.where(causal[d], s, 0.0).astype(BF16)
        if more:
            g_next = [_log_sigmoid(z) / GATE_TAU for z in z_next]
        for d, h in pairs:
            v_ref = dir_refs[d][2]
            rows = rows_of(d, cc)
            vh = v_ref[rows, vsl(h)].astype(BF16)
            o = _dot(sc[d, h], vh) + _dot(prep[slot, d, 0, :, ksl(h)], st[d, h].astype(BF16))
            o_refs[d][rows, vsl(h)] = o
        if more:
            cum_next = [gate_cum(d, g_next[d]) for d in range(N_DIR)]
        for d, h in pairs:
            v_ref = dir_refs[d][2]
            vh = v_ref[rows_of(d, cc), vsl(h)].astype(BF16)
            upd = lax.dot_general(prep[slot, d, 2, :, ksl(h)], vh, (((0,), (0,)), ((), ())),
                                  preferred_element_type=F32)
            decay = jnp.concatenate([dec[slot, d, h]] * (dv // dk), axis=1)
            st[d, h] = decay * st[d, h] + upd
        if more:
            for d in range(N_DIR):
                gate_store(1 - slot, d, cc + 1, g_next[d], cum_next[d])

    if emit_state:
        @pl.when(i == pl.num_programs(1) - 1)
        def _():
            for d in range(N_DIR):
                for h in range(N_HEADS):
                    sf_ref[d, h] = st[d, h]


def _gla_scan(q, k, v, glr, wup, bg, layer, depth, s0, states, emit_state, tt):
    b, t, dkk = q.shape
    dvv = v.shape[-1]
    dk, dv = dkk // N_HEADS, dvv // N_HEADS
    n_t = t // tt
    zero_init = s0 is None
    fwd_spec = lambda n: pl.BlockSpec((None, tt, n), lambda bi, i: (bi, i, 0))
    bwd_spec = lambda n: pl.BlockSpec((None, tt, n), lambda bi, i: (bi, n_t - 1 - i, 0))
    state_spec = pl.BlockSpec((None, None, N_DIR, N_HEADS, dk, dv), lambda bi, i: (bi, layer, 0, 0, 0, 0))
    widths = (dkk, dkk, dvv, glr.shape[-1])
    in_specs = ([fwd_spec(n) for n in widths] + [bwd_spec(n) for n in widths]
                + [_const_spec(wup.shape), _const_spec(bg.shape)])
    args = [q, k, v, glr, q, k, v, glr, wup, bg]
    if not zero_init:
        in_specs.append(state_spec)
        args.append(s0)
    out_specs = [fwd_spec(dvv), bwd_spec(dvv)]
    out_shapes = [jax.ShapeDtypeStruct((b, t, dvv), F32)] * 2
    aliases = {}
    if emit_state:
        out_specs.append(state_spec)
        out_shapes.append(jax.ShapeDtypeStruct((b, depth, N_DIR, N_HEADS, dk, dv), F32))
        if states is not None:
            in_specs.append(pl.BlockSpec(memory_space=pl.ANY))
            args.append(states)
            aliases = {len(args) - 1: 2}
    return pl.pallas_call(
        functools.partial(_gla_kernel, n_chunks=tt // CHUNK, dk=dk, dv=dv, zero_init=zero_init,
                          emit_state=emit_state, aliased_states=bool(aliases)),
        grid=(b, n_t),
        in_specs=in_specs,
        out_specs=out_specs,
        out_shape=out_shapes,
        input_output_aliases=aliases,
        scratch_shapes=[pltpu.VMEM((N_DIR, N_HEADS, dk, dv), F32),
                        pltpu.VMEM((2, N_DIR, 3, CHUNK, dkk), BF16),
                        pltpu.VMEM((2, N_DIR, N_HEADS, dk, dk), F32)],
        compiler_params=pltpu.CompilerParams(
            dimension_semantics=("parallel", "arbitrary"), vmem_limit_bytes=VMEM_LIMIT),
        name="gla_scan",
    )(*args)


def _gelu_tanh(x):
    return 0.5 * x * (1.0 + jnp.tanh(math.sqrt(2.0 / math.pi) * (x + 0.044715 * (x * x * x))))


def _mixmlp_kernel(*refs, d, dv, ff_block, final_norm, n_sub):
    (x_ref, of_ref, ob_ref, r_ref, ga_ref, gb_ref, u_ref, yf_ref, yb_ref, mod_ref, gn_ref, d_ref,
     wglu_ref, bglu_ref, wpg_ref, wps_ref, wout_ref, g2_ref, w1_ref, w2_ref) = refs[:20]
    if final_norm:
        fg_ref, o_ref = refs[20:]
    else:
        o_ref = refs[20]
    sub = x_ref.shape[0] // n_sub
    for si in range(n_sub):
        rows = slice(si * sub, (si + 1) * sub)
        o = of_ref[rows, :] + ob_ref[rows, :]
        r = r_ref[rows, :]
        gn = gn_ref[...]
        parts = []
        for h in range(N_HEADS):
            vs = slice(h * dv, (h + 1) * dv)
            oh = o[:, vs]
            ms = jnp.mean(jnp.square(oh), axis=-1, keepdims=True)
            parts.append((oh * lax.rsqrt(ms + EPS) * gn * _silu(r[:, vs])).astype(BF16))
        pg = _dot(jnp.concatenate(parts, axis=-1), wpg_ref[...])

        u = u_ref[rows, :]
        y = _gelu_tanh(yf_ref[rows, :] + yb_ref[rows, :] + d_ref[...] * u)
        y = y * jax.nn.sigmoid(_dot(y.astype(BF16), wglu_ref[...]) + bglu_ref[...])
        ps = _dot(y.astype(BF16), wps_ref[...])

        merged = jax.nn.sigmoid(ga_ref[rows, :]) * pg + jax.nn.sigmoid(gb_ref[rows, :]) * ps
        gate = mod_ref[:, 2 * d:3 * d]
        x = x_ref[rows, :] + gate * _dot(merged.astype(BF16), wout_ref[...])

        shift = mod_ref[:, 3 * d:4 * d]
        scale = mod_ref[:, 4 * d:5 * d]
        gate = mod_ref[:, 5 * d:6 * d]
        h = _rms_mod(x, g2_ref[...], scale, shift).astype(BF16)
        acc = jnp.zeros(x.shape, F32)
        for j in range(w1_ref.shape[1] // ff_block):
            cs = slice(j * ff_block, (j + 1) * ff_block)
            a = jnp.square(jnp.maximum(_dot(h, w1_ref[:, cs]), 0.0)).astype(BF16)
            acc = acc + _dot(a, w2_ref[cs, :])
        x = x + gate * acc
        if final_norm:
            ms = jnp.mean(jnp.square(x), axis=-1, keepdims=True)
            x = x * lax.rsqrt(ms + EPS) * fg_ref[...]
        o_ref[rows, :] = x


def _mixmlp(x, o_f, o_b, r, ga, gb, u_tm, y, mod, p, final_g, tm):
    b, t, d = x.shape
    dvv = r.shape[-1]
    w = u_tm.shape[1] // b
    per_batch_mod = mod.shape[0] != 1
    final_norm = final_g is not None
    row_spec = lambda n: pl.BlockSpec((None, tm, n), lambda bi, i: (bi, i, 0))
    mod_idx = (lambda bi, i: (bi, 0, 0)) if per_batch_mod else (lambda bi, i: (0, 0, 0))
    weights = [p["w_glu"], p["b_glu"].reshape(1, -1), p["w_proj_gla"], p["w_proj_s5"], p["w_out"],
               p["norm2_g"].reshape(1, d), p["w_ff1"], p["w_ff2"]]
    if final_norm:
        weights.append(final_g.reshape(1, d))
    in_specs = [
        row_spec(d), row_spec(dvv), row_spec(dvv), row_spec(dvv), row_spec(d), row_spec(d),
        pl.BlockSpec((tm, w), lambda bi, i: (i, bi)),
        pl.BlockSpec((None, tm, w), lambda bi, i: (0, i, bi)),
        pl.BlockSpec((None, tm, w), lambda bi, i: (1, i, bi)),
        pl.BlockSpec((None, 1, mod.shape[2]), mod_idx),
        _const_spec((1, dvv // N_HEADS)), _const_spec((1, w)),
    ] + [_const_spec(a.shape) for a in weights]
    return pl.pallas_call(
        functools.partial(_mixmlp_kernel, d=d, dv=dvv // N_HEADS, ff_block=1024,
                          final_norm=final_norm, n_sub=1),
        grid=(b, t // tm),
        in_specs=in_specs,
        out_specs=row_spec(d),
        out_shape=jax.ShapeDtypeStruct((b, t, d), F32),
        compiler_params=pltpu.CompilerParams(
            dimension_semantics=("parallel", "parallel"), vmem_limit_bytes=VMEM_LIMIT),
        name="mixmlp",
    )(x, o_f, o_b, r, ga, gb, u_tm, y, y, mod, p["gla_norm_g"].reshape(1, -1),
      p["s5_d"].reshape(1, -1), *weights)


def _grid_pos_embed(n_tokens, dim):
    rows = n_tokens // GRID_W
    r = jnp.repeat(jnp.arange(rows, dtype=F32), GRID_W)
    col = jnp.tile(jnp.arange(GRID_W, dtype=F32), rows)
    quarter = dim // 4
    omega = 1.0 / (POS_BASE ** (jnp.arange(quarter, dtype=F32) / quarter))
    ar = r[:, None] * omega
    ac = col[:, None] * omega
    return jnp.concatenate([jnp.sin(ar), jnp.cos(ar), jnp.sin(ac), jnp.cos(ac)], axis=-1)


def _split_w_in(w_in, dk_all, dv_all, s5w, d):
    splits = (dk_all, dk_all, dv_all, dv_all, N_DIR * GATE_RANK, s5w, d, d)
    idx = np.cumsum((0,) + splits)
    parts = [w_in[:, idx[j]:idx[j + 1]] for j in range(len(splits))]
    parts[4] = jnp.pad(parts[4], ((0, 0), (0, LANES - splits[4])))
    return [p.astype(BF16) for p in parts]


def _stream(x, pos, mods, gla_s0, s5_s0, layers, s5_prm, final_g, emit_state, tm):
    b, t, d = x.shape
    gla_states, s5_re, s5_im = None, [], []
    for l, p in enumerate(layers):
        res = _inproj(x, pos if l == 0 else None, mods[l], p["norm1_g"], p["w_in"], min(t, 2 * tm))
        if l == 0 and pos is not None:
            x = res[0]
            res = res[1:]
        q, k, v, r, glr, u_tm, ga, gb = res
        s5_out = _s5_scan(u_tm, s5_prm, l, None if s5_s0 is None else s5_s0[l], b, emit_state,
                          tt=min(t, 128))
        gla_out = _gla_scan(q, k, v, glr, p["wup"], p["bg"], l, len(layers), gla_s0, gla_states,
                            emit_state, tt=min(t, 512))
        o_f, o_b = gla_out[:2]
        if emit_state:
            y, f_re, f_im = s5_out
            gla_states = gla_out[2]
            s5_re.append(f_re)
            s5_im.append(f_im)
        else:
            y = s5_out
        last = l == len(layers) - 1
        x = _mixmlp(x, o_f, o_b, r, ga, gb, u_tm, y, mods[l], p, final_g if last else None, tm)
    return x, gla_states, s5_re, s5_im


def kernel(x_prompt, x_sample, c, cache_gla_state, state_s5_re, state_s5_im, c_ctx, w_mod, b_mod,
           norm1_g, w_in, w_gate_up, b_gate, gla_norm_g, w_proj_gla, s5_lam_re, s5_lam_im,
           s5_log_step, s5_b_re, s5_b_im, s5_c_re, s5_c_im, s5_d, w_glu, b_glu, w_proj_s5, w_out,
           norm2_g, w_ff1, w_ff2, final_g):
    depth = w_in.shape[0]
    nb, seq, d = x_prompt.shape
    db, dseq, _ = x_sample.shape
    dk_all = w_gate_up.shape[-1]
    dv_all = w_proj_gla.shape[1]
    s5w = s5_d.shape[-1]
    n_groups, n_state = s5_lam_re.shape[2], s5_lam_re.shape[3]
    dk, dv = dk_all // N_HEADS, dv_all // N_HEADS
    assert nb % SUBLANES == 0 and db % SUBLANES == 0

    n_cond = -(-(db + 1) // SUBLANES) * SUBLANES
    conds = jnp.concatenate([c, c_ctx[None], jnp.zeros((n_cond - db - 1, d), F32)], axis=0)
    mod_all = _modulation(conds, w_mod, b_mod)
    mods_lat = [mod_all[l, :db].reshape(db, 1, N_MOD * d) for l in range(depth)]
    mods_ctx = [mod_all[l, db:db + 1].reshape(1, 1, N_MOD * d) for l in range(depth)]

    s5_prm = _s5_params(s5_lam_re, s5_lam_im, s5_log_step, s5_b_re, s5_b_im, s5_c_re, s5_c_im)
    wup = jnp.stack([jnp.pad(w_gate_up[:, dd], ((0, 0), (dd * GATE_RANK, LANES - (dd + 1) * GATE_RANK),
                                                (0, 0))) for dd in range(N_DIR)], axis=1).astype(BF16)
    layers = []
    for l in range(depth):
        layers.append(dict(
            norm1_g=norm1_g[l], w_in=_split_w_in(w_in[l], dk_all, dv_all, s5w, d),
            wup=wup[l], bg=b_gate[l].reshape(N_DIR, 1, dk_all),
            gla_norm_g=gla_norm_g[l], w_proj_gla=w_proj_gla[l].astype(BF16),
            s5_d=s5_d[l], w_glu=w_glu[l].astype(BF16), b_glu=b_glu[l],
            w_proj_s5=w_proj_s5[l].astype(BF16), w_out=w_out[l].astype(BF16),
            norm2_g=norm2_g[l], w_ff1=w_ff1[l].astype(BF16), w_ff2=w_ff2[l].astype(BF16)))

    y_prompt, new_gla_state, s5_re, s5_im = _stream(
        x_prompt, None, mods_ctx, None, None, layers, s5_prm, final_g, True, tm=min(seq, 256))
    to_state = lambda a: a.transpose(1, 0, 2).reshape(nb, N_DIR, n_groups, n_state)
    new_s5_re = jnp.stack([to_state(a) for a in s5_re], axis=1)
    new_s5_im = jnp.stack([to_state(a) for a in s5_im], axis=1)

    pos = _grid_pos_embed(dseq, d)
    from_state = lambda a: a.reshape(db, N_DIR, n_groups * n_state).transpose(1, 0, 2)
    s5_s0 = [(from_state(state_s5_re[:, l]), from_state(state_s5_im[:, l])) for l in range(depth)]
    y_sample, _, _, _ = _stream(
        x_sample, pos, mods_lat, cache_gla_state, s5_s0, layers, s5_prm, final_g, False,
        tm=min(dseq, 256))

    return (y_prompt, y_sample, new_gla_state, new_s5_re, new_s5_im)
```

```python
import functools
import math

import jax
import jax.numpy as jnp
import numpy as np
from jax import lax
from jax.experimental import pallas as pl
from jax.experimental.pallas import tpu as pltpu

N_DIR = 2
N_HEADS = 4
GATE_RANK = 16
GATE_TAU = 16.0
CHUNK = 64
S5_GROUP = 16
S5_STATE = 64
N_MOD = 6
EPS = 1e-6
GRID_W = 64
POS_BASE = 10000.0

LANES = 128
SUBLANES = 8
VMEM_LIMIT = 56 * 1024 * 1024

F32 = jnp.float32
BF16 = jnp.bfloat16


def _const_spec(shape):
    nd = len(shape)
    return pl.BlockSpec(shape, lambda *_: (0,) * nd, pipeline_mode=pl.Buffered(1))


def _dot(a, b):
    return jnp.dot(a, b, preferred_element_type=F32)


def _rms_mod(x, g, scale, shift):
    ms = jnp.mean(jnp.square(x), axis=-1, keepdims=True)
    return x * lax.rsqrt(ms + EPS) * g * (1.0 + scale) + shift


def _silu(x):
    return x * jax.nn.sigmoid(x)


def _mod_kernel(c_ref, w_ref, b_ref, o_ref):
    a = _silu(c_ref[...]).astype(BF16)
    o_ref[...] = _dot(a, w_ref[...].astype(BF16)) + b_ref[...]


def _modulation(conds, w_mod, b_mod):
    depth, d, dm = w_mod.shape
    rows = conds.shape[0]
    tn = 1024
    return pl.pallas_call(
        _mod_kernel,
        grid=(depth, dm // tn),
        in_specs=[
            pl.BlockSpec((rows, d), lambda l, j: (0, 0)),
            pl.BlockSpec((None, d, tn), lambda l, j: (l, 0, j)),
            pl.BlockSpec((None, 1, tn), lambda l, j: (l, 0, j)),
        ],
        out_specs=pl.BlockSpec((None, rows, tn), lambda l, j: (l, 0, j)),
        out_shape=jax.ShapeDtypeStruct((depth, rows, dm), F32),
        compiler_params=pltpu.CompilerParams(vmem_limit_bytes=VMEM_LIMIT),
        name="modulation",
    )(conds, w_mod, b_mod.reshape(depth, 1, dm))


def _inproj_kernel(*refs, d, add_pos, n_sub):
    if add_pos:
        x_ref, pos_ref, mod_ref, g_ref = refs[:4]
        rest = refs[4:]
    else:
        x_ref, mod_ref, g_ref = refs[:3]
        rest = refs[3:]
    w_refs = rest[:8]
    out_refs = rest[8:]
    shift = mod_ref[:, 0:d]
    scale = mod_ref[:, d:2 * d]
    sub = x_ref.shape[0] // n_sub
    for si in range(n_sub):
        rows = slice(si * sub, (si + 1) * sub)
        x = x_ref[rows, :]
        proj_refs = out_refs
        if add_pos:
            x = x + pos_ref[rows, :]
            out_refs[0][rows, :] = x
            proj_refs = out_refs[1:]
        h = _rms_mod(x, g_ref[...], scale, shift).astype(BF16)
        for w_ref, o_ref in zip(w_refs, proj_refs):
            o_ref[rows, :] = _dot(h, w_ref[...]).astype(o_ref.dtype)


def _inproj(x, pos, mod, g, weights, tm):
    b, t, d = x.shape
    add_pos = pos is not None
    per_batch_mod = mod.shape[0] != 1
    widths = [w.shape[1] for w in weights]
    row_spec = lambda n: pl.BlockSpec((None, tm, n), lambda bi, i: (bi, i, 0))
    in_specs = [row_spec(d)]
    args = [x]
    if add_pos:
        in_specs.append(pl.BlockSpec((tm, d), lambda bi, i: (i, 0)))
        args.append(pos)
    mod_idx = (lambda bi, i: (bi, 0, 0)) if per_batch_mod else (lambda bi, i: (0, 0, 0))
    in_specs += [pl.BlockSpec((None, 1, mod.shape[2]), mod_idx), _const_spec((1, d))]
    args += [mod, g.reshape(1, d)]
    in_specs += [_const_spec(w.shape) for w in weights]
    args += list(weights)

    out_specs, out_shapes = [], []
    if add_pos:
        out_specs.append(row_spec(d))
        out_shapes.append(jax.ShapeDtypeStruct((b, t, d), F32))
    for idx, n in enumerate(widths):
        if idx == 5:
            out_specs.append(pl.BlockSpec((tm, n), lambda bi, i: (i, bi)))
            out_shapes.append(jax.ShapeDtypeStruct((t, b * n), F32))
        else:
            out_specs.append(row_spec(n))
            out_shapes.append(jax.ShapeDtypeStruct((b, t, n), BF16 if idx in (2, 4) else F32))
    return pl.pallas_call(
        functools.partial(_inproj_kernel, d=d, add_pos=add_pos, n_sub=2),
        grid=(b, t // tm),
        in_specs=in_specs,
        out_specs=out_specs,
        out_shape=out_shapes,
        compiler_params=pltpu.CompilerParams(
            dimension_semantics=("parallel", "parallel"), vmem_limit_bytes=VMEM_LIMIT),
        name="inproj",
    )(*args)


S5_BLOCK = 4


def _s5_kernel(*refs, tt, w, n_p, zero_init, emit_state):
    u_ref, l4r_ref, l4i_ref, wzr_ref, wzi_ref, wor_ref, woi_ref, wt_ref = refs[:8]
    rest = refs[8:]
    if not zero_init:
        s0r_ref, s0i_ref = rest[:2]
        rest = rest[2:]
    if emit_state:
        y_ref, fr_ref, fi_ref = rest[:3]
        rest = rest[3:]
    else:
        y_ref = rest[0]
        rest = rest[1:]
    usc, xsc, zre, zim, ypk, car, cai = rest
    direction = pl.program_id(1)
    i = pl.program_id(2)
    n_slab = w // LANES
    n_blk = tt // S5_BLOCK
    half = LANES // 2
    tw = 2 * LANES
    grp = S5_BLOCK * SUBLANES

    @pl.when(i == 0)
    def _():
        if zero_init:
            car[...] = jnp.zeros_like(car)
            cai[...] = jnp.zeros_like(cai)
        else:
            car[...] = s0r_ref[...]
            cai[...] = s0i_ref[...]

    for b in range(SUBLANES):
        for s in range(n_slab):
            usc[s, pl.ds(b, tt, stride=SUBLANES), :] = u_ref[:, b * w + s * LANES:b * w + (s + 1) * LANES]

    low = lax.broadcasted_iota(jnp.int32, (2 * SUBLANES, LANES), 1) < half
    swap = lambda a: pltpu.roll(a, half, 1)

    def pack(s):
        for r2 in range(n_blk // 2):
            a = [jnp.concatenate([usc[s, pl.ds(base + j * SUBLANES, SUBLANES), :]
                                  for base in (2 * r2 * grp, (2 * r2 + 1) * grp)], axis=0)
                 for j in range(S5_BLOCK)]
            asw = [swap(v) for v in a]
            for h in range(2):
                pick = lambda j, slot_half: a[j] if h == slot_half else asw[j]
                cols = [jnp.where(low, pick(2 * c, 0), pick(2 * c + 1, 1)) for c in range(2)]
                xsc[2 * s + h, pl.ds(r2 * 2 * SUBLANES, 2 * SUBLANES), :] = (
                    jnp.concatenate(cols, axis=1).astype(BF16))

    def block_inputs(p):
        x = xsc[p]
        zre[:, p * tw:(p + 1) * tw] = _dot(x, wzr_ref[p])
        zim[:, p * tw:(p + 1) * tw] = _dot(x, wzi_ref[p])

    def scan(s):
        ls = slice(s * 2 * tw, (s + 1) * 2 * tw)
        lr = l4r_ref[:, ls]
        li = l4i_ref[:, ls]
        cr, ci = car[:, ls], cai[:, ls]
        for j in range(n_blk):
            r = j + direction * (n_blk - 1 - 2 * j)
            r0 = pl.multiple_of(r * SUBLANES, SUBLANES)
            zr = zre[pl.ds(r0, SUBLANES), ls]
            zi = zim[pl.ds(r0, SUBLANES), ls]
            zre[pl.ds(r0, SUBLANES), ls] = cr
            zim[pl.ds(r0, SUBLANES), ls] = ci
            cr, ci = lr * cr - li * ci + zr, lr * ci + li * cr + zi
        car[:, ls] = cr
        cai[:, ls] = ci

    def block_outputs(p):
        ypk[p] = (_dot(zre[:, p * tw:(p + 1) * tw].astype(BF16), wor_ref[p])
                  + _dot(zim[:, p * tw:(p + 1) * tw].astype(BF16), woi_ref[p])
                  + _dot(xsc[p], wt_ref[p]))

    def unpack(s):
        for r2 in range(n_blk // 2):
            rows = pl.ds(r2 * 2 * SUBLANES, 2 * SUBLANES)
            for j in range(S5_BLOCK):
                c, slot_half = divmod(j, 2)
                left = ypk[2 * s, rows, c * LANES:(c + 1) * LANES]
                right = ypk[2 * s + 1, rows, c * LANES:(c + 1) * LANES]
                v = jnp.where(low, left if slot_half == 0 else swap(left),
                              right if slot_half == 1 else swap(right))
                usc[s, pl.ds(2 * r2 * grp + j * SUBLANES, SUBLANES), :] = v[:SUBLANES]
                usc[s, pl.ds((2 * r2 + 1) * grp + j * SUBLANES, SUBLANES), :] = v[SUBLANES:]

    for s in range(n_slab + 1):
        if s < n_slab:
            pack(s)
            block_inputs(2 * s)
            block_inputs(2 * s + 1)
        if s >= 1:
            block_outputs(2 * s - 2)
            block_outputs(2 * s - 1)
            unpack(s - 1)
        if s < n_slab:
            scan(s)

    for b in range(SUBLANES):
        for s in range(n_slab):
            y_ref[:, b * w + s * LANES:b * w + (s + 1) * LANES] = usc[s, pl.ds(b, tt, stride=SUBLANES), :]

    if emit_state:
        @pl.when(i == pl.num_programs(2) - 1)
        def _():
            fr_ref[...] = car[...]
            fi_ref[...] = cai[...]


def _s5_scan(u_tm, prm, layer, s0, b, emit_state, tt):
    t = u_tm.shape[0]
    w = u_tm.shape[1] // b
    l4r, l4i = prm[:2]
    mats = prm[2:]
    n_p = mats[0].shape[2]
    ns = l4r.shape[-1]
    n_t = t // tt
    n_g = b // SUBLANES
    rows = tt // S5_BLOCK * SUBLANES
    zero_init = s0 is None
    tile = lambda g, d, i: i + d * (n_t - 1 - 2 * i)
    dir_spec = lambda shape: pl.BlockSpec((None, None) + shape,
                                          lambda g, d, i: (layer, d) + (0,) * len(shape))
    in_specs = [pl.BlockSpec((tt, SUBLANES * w), lambda g, d, i: (tile(g, d, i), g)),
                dir_spec((SUBLANES, ns)), dir_spec((SUBLANES, ns))]
    in_specs += [dir_spec(m.shape[2:]) for m in mats]
    args = [u_tm, l4r, l4i] + list(mats)
    state_spec = pl.BlockSpec((None, SUBLANES, ns), lambda g, d, i: (d, g, 0))
    if not zero_init:
        in_specs += [state_spec, state_spec]
        args += list(s0)
    out_specs = [pl.BlockSpec((None, tt, SUBLANES * w), lambda g, d, i: (d, tile(g, d, i), g))]
    out_shapes = [jax.ShapeDtypeStruct((N_DIR, t, b * w), F32)]
    if emit_state:
        out_specs += [state_spec, state_spec]
        out_shapes += [jax.ShapeDtypeStruct((N_DIR, b, ns), F32)] * 2
    res = pl.pallas_call(
        functools.partial(_s5_kernel, tt=tt, w=w, n_p=n_p, zero_init=zero_init,
                          emit_state=emit_state),
        grid=(n_g, N_DIR, n_t),
        in_specs=in_specs,
        out_specs=out_specs,
        out_shape=out_shapes,
        scratch_shapes=[pltpu.VMEM((w // LANES, tt * SUBLANES, LANES), F32),
                        pltpu.VMEM((n_p, rows, 2 * LANES), BF16),
                        pltpu.VMEM((rows, ns), F32), pltpu.VMEM((rows, ns), F32),
                        pltpu.VMEM((n_p, rows, 2 * LANES), F32),
                        pltpu.VMEM((SUBLANES, ns), F32), pltpu.VMEM((SUBLANES, ns), F32)],
        compiler_params=pltpu.CompilerParams(
            dimension_semantics=("parallel", "parallel", "arbitrary"), vmem_limit_bytes=VMEM_LIMIT),
        name="s5_scan",
    )(*args)
    if emit_state:
        return res[0], res[1], res[2]
    return res[0]


def _s5_params(lam_re, lam_im, log_step, b_re, b_im, c_re, c_im):
    depth, n_dir, g, p = lam_re.shape
    n = b_re.shape[-1]
    blk = S5_BLOCK
    gpt = 2 * LANES // p
    n_p = g // gpt
    tw = 2 * LANES
    assert gpt * n * blk == tw and n_dir == N_DIR
    hi = lax.Precision.HIGHEST
    step = jnp.exp(log_step)[..., None]
    a = lam_re * step
    th = lam_im * step
    k = jnp.arange(blk + 1, dtype=F32).reshape(-1, 1, 1, 1, 1)
    pw_re = jnp.exp(k * a) * jnp.cos(k * th)
    pw_im = jnp.exp(k * a) * jnp.sin(k * th)
    den = lam_re * lam_re + lam_im * lam_im
    fr = ((pw_re[1] - 1.0) * lam_re + pw_im[1] * lam_im) / den
    fi = (pw_im[1] * lam_re - (pw_re[1] - 1.0) * lam_im) / den
    bbr = fr[..., None] * b_re - fi[..., None] * b_im
    bbi = fr[..., None] * b_im + fi[..., None] * b_re
    lb_re = pw_re[..., None] * bbr - pw_im[..., None] * bbi
    lb_im = pw_re[..., None] * bbi + pw_im[..., None] * bbr
    offs = np.arange(blk)
    pos = np.stack([offs, blk - 1 - offs])

    def per_dir(arr, idx):
        return jnp.stack([arr[idx[d], :, d] for d in range(n_dir)], axis=idx.ndim)

    split = lambda m, ax: m.reshape(m.shape[:ax] + (n_p, gpt) + m.shape[ax + 1:])
    ridx, cidx = np.arange(tw)[:, None], np.arange(tw)[None, :]
    rep_state = jnp.asarray(np.arange(p)[:, None] == cidx % p, BF16)
    chan = lambda c: (c // (gpt * n)) * n + c % n
    rep_chan = jnp.asarray(np.arange(blk * n)[:, None] == chan(cidx), BF16)
    grp_of_chan = lambda c: (c // n) % gpt
    mask_z = jnp.asarray(grp_of_chan(ridx) == cidx // p, BF16)
    mask_o = jnp.asarray(ridx // p == grp_of_chan(cidx), BF16)
    mask_t = jnp.asarray(grp_of_chan(ridx) == grp_of_chan(cidx), BF16)

    def dense(compact, rep, mask):
        full = jnp.einsum("ldprk,kc->ldprc", compact.astype(BF16), rep, preferred_element_type=BF16)
        return full * mask

    wz = [dense(split(per_dir(m, blk - 1 - pos), 3).transpose(1, 2, 3, 0, 4, 6, 5)
                .reshape(depth, n_dir, n_p, tw, p), rep_state, mask_z) for m in (lb_re, lb_im)]
    pr = per_dir(pw_re, pos + 1)[:, :, :, :, None, :]
    pi = per_dir(pw_im, pos + 1)[:, :, :, :, None, :]
    wo = [dense(split(m, 3).transpose(1, 2, 3, 4, 6, 0, 5).reshape(depth, n_dir, n_p, tw, blk * n),
                rep_chan, mask_o) for m in (c_re * pr - c_im * pi, -(c_re * pi + c_im * pr))]
    kern = (jnp.einsum("ldgnp,mldgpq->mldgqn", c_re, lb_re[:blk], precision=hi)
            - jnp.einsum("ldgnp,mldgpq->mldgqn", c_im, lb_im[:blk], precision=hi))
    lag = pos[:, None, :] - pos[:, :, None]
    valid = jnp.asarray((lag >= 0).transpose(1, 2, 0), F32).reshape(blk, blk, 1, n_dir, 1, 1, 1)
    wt = dense(split(per_dir(kern, np.maximum(lag, 0)) * valid, 4).transpose(2, 3, 4, 0, 5, 6, 1, 7)
               .reshape(depth, n_dir, n_p, tw, blk * n), rep_chan, mask_t)
    bcast = lambda v: jnp.broadcast_to(v.reshape(depth, n_dir, 1, g * p),
                                       (depth, n_dir, SUBLANES, g * p))
    return [bcast(pw_re[blk]), bcast(pw_im[blk])] + wz + wo + [wt]


def _log_sigmoid(z):
    return -(jnp.maximum(-z, 0.0) + jnp.log(1.0 + jnp.exp(-jnp.abs(z))))


def _split2(x):
    hi = x.astype(BF16)
    return hi, (x - hi.astype(F32)).astype(BF16)


def _gla_kernel(*refs, n_chunks, dk, dv, zero_init, emit_state, aliased_states, layer):
    dir_refs = (refs[0:4], refs[4:8])
    wup_ref, bg_ref = refs[8:10]
    rest = refs[10:]
    if not zero_init:
        s0_ref = rest[0]
        rest = rest[1:]
    if aliased_states:
        rest = rest[1:]
    if emit_state:
        of_ref, ob_ref, sf_ref, st, prep, dec = rest
    else:
        of_ref, ob_ref, st, prep, dec = rest
    o_refs = (of_ref, ob_ref)
    i = pl.program_id(1)
    scale = dk ** -0.5

    @pl.when(i == 0)
    def _():
        for d in range(N_DIR):
            for h in range(N_HEADS):
                if zero_init:
                    st[d, h] = jnp.zeros((dk, dv), F32)
                else:
                    st[d, h] = s0_ref[d, h]

    row = lax.broadcasted_iota(jnp.int32, (CHUNK, CHUNK), 0)
    col = lax.broadcasted_iota(jnp.int32, (CHUNK, CHUNK), 1)
    causal = (col <= row, col >= row)
    ones = tuple(jnp.where(c, 1.0, 0.0).astype(BF16) for c in causal)

    dims = (((1,), (1,)), ((), ()))
    chunk_of = lambda d, cc: cc if d == 0 else n_chunks - 1 - cc
    rows_of = lambda d, cc: pl.ds(chunk_of(d, cc) * CHUNK, CHUNK)

    def gate_z(d, cc):
        glr_ref = dir_refs[d][3]
        return _dot(glr_ref[rows_of(d, cc), :], wup_ref[d]) + bg_ref[d]

    def gate_cum(d, g):
        return sum(_dot(ones[d], part) for part in _split2(g))

    def gate_store(slot, d, cc, g, cum):
        q_ref, k_ref = dir_refs[d][:2]
        rows = rows_of(d, cc)
        total = jnp.sum(g, axis=0, keepdims=True)
        prep[slot, d, 0] = (q_ref[rows, :] * scale * jnp.exp(cum)).astype(BF16)
        kk = k_ref[rows, :]
        prep[slot, d, 1] = (kk * jnp.exp(-cum)).astype(BF16)
        prep[slot, d, 2] = (kk * jnp.exp(total - cum)).astype(BF16)
        decay = jnp.exp(total)
        for h in range(N_HEADS):
            dec[slot, d, h] = jnp.broadcast_to(decay[:, ksl(h)], (dk, dk)).T

    pairs = [(d, h) for d in range(N_DIR) for h in range(N_HEADS)]
    ksl = lambda h: slice(h * dk, (h + 1) * dk)
    vsl = lambda h: slice(h * dv, (h + 1) * dv)

    for d in range(N_DIR):
        g0 = _log_sigmoid(gate_z(d, 0)) / GATE_TAU
        gate_store(0, d, 0, g0, gate_cum(d, g0))

    for cc in range(n_chunks):
        slot = cc % 2
        more = cc + 1 < n_chunks
        if more:
            z_next = [gate_z(d, cc + 1) for d in range(N_DIR)]
        sc = {}
        for d, h in pairs:
            s = lax.dot_general(prep[slot, d, 0, :, ksl(h)], prep[slot, d, 1, :, ksl(h)], dims,
                                preferred_element_type=F32)
            sc[d, h] = jnp.where(causal[d], s, 0.0).astype(BF16)
        if more:
            g_next = [_log_sigmoid(z) / GATE_TAU for z in z_next]
        for d, h in pairs:
            v_ref = dir_refs[d][2]
            rows = rows_of(d, cc)
            vh = v_ref[rows, vsl(h)]
            o = _dot(sc[d, h], vh) + _dot(prep[slot, d, 0, :, ksl(h)], st[d, h].astype(BF16))
            o_refs[d][rows, vsl(h)] = o
        if more:
            cum_next = [gate_cum(d, g_next[d]) for d in range(N_DIR)]
        for d, h in pairs:
            v_ref = dir_refs[d][2]
            vh = v_ref[rows_of(d, cc), vsl(h)]
            upd = lax.dot_general(prep[slot, d, 2, :, ksl(h)], vh, (((0,), (0,)), ((), ())),
                                  preferred_element_type=F32)
            decay = jnp.concatenate([dec[slot, d, h]] * (dv // dk), axis=1)
            st[d, h] = decay * st[d, h] + upd
        if more:
            for d in range(N_DIR):
                gate_store(1 - slot, d, cc + 1, g_next[d], cum_next[d])

    if emit_state:
        @pl.when(i == pl.num_programs(1) - 1)
        def _():
            if not aliased_states:
                sf_ref[...] = jnp.zeros_like(sf_ref)
            for d in range(N_DIR):
                for h in range(N_HEADS):
                    if aliased_states:
                        sf_ref[d, h] = st[d, h]
                    else:
                        sf_ref[layer, d, h] = st[d, h]


def _gla_scan(q, k, v, glr, wup, bg, layer, depth, s0, states, emit_state, tt):
    b, t, dkk = q.shape
    dvv = v.shape[-1]
    dk, dv = dkk // N_HEADS, dvv // N_HEADS
    n_t = t // tt
    zero_init = s0 is None
    fwd_spec = lambda n: pl.BlockSpec((None, tt, n), lambda bi, i: (bi, i, 0))
    bwd_spec = lambda n: pl.BlockSpec((None, tt, n), lambda bi, i: (bi, n_t - 1 - i, 0))
    state_spec = pl.BlockSpec((None, None, N_DIR, N_HEADS, dk, dv), lambda bi, i: (bi, layer, 0, 0, 0, 0))
    widths = (dkk, dkk, dvv, glr.shape[-1])
    in_specs = ([fwd_spec(n) for n in widths] + [bwd_spec(n) for n in widths]
                + [_const_spec(wup.shape), _const_spec(bg.shape)])
    args = [q, k, v, glr, q, k, v, glr, wup, bg]
    if not zero_init:
        in_specs.append(state_spec)
        args.append(s0)
    out_specs = [fwd_spec(dvv), bwd_spec(dvv)]
    out_shapes = [jax.ShapeDtypeStruct((b, t, dvv), F32)] * 2
    aliases = {}
    if emit_state:
        out_shapes.append(jax.ShapeDtypeStruct((b, depth, N_DIR, N_HEADS, dk, dv), F32))
        if states is None:
            out_specs.append(pl.BlockSpec((None, depth, N_DIR, N_HEADS, dk, dv),
                                          lambda bi, i: (bi, 0, 0, 0, 0, 0)))
        else:
            out_specs.append(state_spec)
            in_specs.append(pl.BlockSpec(memory_space=pl.ANY))
            args.append(states)
            aliases = {len(args) - 1: 2}
    return pl.pallas_call(
        functools.partial(_gla_kernel, n_chunks=tt // CHUNK, dk=dk, dv=dv, zero_init=zero_init,
                          emit_state=emit_state, aliased_states=bool(aliases), layer=layer),
        grid=(b, n_t),
        in_specs=in_specs,
        out_specs=out_specs,
        out_shape=out_shapes,
        input_output_aliases=aliases,
        scratch_shapes=[pltpu.VMEM((N_DIR, N_HEADS, dk, dv), F32),
                        pltpu.VMEM((2, N_DIR, 3, CHUNK, dkk), BF16),
                        pltpu.VMEM((2, N_DIR, N_HEADS, dk, dk), F32)],
        compiler_params=pltpu.CompilerParams(
            dimension_semantics=("parallel", "arbitrary"), vmem_limit_bytes=VMEM_LIMIT),
        name="gla_scan",
    )(*args)


def _gelu_tanh(x):
    return 0.5 * x * (1.0 + jnp.tanh(math.sqrt(2.0 / math.pi) * (x + 0.044715 * (x * x * x))))


def _mixmlp_kernel(*refs, d, dv, ff_block, final_norm, n_sub):
    (x_ref, of_ref, ob_ref, r_ref, ga_ref, gb_ref, u_ref, yf_ref, yb_ref, mod_ref, gn_ref, d_ref,
     wglu_ref, bglu_ref, wpg_ref, wps_ref, wout_ref, g2_ref, w1_ref, w2_ref) = refs[:20]
    if final_norm:
        fg_ref, o_ref = refs[20:]
    else:
        o_ref = refs[20]
    sub = x_ref.shape[0] // n_sub
    for si in range(n_sub):
        rows = slice(si * sub, (si + 1) * sub)
        o = of_ref[rows, :] + ob_ref[rows, :]
        r = r_ref[rows, :]
        gn = gn_ref[...]
        parts = []
        for h in range(N_HEADS):
            vs = slice(h * dv, (h + 1) * dv)
            oh = o[:, vs]
            ms = jnp.mean(jnp.square(oh), axis=-1, keepdims=True)
            parts.append((oh * lax.rsqrt(ms + EPS) * gn * _silu(r[:, vs])).astype(BF16))
        pg = _dot(jnp.concatenate(parts, axis=-1), wpg_ref[...])

        u = u_ref[rows, :]
        y = _gelu_tanh(yf_ref[rows, :] + yb_ref[rows, :] + d_ref[...] * u)
        y = y * jax.nn.sigmoid(_dot(y.astype(BF16), wglu_ref[...]) + bglu_ref[...])
        ps = _dot(y.astype(BF16), wps_ref[...])

        merged = jax.nn.sigmoid(ga_ref[rows, :]) * pg + jax.nn.sigmoid(gb_ref[rows, :]) * ps
        gate = mod_ref[:, 2 * d:3 * d]
        x = x_ref[rows, :] + gate * _dot(merged.astype(BF16), wout_ref[...])

        shift = mod_ref[:, 3 * d:4 * d]
        scale = mod_ref[:, 4 * d:5 * d]
        gate = mod_ref[:, 5 * d:6 * d]
        h = _rms_mod(x, g2_ref[...], scale, shift).astype(BF16)
        acc = jnp.zeros(x.shape, F32)
        for j in range(w1_ref.shape[1] // ff_block):
            cs = slice(j * ff_block, (j + 1) * ff_block)
            a = jnp.square(jnp.maximum(_dot(h, w1_ref[:, cs]), 0.0)).astype(BF16)
            acc = acc + _dot(a, w2_ref[cs, :])
        x = x + gate * acc
        if final_norm:
            ms = jnp.mean(jnp.square(x), axis=-1, keepdims=True)
            x = x * lax.rsqrt(ms + EPS) * fg_ref[...]
        o_ref[rows, :] = x


def _mixmlp(x, o_f, o_b, r, ga, gb, u_tm, y, mod, p, final_g, tm):
    b, t, d = x.shape
    dvv = r.shape[-1]
    w = u_tm.shape[1] // b
    per_batch_mod = mod.shape[0] != 1
    final_norm = final_g is not None
    row_spec = lambda n: pl.BlockSpec((None, tm, n), lambda bi, i: (bi, i, 0))
    mod_idx = (lambda bi, i: (bi, 0, 0)) if per_batch_mod else (lambda bi, i: (0, 0, 0))
    weights = [p["w_glu"], p["b_glu"].reshape(1, -1), p["w_proj_gla"], p["w_proj_s5"], p["w_out"],
               p["norm2_g"].reshape(1, d), p["w_ff1"], p["w_ff2"]]
    if final_norm:
        weights.append(final_g.reshape(1, d))
    in_specs = [
        row_spec(d), row_spec(dvv), row_spec(dvv), row_spec(dvv), row_spec(d), row_spec(d),
        pl.BlockSpec((tm, w), lambda bi, i: (i, bi)),
        pl.BlockSpec((None, tm, w), lambda bi, i: (0, i, bi)),
        pl.BlockSpec((None, tm, w), lambda bi, i: (1, i, bi)),
        pl.BlockSpec((None, 1, mod.shape[2]), mod_idx),
        _const_spec((1, dvv // N_HEADS)), _const_spec((1, w)),
    ] + [_const_spec(a.shape) for a in weights]
    return pl.pallas_call(
        functools.partial(_mixmlp_kernel, d=d, dv=dvv // N_HEADS, ff_block=1024,
                          final_norm=final_norm, n_sub=1),
        grid=(b, t // tm),
        in_specs=in_specs,
        out_specs=row_spec(d),
        out_shape=jax.ShapeDtypeStruct((b, t, d), F32),
        compiler_params=pltpu.CompilerParams(
            dimension_semantics=("parallel", "parallel"), vmem_limit_bytes=VMEM_LIMIT),
        name="mixmlp",
    )(x, o_f, o_b, r, ga, gb, u_tm, y, y, mod, p["gla_norm_g"].reshape(1, -1),
      p["s5_d"].reshape(1, -1), *weights)


def _grid_pos_embed(n_tokens, dim):
    rows = n_tokens // GRID_W
    r = jnp.repeat(jnp.arange(rows, dtype=F32), GRID_W)
    col = jnp.tile(jnp.arange(GRID_W, dtype=F32), rows)
    quarter = dim // 4
    omega = 1.0 / (POS_BASE ** (jnp.arange(quarter, dtype=F32) / quarter))
    ar = r[:, None] * omega
    ac = col[:, None] * omega
    return jnp.concatenate([jnp.sin(ar), jnp.cos(ar), jnp.sin(ac), jnp.cos(ac)], axis=-1)


def _split_w_in(w_in, dk_all, dv_all, s5w, d):
    splits = (dk_all, dk_all, dv_all, dv_all, N_DIR * GATE_RANK, s5w, d, d)
    idx = np.cumsum((0,) + splits)
    parts = [w_in[:, idx[j]:idx[j + 1]] for j in range(len(splits))]
    parts[4] = jnp.pad(parts[4], ((0, 0), (0, LANES - splits[4])))
    return [p.astype(BF16) for p in parts]


def _stream(x, pos, mods, gla_s0, s5_s0, layers, s5_prm, final_g, emit_state, tm):
    b, t, d = x.shape
    gla_states, s5_re, s5_im = None, [], []
    for l, p in enumerate(layers):
        res = _inproj(x, pos if l == 0 else None, mods[l], p["norm1_g"], p["w_in"], min(t, 2 * tm))
        if l == 0 and pos is not None:
            x = res[0]
            res = res[1:]
        q, k, v, r, glr, u_tm, ga, gb = res
        s5_out = _s5_scan(u_tm, s5_prm, l, None if s5_s0 is None else s5_s0[l], b, emit_state,
                          tt=min(t, 128))
        gla_out = _gla_scan(q, k, v, glr, p["wup"], p["bg"], l, len(layers), gla_s0, gla_states,
                            emit_state, tt=min(t, 512))
        o_f, o_b = gla_out[:2]
        if emit_state:
            y, f_re, f_im = s5_out
            gla_states = gla_out[2]
            s5_re.append(f_re)
            s5_im.append(f_im)
        else:
            y = s5_out
        last = l == len(layers) - 1
        x = _mixmlp(x, o_f, o_b, r, ga, gb, u_tm, y, mods[l], p, final_g if last else None, tm)
    return x, gla_states, s5_re, s5_im


def kernel(x_prompt, x_sample, c, cache_gla_state, state_s5_re, state_s5_im, c_ctx, w_mod, b_mod,
           norm1_g, w_in, w_gate_up, b_gate, gla_norm_g, w_proj_gla, s5_lam_re, s5_lam_im,
           s5_log_step, s5_b_re, s5_b_im, s5_c_re, s5_c_im, s5_d, w_glu, b_glu, w_proj_s5, w_out,
           norm2_g, w_ff1, w_ff2, final_g):
    depth = w_in.shape[0]
    nb, seq, d = x_prompt.shape
    db, dseq, _ = x_sample.shape
    dk_all = w_gate_up.shape[-1]
    dv_all = w_proj_gla.shape[1]
    s5w = s5_d.shape[-1]
    n_groups, n_state = s5_lam_re.shape[2], s5_lam_re.shape[3]
    dk, dv = dk_all // N_HEADS, dv_all // N_HEADS
    assert nb % SUBLANES == 0 and db % SUBLANES == 0

    n_cond = -(-(db + 1) // SUBLANES) * SUBLANES
    conds = jnp.concatenate([c, c_ctx[None], jnp.zeros((n_cond - db - 1, d), F32)], axis=0)
    mod_all = _modulation(conds, w_mod, b_mod)
    mods_lat = [mod_all[l, :db].reshape(db, 1, N_MOD * d) for l in range(depth)]
    mods_ctx = [mod_all[l, db:db + 1].reshape(1, 1, N_MOD * d) for l in range(depth)]

    s5_prm = _s5_params(s5_lam_re, s5_lam_im, s5_log_step, s5_b_re, s5_b_im, s5_c_re, s5_c_im)
    wup = jnp.stack([jnp.pad(w_gate_up[:, dd], ((0, 0), (dd * GATE_RANK, LANES - (dd + 1) * GATE_RANK),
                                                (0, 0))) for dd in range(N_DIR)], axis=1).astype(BF16)
    layers = []
    for l in range(depth):
        layers.append(dict(
            norm1_g=norm1_g[l], w_in=_split_w_in(w_in[l], dk_all, dv_all, s5w, d),
            wup=wup[l], bg=b_gate[l].reshape(N_DIR, 1, dk_all),
            gla_norm_g=gla_norm_g[l], w_proj_gla=w_proj_gla[l].astype(BF16),
            s5_d=s5_d[l], w_glu=w_glu[l].astype(BF16), b_glu=b_glu[l],
            w_proj_s5=w_proj_s5[l].astype(BF16), w_out=w_out[l].astype(BF16),
            norm2_g=norm2_g[l], w_ff1=w_ff1[l].astype(BF16), w_ff2=w_ff2[l].astype(BF16)))

    y_prompt, new_gla_state, s5_re, s5_im = _stream(
        x_prompt, None, mods_ctx, None, None, layers, s5_prm, final_g, True, tm=min(seq, 256))
    to_state = lambda a: a.transpose(1, 0, 2).reshape(nb, N_DIR, n_groups, n_state)
    new_s5_re = jnp.stack([to_state(a) for a in s5_re], axis=1)
    new_s5_im = jnp.stack([to_state(a) for a in s5_im], axis=1)

    pos = _grid_pos_embed(dseq, d)
    from_state = lambda a: a.reshape(db, N_DIR, n_groups * n_state).transpose(1, 0, 2)
    s5_s0 = [(from_state(state_s5_re[:, l]), from_state(state_s5_im[:, l])) for l in range(depth)]
    y_sample, _, _, _ = _stream(
        x_sample, pos, mods_lat, cache_gla_state, s5_s0, layers, s5_prm, final_g, False,
        tm=min(dseq, 256))

    return (y_prompt, y_sample, new_gla_state, new_s5_re, new_s5_im)
```

```python
import functools
import math

import jax
import jax.numpy as jnp
import numpy as np
from jax import lax
from jax.experimental import pallas as pl
from jax.experimental.pallas import tpu as pltpu

N_DIR = 2
N_HEADS = 4
GATE_RANK = 16
GATE_TAU = 16.0
CHUNK = 64
S5_GROUP = 16
S5_STATE = 64
N_MOD = 6
EPS = 1e-6
GRID_W = 64
POS_BASE = 10000.0

LANES = 128
SUBLANES = 8
VMEM_LIMIT = 56 * 1024 * 1024

F32 = jnp.float32
BF16 = jnp.bfloat16


def _const_spec(shape):
    nd = len(shape)
    return pl.BlockSpec(shape, lambda *_: (0,) * nd, pipeline_mode=pl.Buffered(1))


def _dot(a, b):
    return jnp.dot(a, b, preferred_element_type=F32)


def _rms_mod(x, g, scale, shift):
    ms = jnp.mean(jnp.square(x), axis=-1, keepdims=True)
    return x * lax.rsqrt(ms + EPS) * g * (1.0 + scale) + shift


def _silu(x):
    return x * jax.nn.sigmoid(x)


def _mod_kernel(c_ref, w_ref, b_ref, o_ref):
    a = _silu(c_ref[...]).astype(BF16)
    o_ref[...] = _dot(a, w_ref[...].astype(BF16)) + b_ref[...]


def _modulation(conds, w_mod, b_mod):
    depth, d, dm = w_mod.shape
    rows = conds.shape[0]
    tn = 1024
    return pl.pallas_call(
        _mod_kernel,
        grid=(depth, dm // tn),
        in_specs=[
            pl.BlockSpec((rows, d), lambda l, j: (0, 0)),
            pl.BlockSpec((None, d, tn), lambda l, j: (l, 0, j)),
            pl.BlockSpec((None, 1, tn), lambda l, j: (l, 0, j)),
        ],
        out_specs=pl.BlockSpec((None, rows, tn), lambda l, j: (l, 0, j)),
        out_shape=jax.ShapeDtypeStruct((depth, rows, dm), F32),
        compiler_params=pltpu.CompilerParams(vmem_limit_bytes=VMEM_LIMIT),
        name="modulation",
    )(conds, w_mod, b_mod.reshape(depth, 1, dm))


def _inproj_kernel(*refs, d, add_pos, n_sub):
    if add_pos:
        x_ref, prow_ref, pcol_ref, mod_ref, g_ref = refs[:5]
        rest = refs[5:]
    else:
        x_ref, mod_ref, g_ref = refs[:3]
        rest = refs[3:]
    w_refs = rest[:8]
    out_refs = rest[8:]
    shift = mod_ref[:, 0:d]
    scale = mod_ref[:, d:2 * d]
    sub = x_ref.shape[0] // n_sub
    for si in range(n_sub):
        rows = slice(si * sub, (si + 1) * sub)
        x = x_ref[rows, :]
        proj_refs = out_refs
        if add_pos:
            n_rows = sub // GRID_W
            prow = prow_ref[si * n_rows:(si + 1) * n_rows, :]
            prow = jnp.concatenate([jnp.broadcast_to(prow[r:r + 1], (GRID_W, prow.shape[1]))
                                    for r in range(n_rows)], axis=0)
            pcol = jnp.concatenate([pcol_ref[...]] * n_rows, axis=0)
            x = x + jnp.concatenate([prow, pcol], axis=1)
            out_refs[0][rows, :] = x
            proj_refs = out_refs[1:]
        h = _rms_mod(x, g_ref[...], scale, shift).astype(BF16)
        for w_ref, o_ref in zip(w_refs, proj_refs):
            o_ref[rows, :] = _dot(h, w_ref[...]).astype(o_ref.dtype)


def _inproj(x, pos, mod, g, weights, tm):
    b, t, d = x.shape
    add_pos = pos is not None
    per_batch_mod = mod.shape[0] != 1
    widths = [w.shape[1] for w in weights]
    row_spec = lambda n: pl.BlockSpec((None, tm, n), lambda bi, i: (bi, i, 0))
    in_specs = [row_spec(d)]
    args = [x]
    if add_pos:
        assert tm % (2 * GRID_W) == 0
        in_specs += [pl.BlockSpec((tm // GRID_W, d // 2), lambda bi, i: (i, 0)),
                     _const_spec(pos[1].shape)]
        args += list(pos)
    mod_idx = (lambda bi, i: (bi, 0, 0)) if per_batch_mod else (lambda bi, i: (0, 0, 0))
    in_specs += [pl.BlockSpec((None, 1, mod.shape[2]), mod_idx), _const_spec((1, d))]
    args += [mod, g.reshape(1, d)]
    in_specs += [_const_spec(w.shape) for w in weights]
    args += list(weights)

    out_specs, out_shapes = [], []
    if add_pos:
        out_specs.append(row_spec(d))
        out_shapes.append(jax.ShapeDtypeStruct((b, t, d), F32))
    for idx, n in enumerate(widths):
        if idx == 5:
            out_specs.append(pl.BlockSpec((tm, n), lambda bi, i: (i, bi)))
            out_shapes.append(jax.ShapeDtypeStruct((t, b * n), F32))
        else:
            out_specs.append(row_spec(n))
            out_shapes.append(jax.ShapeDtypeStruct((b, t, n), BF16 if idx in (2, 4) else F32))
    return pl.pallas_call(
        functools.partial(_inproj_kernel, d=d, add_pos=add_pos, n_sub=2),
        grid=(b, t // tm),
        in_specs=in_specs,
        out_specs=out_specs,
        out_shape=out_shapes,
        compiler_params=pltpu.CompilerParams(
            dimension_semantics=("parallel", "parallel"), vmem_limit_bytes=VMEM_LIMIT),
        name="inproj",
    )(*args)


S5_BLOCK = 4


def _s5_kernel(*refs, tt, w, n_p, zero_init, emit_state):
    u_ref, l4r_ref, l4i_ref, wzr_ref, wzi_ref, wor_ref, woi_ref, wt_ref = refs[:8]
    rest = refs[8:]
    if not zero_init:
        s0r_ref, s0i_ref = rest[:2]
        rest = rest[2:]
    if emit_state:
        y_ref, fr_ref, fi_ref = rest[:3]
        rest = rest[3:]
    else:
        y_ref = rest[0]
        rest = rest[1:]
    usc, xsc, zre, zim, ypk, car, cai = rest
    direction = pl.program_id(1)
    i = pl.program_id(2)
    n_slab = w // LANES
    n_blk = tt // S5_BLOCK
    half = LANES // 2
    tw = 2 * LANES
    grp = S5_BLOCK * SUBLANES

    @pl.when(i == 0)
    def _():
        if zero_init:
            car[...] = jnp.zeros_like(car)
            cai[...] = jnp.zeros_like(cai)
        else:
            car[...] = s0r_ref[...]
            cai[...] = s0i_ref[...]

    for b in range(SUBLANES):
        for s in range(n_slab):
            usc[s, pl.ds(b, tt, stride=SUBLANES), :] = u_ref[:, b * w + s * LANES:b * w + (s + 1) * LANES]

    low = lax.broadcasted_iota(jnp.int32, (2 * SUBLANES, LANES), 1) < half
    swap = lambda a: pltpu.roll(a, half, 1)

    def pack(s):
        for r2 in range(n_blk // 2):
            a = [jnp.concatenate([usc[s, pl.ds(base + j * SUBLANES, SUBLANES), :]
                                  for base in (2 * r2 * grp, (2 * r2 + 1) * grp)], axis=0)
                 for j in range(S5_BLOCK)]
            asw = [swap(v) for v in a]
            for h in range(2):
                pick = lambda j, slot_half: a[j] if h == slot_half else asw[j]
                cols = [jnp.where(low, pick(2 * c, 0), pick(2 * c + 1, 1)) for c in range(2)]
                xsc[2 * s + h, pl.ds(r2 * 2 * SUBLANES, 2 * SUBLANES), :] = (
                    jnp.concatenate(cols, axis=1).astype(BF16))

    def block_inputs(p):
        x = xsc[p]
        zre[:, p * tw:(p + 1) * tw] = _dot(x, wzr_ref[p])
        zim[:, p * tw:(p + 1) * tw] = _dot(x, wzi_ref[p])

    def scan(s):
        ls = slice(s * 2 * tw, (s + 1) * 2 * tw)
        lr = l4r_ref[:, ls]
        li = l4i_ref[:, ls]
        cr, ci = car[:, ls], cai[:, ls]
        for j in range(n_blk):
            r = j + direction * (n_blk - 1 - 2 * j)
            r0 = pl.multiple_of(r * SUBLANES, SUBLANES)
            zr = zre[pl.ds(r0, SUBLANES), ls]
            zi = zim[pl.ds(r0, SUBLANES), ls]
            zre[pl.ds(r0, SUBLANES), ls] = cr
            zim[pl.ds(r0, SUBLANES), ls] = ci
            cr, ci = lr * cr - li * ci + zr, lr * ci + li * cr + zi
        car[:, ls] = cr
        cai[:, ls] = ci

    def block_outputs(p):
        ypk[p] = (_dot(zre[:, p * tw:(p + 1) * tw].astype(BF16), wor_ref[p])
                  + _dot(zim[:, p * tw:(p + 1) * tw].astype(BF16), woi_ref[p])
                  + _dot(xsc[p], wt_ref[p]))

    def unpack(s):
        for r2 in range(n_blk // 2):
            rows = pl.ds(r2 * 2 * SUBLANES, 2 * SUBLANES)
            for j in range(S5_BLOCK):
                c, slot_half = divmod(j, 2)
                left = ypk[2 * s, rows, c * LANES:(c + 1) * LANES]
                right = ypk[2 * s + 1, rows, c * LANES:(c + 1) * LANES]
                v = jnp.where(low, left if slot_half == 0 else swap(left),
                              right if slot_half == 1 else swap(right))
                usc[s, pl.ds(2 * r2 * grp + j * SUBLANES, SUBLANES), :] = v[:SUBLANES]
                usc[s, pl.ds((2 * r2 + 1) * grp + j * SUBLANES, SUBLANES), :] = v[SUBLANES:]

    for s in range(n_slab + 1):
        if s < n_slab:
            pack(s)
            block_inputs(2 * s)
            block_inputs(2 * s + 1)
        if s >= 1:
            block_outputs(2 * s - 2)
            block_outputs(2 * s - 1)
            unpack(s - 1)
        if s < n_slab:
            scan(s)

    for b in range(SUBLANES):
        for s in range(n_slab):
            y_ref[:, b * w + s * LANES:b * w + (s + 1) * LANES] = usc[s, pl.ds(b, tt, stride=SUBLANES), :]

    if emit_state:
        @pl.when(i == pl.num_programs(2) - 1)
        def _():
            fr_ref[...] = car[...]
            fi_ref[...] = cai[...]


def _s5_scan(u_tm, prm, layer, s0, b, emit_state, tt):
    t = u_tm.shape[0]
    w = u_tm.shape[1] // b
    l4r, l4i = prm[:2]
    mats = prm[2:]
    n_p = mats[0].shape[2]
    ns = l4r.shape[-1]
    n_t = t // tt
    n_g = b // SUBLANES
    rows = tt // S5_BLOCK * SUBLANES
    zero_init = s0 is None
    tile = lambda g, d, i: i + d * (n_t - 1 - 2 * i)
    dir_spec = lambda shape: pl.BlockSpec((None, None) + shape,
                                          lambda g, d, i: (layer, d) + (0,) * len(shape))
    in_specs = [pl.BlockSpec((tt, SUBLANES * w), lambda g, d, i: (tile(g, d, i), g)),
                dir_spec((SUBLANES, ns)), dir_spec((SUBLANES, ns))]
    in_specs += [dir_spec(m.shape[2:]) for m in mats]
    args = [u_tm, l4r, l4i] + list(mats)
    state_spec = pl.BlockSpec((None, SUBLANES, ns), lambda g, d, i: (d, g, 0))
    if not zero_init:
        in_specs += [state_spec, state_spec]
        args += list(s0)
    out_specs = [pl.BlockSpec((None, tt, SUBLANES * w), lambda g, d, i: (d, tile(g, d, i), g))]
    out_shapes = [jax.ShapeDtypeStruct((N_DIR, t, b * w), F32)]
    if emit_state:
        out_specs += [state_spec, state_spec]
        out_shapes += [jax.ShapeDtypeStruct((N_DIR, b, ns), F32)] * 2
    res = pl.pallas_call(
        functools.partial(_s5_kernel, tt=tt, w=w, n_p=n_p, zero_init=zero_init,
                          emit_state=emit_state),
        grid=(n_g, N_DIR, n_t),
        in_specs=in_specs,
        out_specs=out_specs,
        out_shape=out_shapes,
        scratch_shapes=[pltpu.VMEM((w // LANES, tt * SUBLANES, LANES), F32),
                        pltpu.VMEM((n_p, rows, 2 * LANES), BF16),
                        pltpu.VMEM((rows, ns), F32), pltpu.VMEM((rows, ns), F32),
                        pltpu.VMEM((n_p, rows, 2 * LANES), F32),
                        pltpu.VMEM((SUBLANES, ns), F32), pltpu.VMEM((SUBLANES, ns), F32)],
        compiler_params=pltpu.CompilerParams(
            dimension_semantics=("parallel", "parallel", "arbitrary"), vmem_limit_bytes=VMEM_LIMIT),
        name="s5_scan",
    )(*args)
    if emit_state:
        return res[0], res[1], res[2]
    return res[0]


def _s5_params(lam_re, lam_im, log_step, b_re, b_im, c_re, c_im):
    depth, n_dir, g, p = lam_re.shape
    n = b_re.shape[-1]
    blk = S5_BLOCK
    gpt = 2 * LANES // p
    n_p = g // gpt
    tw = 2 * LANES
    assert gpt * n * blk == tw and n_dir == N_DIR
    hi = lax.Precision.HIGHEST
    step = jnp.exp(log_step)[..., None]
    a = lam_re * step
    th = lam_im * step
    k = jnp.arange(blk + 1, dtype=F32).reshape(-1, 1, 1, 1, 1)
    pw_re = jnp.exp(k * a) * jnp.cos(k * th)
    pw_im = jnp.exp(k * a) * jnp.sin(k * th)
    den = lam_re * lam_re + lam_im * lam_im
    fr = ((pw_re[1] - 1.0) * lam_re + pw_im[1] * lam_im) / den
    fi = (pw_im[1] * lam_re - (pw_re[1] - 1.0) * lam_im) / den
    bbr = fr[..., None] * b_re - fi[..., None] * b_im
    bbi = fr[..., None] * b_im + fi[..., None] * b_re
    lb_re = pw_re[..., None] * bbr - pw_im[..., None] * bbi
    lb_im = pw_re[..., None] * bbi + pw_im[..., None] * bbr
    offs = np.arange(blk)
    pos = np.stack([offs, blk - 1 - offs])

    def per_dir(arr, idx):
        return jnp.stack([arr[idx[d], :, d] for d in range(n_dir)], axis=idx.ndim)

    split = lambda m, ax: m.reshape(m.shape[:ax] + (n_p, gpt) + m.shape[ax + 1:])
    ridx, cidx = np.arange(tw)[:, None], np.arange(tw)[None, :]
    rep_state = jnp.asarray(np.arange(p)[:, None] == cidx % p, BF16)
    chan = lambda c: (c // (gpt * n)) * n + c % n
    rep_chan = jnp.asarray(np.arange(blk * n)[:, None] == chan(cidx), BF16)
    grp_of_chan = lambda c: (c // n) % gpt
    mask_z = jnp.asarray(grp_of_chan(ridx) == cidx // p, BF16)
    mask_o = jnp.asarray(ridx // p == grp_of_chan(cidx), BF16)
    mask_t = jnp.asarray(grp_of_chan(ridx) == grp_of_chan(cidx), BF16)

    def dense(compact, rep, mask):
        full = jnp.einsum("ldprk,kc->ldprc", compact.astype(BF16), rep, preferred_element_type=BF16)
        return full * mask

    wz = [dense(split(per_dir(m, blk - 1 - pos), 3).transpose(1, 2, 3, 0, 4, 6, 5)
                .reshape(depth, n_dir, n_p, tw, p), rep_state, mask_z) for m in (lb_re, lb_im)]
    pr = per_dir(pw_re, pos + 1)[:, :, :, :, None, :]
    pi = per_dir(pw_im, pos + 1)[:, :, :, :, None, :]
    wo = [dense(split(m, 3).transpose(1, 2, 3, 4, 6, 0, 5).reshape(depth, n_dir, n_p, tw, blk * n),
                rep_chan, mask_o) for m in (c_re * pr - c_im * pi, -(c_re * pi + c_im * pr))]
    kern = (jnp.einsum("ldgnp,mldgpq->mldgqn", c_re, lb_re[:blk], precision=hi)
            - jnp.einsum("ldgnp,mldgpq->mldgqn", c_im, lb_im[:blk], precision=hi))
    lag = pos[:, None, :] - pos[:, :, None]
    valid = jnp.asarray((lag >= 0).transpose(1, 2, 0), F32).reshape(blk, blk, 1, n_dir, 1, 1, 1)
    wt = dense(split(per_dir(kern, np.maximum(lag, 0)) * valid, 4).transpose(2, 3, 4, 0, 5, 6, 1, 7)
               .reshape(depth, n_dir, n_p, tw, blk * n), rep_chan, mask_t)
    bcast = lambda v: jnp.broadcast_to(v.reshape(depth, n_dir, 1, g * p),
                                       (depth, n_dir, SUBLANES, g * p))
    return [bcast(pw_re[blk]), bcast(pw_im[blk])] + wz + wo + [wt]


def _log_sigmoid(z):
    return -(jnp.maximum(-z, 0.0) + jnp.log(1.0 + jnp.exp(-jnp.abs(z))))


def _split2(x):
    hi = x.astype(BF16)
    return hi, (x - hi.astype(F32)).astype(BF16)


def _gla_kernel(*refs, n_chunks, dk, dv, zero_init, emit_state, aliased_states, layer):
    dir_refs = (refs[0:4], refs[4:8])
    wup_ref, bg_ref = refs[8:10]
    rest = refs[10:]
    if not zero_init:
        s0_ref = rest[0]
        rest = rest[1:]
    if aliased_states:
        rest = rest[1:]
    if emit_state:
        of_ref, ob_ref, sf_ref, st, prep, dec = rest
    else:
        of_ref, ob_ref, st, prep, dec = rest
    o_refs = (of_ref, ob_ref)
    i = pl.program_id(1)
    scale = dk ** -0.5

    @pl.when(i == 0)
    def _():
        for d in range(N_DIR):
            for h in range(N_HEADS):
                if zero_init:
                    st[d, h] = jnp.zeros((dk, dv), F32)
                else:
                    st[d, h] = s0_ref[d, h]

    row = lax.broadcasted_iota(jnp.int32, (CHUNK, CHUNK), 0)
    col = lax.broadcasted_iota(jnp.int32, (CHUNK, CHUNK), 1)
    causal = (col <= row, col >= row)
    row2 = lax.broadcasted_iota(jnp.int32, (CHUNK, 2 * CHUNK), 0)
    col2 = lax.broadcasted_iota(jnp.int32, (CHUNK, 2 * CHUNK), 1) % CHUNK
    causal2 = (col2 <= row2, col2 >= row2)
    ones = tuple(jnp.where(c, 1.0, 0.0).astype(BF16) for c in causal)
    first_head = lax.broadcasted_iota(jnp.int32, (CHUNK, 2 * dk), 1) < dk

    dims = (((1,), (1,)), ((), ()))
    chunk_of = lambda d, cc: cc if d == 0 else n_chunks - 1 - cc
    rows_of = lambda d, cc: pl.ds(chunk_of(d, cc) * CHUNK, CHUNK)

    def gate_z(d, cc):
        glr_ref = dir_refs[d][3]
        return _dot(glr_ref[rows_of(d, cc), :], wup_ref[d]) + bg_ref[d]

    def gate_cum(d, g):
        return sum(_dot(ones[d], part) for part in _split2(g))

    def gate_store(slot, d, cc, g, cum):
        q_ref, k_ref = dir_refs[d][:2]
        rows = rows_of(d, cc)
        total = jnp.sum(g, axis=0, keepdims=True)
        prep[slot, d, 0] = (q_ref[rows, :] * scale * jnp.exp(cum)).astype(BF16)
        kk = k_ref[rows, :]
        prep[slot, d, 1] = (kk * jnp.exp(-cum)).astype(BF16)
        prep[slot, d, 2] = (kk * jnp.exp(total - cum)).astype(BF16)
        decay = jnp.exp(total)
        for h in range(N_HEADS):
            dec[slot, d, h] = jnp.broadcast_to(decay[:, ksl(h)], (dk, dk)).T

    pairs = [(d, h) for d in range(N_DIR) for h in range(N_HEADS)]
    ksl = lambda h: slice(h * dk, (h + 1) * dk)
    vsl = lambda h: slice(h * dv, (h + 1) * dv)

    for d in range(N_DIR):
        g0 = _log_sigmoid(gate_z(d, 0)) / GATE_TAU
        gate_store(0, d, 0, g0, gate_cum(d, g0))

    for cc in range(n_chunks):
        slot = cc % 2
        more = cc + 1 < n_chunks
        if more:
            z_next = [gate_z(d, cc + 1) for d in range(N_DIR)]
        sc = {}
        for d in range(N_DIR):
            for hp in range(N_HEADS // 2):
                ks2 = slice(2 * hp * dk, (2 * hp + 2) * dk)
                k2 = prep[slot, d, 1, :, ks2]
                none = jnp.zeros_like(k2)
                k_blocks = jnp.concatenate([jnp.where(first_head, k2, none),
                                            jnp.where(first_head, none, k2)], axis=0)
                s = lax.dot_general(prep[slot, d, 0, :, ks2], k_blocks, dims,
                                    preferred_element_type=F32)
                sc[d, hp] = jnp.where(causal2[d], s, 0.0).astype(BF16)
        if more:
            g_next = [_log_sigmoid(z) / GATE_TAU for z in z_next]
        for d, h in pairs:
            v_ref = dir_refs[d][2]
            rows = rows_of(d, cc)
            vh = v_ref[rows, vsl(h)]
            none = jnp.zeros_like(vh)
            lhs = jnp.concatenate([prep[slot, d, 0, :, ksl(h)], sc[d, h // 2]], axis=1)
            rhs = jnp.concatenate([st[d, h].astype(BF16)] + ([vh, none] if h % 2 == 0 else [none, vh]),
                                  axis=0)
            o_refs[d][rows, vsl(h)] = _dot(lhs, rhs)
        if more:
            cum_next = [gate_cum(d, g_next[d]) for d in range(N_DIR)]
        for d, h in pairs:
            v_ref = dir_refs[d][2]
            vh = v_ref[rows_of(d, cc), vsl(h)]
            upd = lax.dot_general(prep[slot, d, 2, :, ksl(h)], vh, (((0,), (0,)), ((), ())),
                                  preferred_element_type=F32)
            decay = jnp.concatenate([dec[slot, d, h]] * (dv // dk), axis=1)
            st[d, h] = decay * st[d, h] + upd
        if more:
            for d in range(N_DIR):
                gate_store(1 - slot, d, cc + 1, g_next[d], cum_next[d])

    if emit_state:
        @pl.when(i == pl.num_programs(1) - 1)
        def _():
            if not aliased_states:
                sf_ref[...] = jnp.zeros_like(sf_ref)
            for d in range(N_DIR):
                for h in range(N_HEADS):
                    if aliased_states:
                        sf_ref[d, h] = st[d, h]
                    else:
                        sf_ref[layer, d, h] = st[d, h]


def _gla_scan(q, k, v, glr, wup, bg, layer, depth, s0, states, emit_state, tt):
    b, t, dkk = q.shape
    dvv = v.shape[-1]
    dk, dv = dkk // N_HEADS, dvv // N_HEADS
    n_t = t // tt
    zero_init = s0 is None
    fwd_spec = lambda n: pl.BlockSpec((None, tt, n), lambda bi, i: (bi, i, 0))
    bwd_spec = lambda n: pl.BlockSpec((None, tt, n), lambda bi, i: (bi, n_t - 1 - i, 0))
    state_spec = pl.BlockSpec((None, None, N_DIR, N_HEADS, dk, dv), lambda bi, i: (bi, layer, 0, 0, 0, 0))
    widths = (dkk, dkk, dvv, glr.shape[-1])
    in_specs = ([fwd_spec(n) for n in widths] + [bwd_spec(n) for n in widths]
                + [_const_spec(wup.shape), _const_spec(bg.shape)])
    args = [q, k, v, glr, q, k, v, glr, wup, bg]
    if not zero_init:
        in_specs.append(state_spec)
        args.append(s0)
    out_specs = [fwd_spec(dvv), bwd_spec(dvv)]
    out_shapes = [jax.ShapeDtypeStruct((b, t, dvv), F32)] * 2
    aliases = {}
    if emit_state:
        out_shapes.append(jax.ShapeDtypeStruct((b, depth, N_DIR, N_HEADS, dk, dv), F32))
        if states is None:
            out_specs.append(pl.BlockSpec((None, depth, N_DIR, N_HEADS, dk, dv),
                                          lambda bi, i: (bi, 0, 0, 0, 0, 0)))
        else:
            out_specs.append(state_spec)
            in_specs.append(pl.BlockSpec(memory_space=pl.ANY))
            args.append(states)
            aliases = {len(args) - 1: 2}
    return pl.pallas_call(
        functools.partial(_gla_kernel, n_chunks=tt // CHUNK, dk=dk, dv=dv, zero_init=zero_init,
                          emit_state=emit_state, aliased_states=bool(aliases), layer=layer),
        grid=(b, n_t),
        in_specs=in_specs,
        out_specs=out_specs,
        out_shape=out_shapes,
        input_output_aliases=aliases,
        scratch_shapes=[pltpu.VMEM((N_DIR, N_HEADS, dk, dv), F32),
                        pltpu.VMEM((2, N_DIR, 3, CHUNK, dkk), BF16),
                        pltpu.VMEM((2, N_DIR, N_HEADS, dk, dk), F32)],
        compiler_params=pltpu.CompilerParams(
            dimension_semantics=("parallel", "arbitrary"), vmem_limit_bytes=VMEM_LIMIT),
        name="gla_scan",
    )(*args)


def _gelu_tanh(x):
    return 0.5 * x * (1.0 + jnp.tanh(math.sqrt(2.0 / math.pi) * (x + 0.044715 * (x * x * x))))


def _mixmlp_kernel(*refs, d, dv, ff_block, final_norm, n_sub):
    (x_ref, of_ref, ob_ref, r_ref, ga_ref, gb_ref, u_ref, yf_ref, yb_ref, mod_ref, gn_ref, d_ref,
     wglu_ref, bglu_ref, wpg_ref, wps_ref, wout_ref, g2_ref, w1_ref, w2_ref) = refs[:20]
    if final_norm:
        fg_ref, o_ref = refs[20:]
    else:
        o_ref = refs[20]
    sub = x_ref.shape[0] // n_sub
    for si in range(n_sub):
        rows = slice(si * sub, (si + 1) * sub)
        o = of_ref[rows, :] + ob_ref[rows, :]
        r = r_ref[rows, :]
        gn = gn_ref[...]
        parts = []
        for h in range(N_HEADS):
            vs = slice(h * dv, (h + 1) * dv)
            oh = o[:, vs]
            ms = jnp.mean(jnp.square(oh), axis=-1, keepdims=True)
            parts.append((oh * lax.rsqrt(ms + EPS) * gn * _silu(r[:, vs])).astype(BF16))
        pg = _dot(jnp.concatenate(parts, axis=-1), wpg_ref[...])

        u = u_ref[rows, :]
        y = _gelu_tanh(yf_ref[rows, :] + yb_ref[rows, :] + d_ref[...] * u)
        y = y * jax.nn.sigmoid(_dot(y.astype(BF16), wglu_ref[...]) + bglu_ref[...])
        ps = _dot(y.astype(BF16), wps_ref[...])

        merged = jax.nn.sigmoid(ga_ref[rows, :]) * pg + jax.nn.sigmoid(gb_ref[rows, :]) * ps
        gate = mod_ref[:, 2 * d:3 * d]
        x = x_ref[rows, :] + gate * _dot(merged.astype(BF16), wout_ref[...])

        shift = mod_ref[:, 3 * d:4 * d]
        scale = mod_ref[:, 4 * d:5 * d]
        gate = mod_ref[:, 5 * d:6 * d]
        h = _rms_mod(x, g2_ref[...], scale, shift).astype(BF16)
        acc = jnp.zeros(x.shape, F32)
        for j in range(w1_ref.shape[1] // ff_block):
            cs = slice(j * ff_block, (j + 1) * ff_block)
            a = jnp.square(jnp.maximum(_dot(h, w1_ref[:, cs]), 0.0)).astype(BF16)
            acc = acc + _dot(a, w2_ref[cs, :])
        x = x + gate * acc
        if final_norm:
            ms = jnp.mean(jnp.square(x), axis=-1, keepdims=True)
            x = x * lax.rsqrt(ms + EPS) * fg_ref[...]
        o_ref[rows, :] = x


def _mixmlp(x, o_f, o_b, r, ga, gb, u_tm, y, mod, p, final_g, tm):
    b, t, d = x.shape
    dvv = r.shape[-1]
    w = u_tm.shape[1] // b
    per_batch_mod = mod.shape[0] != 1
    final_norm = final_g is not None
    row_spec = lambda n: pl.BlockSpec((None, tm, n), lambda bi, i: (bi, i, 0))
    mod_idx = (lambda bi, i: (bi, 0, 0)) if per_batch_mod else (lambda bi, i: (0, 0, 0))
    weights = [p["w_glu"], p["b_glu"].reshape(1, -1), p["w_proj_gla"], p["w_proj_s5"], p["w_out"],
               p["norm2_g"].reshape(1, d), p["w_ff1"], p["w_ff2"]]
    if final_norm:
        weights.append(final_g.reshape(1, d))
    in_specs = [
        row_spec(d), row_spec(dvv), row_spec(dvv), row_spec(dvv), row_spec(d), row_spec(d),
        pl.BlockSpec((tm, w), lambda bi, i: (i, bi)),
        pl.BlockSpec((None, tm, w), lambda bi, i: (0, i, bi)),
        pl.BlockSpec((None, tm, w), lambda bi, i: (1, i, bi)),
        pl.BlockSpec((None, 1, mod.shape[2]), mod_idx),
        _const_spec((1, dvv // N_HEADS)), _const_spec((1, w)),
    ] + [_const_spec(a.shape) for a in weights]
    return pl.pallas_call(
        functools.partial(_mixmlp_kernel, d=d, dv=dvv // N_HEADS, ff_block=1024,
                          final_norm=final_norm, n_sub=1),
        grid=(b, t // tm),
        in_specs=in_specs,
        out_specs=row_spec(d),
        out_shape=jax.ShapeDtypeStruct((b, t, d), F32),
        compiler_params=pltpu.CompilerParams(
            dimension_semantics=("parallel", "parallel"), vmem_limit_bytes=VMEM_LIMIT),
        name="mixmlp",
    )(x, o_f, o_b, r, ga, gb, u_tm, y, y, mod, p["gla_norm_g"].reshape(1, -1),
      p["s5_d"].reshape(1, -1), *weights)


def _grid_pos_tables(n_tokens, dim):
    quarter = dim // 4
    omega = 1.0 / (POS_BASE ** (jnp.arange(quarter, dtype=F32) / quarter))
    ar = jnp.arange(n_tokens // GRID_W, dtype=F32)[:, None] * omega
    ac = jnp.arange(GRID_W, dtype=F32)[:, None] * omega
    return (jnp.concatenate([jnp.sin(ar), jnp.cos(ar)], axis=-1),
            jnp.concatenate([jnp.sin(ac), jnp.cos(ac)], axis=-1))


def _split_w_in(w_in, dk_all, dv_all, s5w, d):
    splits = (dk_all, dk_all, dv_all, dv_all, N_DIR * GATE_RANK, s5w, d, d)
    idx = np.cumsum((0,) + splits)
    parts = [w_in[:, idx[j]:idx[j + 1]] for j in range(len(splits))]
    parts[4] = jnp.pad(parts[4], ((0, 0), (0, LANES - splits[4])))
    return [p.astype(BF16) for p in parts]


def _stream(x, pos, mods, gla_s0, s5_s0, layers, s5_prm, final_g, emit_state, tm):
    b, t, d = x.shape
    gla_states, s5_re, s5_im = None, [], []
    for l, p in enumerate(layers):
        res = _inproj(x, pos if l == 0 else None, mods[l], p["norm1_g"], p["w_in"], min(t, 2 * tm))
        if l == 0 and pos is not None:
            x = res[0]
            res = res[1:]
        q, k, v, r, glr, u_tm, ga, gb = res
        s5_out = _s5_scan(u_tm, s5_prm, l, None if s5_s0 is None else s5_s0[l], b, emit_state,
                          tt=min(t, 128))
        gla_out = _gla_scan(q, k, v, glr, p["wup"], p["bg"], l, len(layers), gla_s0, gla_states,
                            emit_state, tt=min(t, 512))
        o_f, o_b = gla_out[:2]
        if emit_state:
            y, f_re, f_im = s5_out
            gla_states = gla_out[2]
            s5_re.append(f_re)
            s5_im.append(f_im)
        else:
            y = s5_out
        last = l == len(layers) - 1
        x = _mixmlp(x, o_f, o_b, r, ga, gb, u_tm, y, mods[l], p, final_g if last else None, tm)
    return x, gla_states, s5_re, s5_im


def kernel(x_prompt, x_sample, c, cache_gla_state, state_s5_re, state_s5_im, c_ctx, w_mod, b_mod,
           norm1_g, w_in, w_gate_up, b_gate, gla_norm_g, w_proj_gla, s5_lam_re, s5_lam_im,
           s5_log_step, s5_b_re, s5_b_im, s5_c_re, s5_c_im, s5_d, w_glu, b_glu, w_proj_s5, w_out,
           norm2_g, w_ff1, w_ff2, final_g):
    depth = w_in.shape[0]
    nb, seq, d = x_prompt.shape
    db, dseq, _ = x_sample.shape
    dk_all = w_gate_up.shape[-1]
    dv_all = w_proj_gla.shape[1]
    s5w = s5_d.shape[-1]
    n_groups, n_state = s5_lam_re.shape[2], s5_lam_re.shape[3]
    dk, dv = dk_all // N_HEADS, dv_all // N_HEADS
    assert nb % SUBLANES == 0 and db % SUBLANES == 0

    n_cond = -(-(db + 1) // SUBLANES) * SUBLANES
    conds = jnp.concatenate([c, c_ctx[None], jnp.zeros((n_cond - db - 1, d), F32)], axis=0)
    mod_all = _modulation(conds, w_mod, b_mod)
    mods_lat = [mod_all[l, :db].reshape(db, 1, N_MOD * d) for l in range(depth)]
    mods_ctx = [mod_all[l, db:db + 1].reshape(1, 1, N_MOD * d) for l in range(depth)]

    s5_prm = _s5_params(s5_lam_re, s5_lam_im, s5_log_step, s5_b_re, s5_b_im, s5_c_re, s5_c_im)
    wup = jnp.stack([jnp.pad(w_gate_up[:, dd], ((0, 0), (dd * GATE_RANK, LANES - (dd + 1) * GATE_RANK),
                                                (0, 0))) for dd in range(N_DIR)], axis=1).astype(BF16)
    layers = []
    for l in range(depth):
        layers.append(dict(
            norm1_g=norm1_g[l], w_in=_split_w_in(w_in[l], dk_all, dv_all, s5w, d),
            wup=wup[l], bg=b_gate[l].reshape(N_DIR, 1, dk_all),
            gla_norm_g=gla_norm_g[l], w_proj_gla=w_proj_gla[l].astype(BF16),
            s5_d=s5_d[l], w_glu=w_glu[l].astype(BF16), b_glu=b_glu[l],
            w_proj_s5=w_proj_s5[l].astype(BF16), w_out=w_out[l].astype(BF16),
            norm2_g=norm2_g[l], w_ff1=w_ff1[l].astype(BF16), w_ff2=w_ff2[l].astype(BF16)))

    y_prompt, new_gla_state, s5_re, s5_im = _stream(
        x_prompt, None, mods_ctx, None, None, layers, s5_prm, final_g, True, tm=min(seq, 256))
    to_state = lambda a: a.transpose(1, 0, 2).reshape(nb, N_DIR, n_groups, n_state)
    new_s5_re = jnp.stack([to_state(a) for a in s5_re], axis=1)
    new_s5_im = jnp.stack([to_state(a) for a in s5_im], axis=1)

    pos = _grid_pos_tables(dseq, d)
    from_state = lambda a: a.reshape(db, N_DIR, n_groups * n_state).transpose(1, 0, 2)
    s5_s0 = [(from_state(state_s5_re[:, l]), from_state(state_s5_im[:, l])) for l in range(depth)]
    y_sample, _, _, _ = _stream(
        x_sample, pos, mods_lat, cache_gla_state, s5_s0, layers, s5_prm, final_g, False,
        tm=min(dseq, 256))

    return (y_prompt, y_sample, new_gla_state, new_s5_re, new_s5_im)
```

```python
import functools
import math

import jax
import jax.numpy as jnp
import numpy as np
from jax import lax
from jax.experimental import pallas as pl
from jax.experimental.pallas import tpu as pltpu

N_DIR = 2
N_HEADS = 4
GATE_RANK = 16
GATE_TAU = 16.0
CHUNK = 64
S5_GROUP = 16
S5_STATE = 64
N_MOD = 6
EPS = 1e-6
GRID_W = 64
POS_BASE = 10000.0

LANES = 128
SUBLANES = 8
VMEM_LIMIT = 56 * 1024 * 1024

F32 = jnp.float32
BF16 = jnp.bfloat16


def _const_spec(shape):
    nd = len(shape)
    return pl.BlockSpec(shape, lambda *_: (0,) * nd, pipeline_mode=pl.Buffered(1))


def _dot(a, b):
    return jnp.dot(a, b, preferred_element_type=F32)


def _rms_mod(x, g, scale, shift):
    ms = jnp.mean(jnp.square(x), axis=-1, keepdims=True)
    return x * lax.rsqrt(ms + EPS) * g * (1.0 + scale) + shift


def _silu(x):
    return x * jax.nn.sigmoid(x)


def _mod_kernel(c_ref, w_ref, b_ref, o_ref):
    a = _silu(c_ref[...]).astype(BF16)
    o_ref[...] = _dot(a, w_ref[...].astype(BF16)) + b_ref[...]


def _modulation(conds, w_mod, b_mod):
    depth, d, dm = w_mod.shape
    rows = conds.shape[0]
    tn = 1024
    return pl.pallas_call(
        _mod_kernel,
        grid=(depth, dm // tn),
        in_specs=[
            pl.BlockSpec((rows, d), lambda l, j: (0, 0)),
            pl.BlockSpec((None, d, tn), lambda l, j: (l, 0, j)),
            pl.BlockSpec((None, 1, tn), lambda l, j: (l, 0, j)),
        ],
        out_specs=pl.BlockSpec((None, rows, tn), lambda l, j: (l, 0, j)),
        out_shape=jax.ShapeDtypeStruct((depth, rows, dm), F32),
        compiler_params=pltpu.CompilerParams(vmem_limit_bytes=VMEM_LIMIT),
        name="modulation",
    )(conds, w_mod, b_mod.reshape(depth, 1, dm))


def _inproj_kernel(*refs, d, add_pos, n_sub):
    if add_pos:
        x_ref, prow_ref, pcol_ref, mod_ref, g_ref = refs[:5]
        rest = refs[5:]
    else:
        x_ref, mod_ref, g_ref = refs[:3]
        rest = refs[3:]
    w_refs = rest[:8]
    out_refs = rest[8:]
    shift = mod_ref[:, 0:d]
    scale = mod_ref[:, d:2 * d]
    sub = x_ref.shape[0] // n_sub
    for si in range(n_sub):
        rows = slice(si * sub, (si + 1) * sub)
        x = x_ref[rows, :]
        proj_refs = out_refs
        if add_pos:
            n_rows = sub // GRID_W
            prow = prow_ref[si * n_rows:(si + 1) * n_rows, :]
            prow = jnp.concatenate([jnp.broadcast_to(prow[r:r + 1], (GRID_W, prow.shape[1]))
                                    for r in range(n_rows)], axis=0)
            pcol = jnp.concatenate([pcol_ref[...]] * n_rows, axis=0)
            x = x + jnp.concatenate([prow, pcol], axis=1)
            out_refs[0][rows, :] = x
            proj_refs = out_refs[1:]
        h = _rms_mod(x, g_ref[...], scale, shift).astype(BF16)
        for w_ref, o_ref in zip(w_refs, proj_refs):
            o_ref[rows, :] = _dot(h, w_ref[...]).astype(o_ref.dtype)


def _inproj(x, pos, mod, g, weights, tm):
    b, t, d = x.shape
    add_pos = pos is not None
    per_batch_mod = mod.shape[0] != 1
    widths = [w.shape[1] for w in weights]
    row_spec = lambda n: pl.BlockSpec((None, tm, n), lambda bi, i: (bi, i, 0))
    in_specs = [row_spec(d)]
    args = [x]
    if add_pos:
        assert tm % (2 * GRID_W) == 0
        in_specs += [pl.BlockSpec((tm // GRID_W, d // 2), lambda bi, i: (i, 0)),
                     _const_spec(pos[1].shape)]
        args += list(pos)
    mod_idx = (lambda bi, i: (bi, 0, 0)) if per_batch_mod else (lambda bi, i: (0, 0, 0))
    in_specs += [pl.BlockSpec((None, 1, mod.shape[2]), mod_idx), _const_spec((1, d))]
    args += [mod, g.reshape(1, d)]
    in_specs += [_const_spec(w.shape) for w in weights]
    args += list(weights)

    out_specs, out_shapes = [], []
    if add_pos:
        out_specs.append(row_spec(d))
        out_shapes.append(jax.ShapeDtypeStruct((b, t, d), F32))
    for idx, n in enumerate(widths):
        if idx == 5:
            out_specs.append(pl.BlockSpec((tm, n), lambda bi, i: (i, bi)))
            out_shapes.append(jax.ShapeDtypeStruct((t, b * n), F32))
        else:
            out_specs.append(row_spec(n))
            out_shapes.append(jax.ShapeDtypeStruct((b, t, n), BF16 if idx in (2, 4) else F32))
    return pl.pallas_call(
        functools.partial(_inproj_kernel, d=d, add_pos=add_pos, n_sub=2),
        grid=(b, t // tm),
        in_specs=in_specs,
        out_specs=out_specs,
        out_shape=out_shapes,
        compiler_params=pltpu.CompilerParams(
            dimension_semantics=("parallel", "parallel"), vmem_limit_bytes=VMEM_LIMIT),
        name="inproj",
    )(*args)


S5_BLOCK = 4


def _s5_kernel(*refs, tt, w, n_p, zero_init, emit_state):
    u_ref, l4r_ref, l4i_ref, wzr_ref, wzi_ref, wor_ref, woi_ref, wt_ref = refs[:8]
    rest = refs[8:]
    if not zero_init:
        s0r_ref, s0i_ref = rest[:2]
        rest = rest[2:]
    if emit_state:
        y_ref, fr_ref, fi_ref = rest[:3]
        rest = rest[3:]
    else:
        y_ref = rest[0]
        rest = rest[1:]
    usc, xsc, zre, zim, ypk, car, cai = rest
    direction = pl.program_id(1)
    i = pl.program_id(2)
    n_slab = w // LANES
    n_blk = tt // S5_BLOCK
    half = LANES // 2
    tw = 2 * LANES
    grp = S5_BLOCK * SUBLANES

    @pl.when(i == 0)
    def _():
        if zero_init:
            car[...] = jnp.zeros_like(car)
            cai[...] = jnp.zeros_like(cai)
        else:
            car[...] = s0r_ref[...]
            cai[...] = s0i_ref[...]

    for b in range(SUBLANES):
        for s in range(n_slab):
            usc[s, pl.ds(b, tt, stride=SUBLANES), :] = u_ref[:, b * w + s * LANES:b * w + (s + 1) * LANES]

    low = lax.broadcasted_iota(jnp.int32, (2 * SUBLANES, LANES), 1) < half
    swap = lambda a: pltpu.roll(a, half, 1)

    def pack(s):
        for r2 in range(n_blk // 2):
            a = [jnp.concatenate([usc[s, pl.ds(base + j * SUBLANES, SUBLANES), :]
                                  for base in (2 * r2 * grp, (2 * r2 + 1) * grp)], axis=0)
                 for j in range(S5_BLOCK)]
            asw = [swap(v) for v in a]
            for h in range(2):
                pick = lambda j, slot_half: a[j] if h == slot_half else asw[j]
                cols = [jnp.where(low, pick(2 * c, 0), pick(2 * c + 1, 1)) for c in range(2)]
                xsc[2 * s + h, pl.ds(r2 * 2 * SUBLANES, 2 * SUBLANES), :] = (
                    jnp.concatenate(cols, axis=1).astype(BF16))

    def block_inputs(p):
        x = xsc[p]
        zre[:, p * tw:(p + 1) * tw] = _dot(x, wzr_ref[p])
        zim[:, p * tw:(p + 1) * tw] = _dot(x, wzi_ref[p])

    def scan(s):
        ls = slice(s * 2 * tw, (s + 1) * 2 * tw)
        lr = l4r_ref[:, ls]
        li = l4i_ref[:, ls]
        cr, ci = car[:, ls], cai[:, ls]
        for j in range(n_blk):
            r = j + direction * (n_blk - 1 - 2 * j)
            r0 = pl.multiple_of(r * SUBLANES, SUBLANES)
            zr = zre[pl.ds(r0, SUBLANES), ls]
            zi = zim[pl.ds(r0, SUBLANES), ls]
            zre[pl.ds(r0, SUBLANES), ls] = cr
            zim[pl.ds(r0, SUBLANES), ls] = ci
            cr, ci = lr * cr - li * ci + zr, lr * ci + li * cr + zi
        car[:, ls] = cr
        cai[:, ls] = ci

    def block_outputs(p):
        ypk[p] = (_dot(zre[:, p * tw:(p + 1) * tw].astype(BF16), wor_ref[p])
                  + _dot(zim[:, p * tw:(p + 1) * tw].astype(BF16), woi_ref[p])
                  + _dot(xsc[p], wt_ref[p]))

    def unpack(s):
        for r2 in range(n_blk // 2):
            rows = pl.ds(r2 * 2 * SUBLANES, 2 * SUBLANES)
            for j in range(S5_BLOCK):
                c, slot_half = divmod(j, 2)
                left = ypk[2 * s, rows, c * LANES:(c + 1) * LANES]
                right = ypk[2 * s + 1, rows, c * LANES:(c + 1) * LANES]
                v = jnp.where(low, left if slot_half == 0 else swap(left),
                              right if slot_half == 1 else swap(right))
                usc[s, pl.ds(2 * r2 * grp + j * SUBLANES, SUBLANES), :] = v[:SUBLANES]
                usc[s, pl.ds((2 * r2 + 1) * grp + j * SUBLANES, SUBLANES), :] = v[SUBLANES:]

    for s in range(n_slab + 1):
        if s < n_slab:
            pack(s)
            block_inputs(2 * s)
            block_inputs(2 * s + 1)
        if s >= 1:
            block_outputs(2 * s - 2)
            block_outputs(2 * s - 1)
            unpack(s - 1)
        if s < n_slab:
            scan(s)

    for b in range(SUBLANES):
        for s in range(n_slab):
            y_ref[:, b * w + s * LANES:b * w + (s + 1) * LANES] = usc[s, pl.ds(b, tt, stride=SUBLANES), :]

    if emit_state:
        @pl.when(i == pl.num_programs(2) - 1)
        def _():
            fr_ref[...] = car[...]
            fi_ref[...] = cai[...]


def _s5_scan(u_tm, prm, layer, s0, b, emit_state, tt):
    t = u_tm.shape[0]
    w = u_tm.shape[1] // b
    l4r, l4i = prm[:2]
    mats = prm[2:]
    n_p = mats[0].shape[2]
    ns = l4r.shape[-1]
    n_t = t // tt
    n_g = b // SUBLANES
    rows = tt // S5_BLOCK * SUBLANES
    zero_init = s0 is None
    tile = lambda g, d, i: i + d * (n_t - 1 - 2 * i)
    dir_spec = lambda shape: pl.BlockSpec((None, None) + shape,
                                          lambda g, d, i: (layer, d) + (0,) * len(shape))
    in_specs = [pl.BlockSpec((tt, SUBLANES * w), lambda g, d, i: (tile(g, d, i), g)),
                dir_spec((SUBLANES, ns)), dir_spec((SUBLANES, ns))]
    in_specs += [dir_spec(m.shape[2:]) for m in mats]
    args = [u_tm, l4r, l4i] + list(mats)
    state_spec = pl.BlockSpec((None, SUBLANES, ns), lambda g, d, i: (d, g, 0))
    if not zero_init:
        in_specs += [state_spec, state_spec]
        args += list(s0)
    out_specs = [pl.BlockSpec((None, tt, SUBLANES * w), lambda g, d, i: (d, tile(g, d, i), g))]
    out_shapes = [jax.ShapeDtypeStruct((N_DIR, t, b * w), F32)]
    if emit_state:
        out_specs += [state_spec, state_spec]
        out_shapes += [jax.ShapeDtypeStruct((N_DIR, b, ns), F32)] * 2
    res = pl.pallas_call(
        functools.partial(_s5_kernel, tt=tt, w=w, n_p=n_p, zero_init=zero_init,
                          emit_state=emit_state),
        grid=(n_g, N_DIR, n_t),
        in_specs=in_specs,
        out_specs=out_specs,
        out_shape=out_shapes,
        scratch_shapes=[pltpu.VMEM((w // LANES, tt * SUBLANES, LANES), F32),
                        pltpu.VMEM((n_p, rows, 2 * LANES), BF16),
                        pltpu.VMEM((rows, ns), F32), pltpu.VMEM((rows, ns), F32),
                        pltpu.VMEM((n_p, rows, 2 * LANES), F32),
                        pltpu.VMEM((SUBLANES, ns), F32), pltpu.VMEM((SUBLANES, ns), F32)],
        compiler_params=pltpu.CompilerParams(
            dimension_semantics=("parallel", "parallel", "arbitrary"), vmem_limit_bytes=VMEM_LIMIT),
        name="s5_scan",
    )(*args)
    if emit_state:
        return res[0], res[1], res[2]
    return res[0]


def _s5_params(lam_re, lam_im, log_step, b_re, b_im, c_re, c_im):
    depth, n_dir, g, p = lam_re.shape
    n = b_re.shape[-1]
    blk = S5_BLOCK
    gpt = 2 * LANES // p
    n_p = g // gpt
    tw = 2 * LANES
    assert gpt * n * blk == tw and n_dir == N_DIR
    hi = lax.Precision.HIGHEST
    step = jnp.exp(log_step)[..., None]
    a = lam_re * step
    th = lam_im * step
    k = jnp.arange(blk + 1, dtype=F32).reshape(-1, 1, 1, 1, 1)
    pw_re = jnp.exp(k * a) * jnp.cos(k * th)
    pw_im = jnp.exp(k * a) * jnp.sin(k * th)
    den = lam_re * lam_re + lam_im * lam_im
    fr = ((pw_re[1] - 1.0) * lam_re + pw_im[1] * lam_im) / den
    fi = (pw_im[1] * lam_re - (pw_re[1] - 1.0) * lam_im) / den
    bbr = fr[..., None] * b_re - fi[..., None] * b_im
    bbi = fr[..., None] * b_im + fi[..., None] * b_re
    lb_re = pw_re[..., None] * bbr - pw_im[..., None] * bbi
    lb_im = pw_re[..., None] * bbi + pw_im[..., None] * bbr
    offs = np.arange(blk)
    pos = np.stack([offs, blk - 1 - offs])

    def per_dir(arr, idx):
        return jnp.stack([arr[idx[d], :, d] for d in range(n_dir)], axis=idx.ndim)

    split = lambda m, ax: m.reshape(m.shape[:ax] + (n_p, gpt) + m.shape[ax + 1:])
    ridx, cidx = np.arange(tw)[:, None], np.arange(tw)[None, :]
    rep_state = jnp.asarray(np.arange(p)[:, None] == cidx % p, BF16)
    chan = lambda c: (c // (gpt * n)) * n + c % n
    rep_chan = jnp.asarray(np.arange(blk * n)[:, None] == chan(cidx), BF16)
    grp_of_chan = lambda c: (c // n) % gpt
    mask_z = jnp.asarray(grp_of_chan(ridx) == cidx // p, BF16)
    mask_o = jnp.asarray(ridx // p == grp_of_chan(cidx), BF16)
    mask_t = jnp.asarray(grp_of_chan(ridx) == grp_of_chan(cidx), BF16)

    def dense(compact, rep, mask):
        full = jnp.einsum("ldprk,kc->ldprc", compact.astype(BF16), rep, preferred_element_type=BF16)
        return full * mask

    wz = [dense(split(per_dir(m, blk - 1 - pos), 3).transpose(1, 2, 3, 0, 4, 6, 5)
                .reshape(depth, n_dir, n_p, tw, p), rep_state, mask_z) for m in (lb_re, lb_im)]
    pr = per_dir(pw_re, pos + 1)[:, :, :, :, None, :]
    pi = per_dir(pw_im, pos + 1)[:, :, :, :, None, :]
    wo = [dense(split(m, 3).transpose(1, 2, 3, 4, 6, 0, 5).reshape(depth, n_dir, n_p, tw, blk * n),
                rep_chan, mask_o) for m in (c_re * pr - c_im * pi, -(c_re * pi + c_im * pr))]
    kern = (jnp.einsum("ldgnp,mldgpq->mldgqn", c_re, lb_re[:blk], precision=hi)
            - jnp.einsum("ldgnp,mldgpq->mldgqn", c_im, lb_im[:blk], precision=hi))
    lag = pos[:, None, :] - pos[:, :, None]
    valid = jnp.asarray((lag >= 0).transpose(1, 2, 0), F32).reshape(blk, blk, 1, n_dir, 1, 1, 1)
    wt = dense(split(per_dir(kern, np.maximum(lag, 0)) * valid, 4).transpose(2, 3, 4, 0, 5, 6, 1, 7)
               .reshape(depth, n_dir, n_p, tw, blk * n), rep_chan, mask_t)
    bcast = lambda v: jnp.broadcast_to(v.reshape(depth, n_dir, 1, g * p),
                                       (depth, n_dir, SUBLANES, g * p))
    return [bcast(pw_re[blk]), bcast(pw_im[blk])] + wz + wo + [wt]


def _log_decay(z):
    return (jnp.minimum(z, 0.0) - jnp.log(1.0 + jnp.exp(-jnp.abs(z)))) * (1.0 / GATE_TAU)


def _split2(x):
    hi = x.astype(BF16)
    return hi, (x - hi.astype(F32)).astype(BF16)


def _gla_kernel(*refs, n_chunks, dk, dv, zero_init, emit_state, aliased_states, layer):
    dir_refs = (refs[0:4], refs[4:8])
    wup_ref, bg_ref = refs[8:10]
    rest = refs[10:]
    if not zero_init:
        s0_ref = rest[0]
        rest = rest[1:]
    if aliased_states:
        rest = rest[1:]
    if emit_state:
        of_ref, ob_ref, sf_ref, st, prep, dec = rest
    else:
        of_ref, ob_ref, st, prep, dec = rest
    o_refs = (of_ref, ob_ref)
    i = pl.program_id(1)
    scale = dk ** -0.5

    @pl.when(i == 0)
    def _():
        for d in range(N_DIR):
            for h in range(N_HEADS):
                if zero_init:
                    st[d, h] = jnp.zeros((dk, dv), F32)
                else:
                    st[d, h] = s0_ref[d, h]

    row = lax.broadcasted_iota(jnp.int32, (CHUNK, CHUNK), 0)
    col = lax.broadcasted_iota(jnp.int32, (CHUNK, CHUNK), 1)
    causal = (col <= row, col >= row)
    row2 = lax.broadcasted_iota(jnp.int32, (CHUNK, 2 * CHUNK), 0)
    col2 = lax.broadcasted_iota(jnp.int32, (CHUNK, 2 * CHUNK), 1) % CHUNK
    causal2 = (col2 <= row2, col2 >= row2)
    ones = tuple(jnp.where(c, 1.0, 0.0).astype(BF16) for c in causal)
    first_head = lax.broadcasted_iota(jnp.int32, (CHUNK, 2 * dk), 1) < dk

    dims = (((1,), (1,)), ((), ()))
    chunk_of = lambda d, cc: cc if d == 0 else n_chunks - 1 - cc
    rows_of = lambda d, cc: pl.ds(chunk_of(d, cc) * CHUNK, CHUNK)

    def gate_z(d, cc):
        glr_ref = dir_refs[d][3]
        return _dot(glr_ref[rows_of(d, cc), :], wup_ref[d]) + bg_ref[d]

    def gate_cum(d, g):
        return sum(_dot(ones[d], part) for part in _split2(g))

    def gate_store(slot, d, cc, g, cum):
        q_ref, k_ref = dir_refs[d][:2]
        rows = rows_of(d, cc)
        total = jnp.sum(g, axis=0, keepdims=True)
        prep[slot, d, 0] = (q_ref[rows, :] * scale * jnp.exp(cum)).astype(BF16)
        kk = k_ref[rows, :]
        prep[slot, d, 1] = (kk * jnp.exp(-cum)).astype(BF16)
        prep[slot, d, 2] = (kk * jnp.exp(total - cum)).astype(BF16)
        decay = jnp.exp(total)
        for h in range(N_HEADS):
            dec[slot, d, h] = jnp.broadcast_to(decay[:, ksl(h)], (dk, dk)).T

    pairs = [(d, h) for d in range(N_DIR) for h in range(N_HEADS)]
    ksl = lambda h: slice(h * dk, (h + 1) * dk)
    vsl = lambda h: slice(h * dv, (h + 1) * dv)

    for d in range(N_DIR):
        g0 = _log_decay(gate_z(d, 0))
        gate_store(0, d, 0, g0, gate_cum(d, g0))

    for cc in range(n_chunks):
        slot = cc % 2
        more = cc + 1 < n_chunks
        if more:
            z_next = [gate_z(d, cc + 1) for d in range(N_DIR)]
        sc = {}
        for d in range(N_DIR):
            for hp in range(N_HEADS // 2):
                ks2 = slice(2 * hp * dk, (2 * hp + 2) * dk)
                k2 = prep[slot, d, 1, :, ks2]
                none = jnp.zeros_like(k2)
                k_blocks = jnp.concatenate([jnp.where(first_head, k2, none),
                                            jnp.where(first_head, none, k2)], axis=0)
                s = lax.dot_general(prep[slot, d, 0, :, ks2], k_blocks, dims,
                                    preferred_element_type=F32)
                sc[d, hp] = jnp.where(causal2[d], s, 0.0).astype(BF16)
        if more:
            g_next = [_log_decay(z) for z in z_next]
        for d, h in pairs:
            v_ref = dir_refs[d][2]
            rows = rows_of(d, cc)
            vh = v_ref[rows, vsl(h)]
            none = jnp.zeros_like(vh)
            lhs = jnp.concatenate([prep[slot, d, 0, :, ksl(h)], sc[d, h // 2]], axis=1)
            rhs = jnp.concatenate([st[d, h].astype(BF16)] + ([vh, none] if h % 2 == 0 else [none, vh]),
                                  axis=0)
            o_refs[d][rows, vsl(h)] = _dot(lhs, rhs)
        if more:
            cum_next = [gate_cum(d, g_next[d]) for d in range(N_DIR)]
        for d, h in pairs:
            v_ref = dir_refs[d][2]
            vh = v_ref[rows_of(d, cc), vsl(h)]
            upd = lax.dot_general(prep[slot, d, 2, :, ksl(h)], vh, (((0,), (0,)), ((), ())),
                                  preferred_element_type=F32)
            decay = jnp.concatenate([dec[slot, d, h]] * (dv // dk), axis=1)
            st[d, h] = decay * st[d, h] + upd
        if more:
            for d in range(N_DIR):
                gate_store(1 - slot, d, cc + 1, g_next[d], cum_next[d])

    if emit_state:
        @pl.when(i == pl.num_programs(1) - 1)
        def _():
            if not aliased_states:
                sf_ref[...] = jnp.zeros_like(sf_ref)
            for d in range(N_DIR):
                for h in range(N_HEADS):
                    if aliased_states:
                        sf_ref[d, h] = st[d, h]
                    else:
                        sf_ref[layer, d, h] = st[d, h]


def _gla_scan(q, k, v, glr, wup, bg, layer, depth, s0, states, emit_state, tt):
    b, t, dkk = q.shape
    dvv = v.shape[-1]
    dk, dv = dkk // N_HEADS, dvv // N_HEADS
    n_t = t // tt
    zero_init = s0 is None
    fwd_spec = lambda n: pl.BlockSpec((None, tt, n), lambda bi, i: (bi, i, 0))
    bwd_spec = lambda n: pl.BlockSpec((None, tt, n), lambda bi, i: (bi, n_t - 1 - i, 0))
    state_spec = pl.BlockSpec((None, None, N_DIR, N_HEADS, dk, dv), lambda bi, i: (bi, layer, 0, 0, 0, 0))
    widths = (dkk, dkk, dvv, glr.shape[-1])
    in_specs = ([fwd_spec(n) for n in widths] + [bwd_spec(n) for n in widths]
                + [_const_spec(wup.shape), _const_spec(bg.shape)])
    args = [q, k, v, glr, q, k, v, glr, wup, bg]
    if not zero_init:
        in_specs.append(state_spec)
        args.append(s0)
    out_specs = [fwd_spec(dvv), bwd_spec(dvv)]
    out_shapes = [jax.ShapeDtypeStruct((b, t, dvv), F32)] * 2
    aliases = {}
    if emit_state:
        out_shapes.append(jax.ShapeDtypeStruct((b, depth, N_DIR, N_HEADS, dk, dv), F32))
        if states is None:
            out_specs.append(pl.BlockSpec((None, depth, N_DIR, N_HEADS, dk, dv),
                                          lambda bi, i: (bi, 0, 0, 0, 0, 0)))
        else:
            out_specs.append(state_spec)
            in_specs.append(pl.BlockSpec(memory_space=pl.ANY))
            args.append(states)
            aliases = {len(args) - 1: 2}
    return pl.pallas_call(
        functools.partial(_gla_kernel, n_chunks=tt // CHUNK, dk=dk, dv=dv, zero_init=zero_init,
                          emit_state=emit_state, aliased_states=bool(aliases), layer=layer),
        grid=(b, n_t),
        in_specs=in_specs,
        out_specs=out_specs,
        out_shape=out_shapes,
        input_output_aliases=aliases,
        scratch_shapes=[pltpu.VMEM((N_DIR, N_HEADS, dk, dv), F32),
                        pltpu.VMEM((2, N_DIR, 3, CHUNK, dkk), BF16),
                        pltpu.VMEM((2, N_DIR, N_HEADS, dk, dk), F32)],
        compiler_params=pltpu.CompilerParams(
            dimension_semantics=("parallel", "arbitrary"), vmem_limit_bytes=VMEM_LIMIT),
        name="gla_scan",
    )(*args)


def _gelu_tanh(x):
    return 0.5 * x * (1.0 + jnp.tanh(math.sqrt(2.0 / math.pi) * (x + 0.044715 * (x * x * x))))


def _mixmlp_kernel(*refs, d, dv, ff_block, final_norm):
    (x_ref, of_ref, ob_ref, r_ref, ga_ref, gb_ref, u_ref, yf_ref, yb_ref, mod_ref, gn_ref, d_ref,
     wglu_ref, bglu_ref, wpg_ref, wps_ref, wout_ref, g2_ref, w1_ref, w2_ref) = refs[:20]
    if final_norm:
        fg_ref, o_ref = refs[20:]
    else:
        o_ref = refs[20]
    o = of_ref[...] + ob_ref[...]
    r = r_ref[...]
    gn = gn_ref[...]
    parts = []
    for h in range(N_HEADS):
        vs = slice(h * dv, (h + 1) * dv)
        oh = o[:, vs]
        ms = jnp.mean(jnp.square(oh), axis=-1, keepdims=True)
        parts.append((oh * lax.rsqrt(ms + EPS) * gn * _silu(r[:, vs])).astype(BF16))
    pg = _dot(jnp.concatenate(parts, axis=-1), wpg_ref[...])

    y = _gelu_tanh(yf_ref[...] + yb_ref[...] + d_ref[...] * u_ref[...])
    y = y * jax.nn.sigmoid(_dot(y.astype(BF16), wglu_ref[...]) + bglu_ref[...])
    ps = _dot(y.astype(BF16), wps_ref[...])

    merged = jax.nn.sigmoid(ga_ref[...]) * pg + jax.nn.sigmoid(gb_ref[...]) * ps
    x = x_ref[...] + mod_ref[:, 2 * d:3 * d] * _dot(merged.astype(BF16), wout_ref[...])

    h = _rms_mod(x, g2_ref[...], mod_ref[:, 4 * d:5 * d], mod_ref[:, 3 * d:4 * d]).astype(BF16)
    n_ff = w1_ref.shape[1] // ff_block
    cols = lambda j: slice(j * ff_block, (j + 1) * ff_block)
    hidden = lambda j: jnp.square(jnp.maximum(_dot(h, w1_ref[:, cols(j)]), 0.0)).astype(BF16)
    a = hidden(0)
    acc = None
    for j in range(n_ff):
        a_next = hidden(j + 1) if j + 1 < n_ff else None
        part = _dot(a, w2_ref[cols(j), :])
        acc = part if acc is None else acc + part
        a = a_next
    x = x + mod_ref[:, 5 * d:6 * d] * acc
    if final_norm:
        ms = jnp.mean(jnp.square(x), axis=-1, keepdims=True)
        x = x * lax.rsqrt(ms + EPS) * fg_ref[...]
    o_ref[...] = x


def _mixmlp(x, o_f, o_b, r, ga, gb, u_tm, y, mod, p, final_g, tm):
    b, t, d = x.shape
    dvv = r.shape[-1]
    w = u_tm.shape[1] // b
    per_batch_mod = mod.shape[0] != 1
    final_norm = final_g is not None
    row_spec = lambda n: pl.BlockSpec((None, tm, n), lambda bi, i: (bi, i, 0))
    mod_idx = (lambda bi, i: (bi, 0, 0)) if per_batch_mod else (lambda bi, i: (0, 0, 0))
    weights = [p["w_glu"], p["b_glu"].reshape(1, -1), p["w_proj_gla"], p["w_proj_s5"], p["w_out"],
               p["norm2_g"].reshape(1, d), p["w_ff1"], p["w_ff2"]]
    if final_norm:
        weights.append(final_g.reshape(1, d))
    in_specs = [
        row_spec(d), row_spec(dvv), row_spec(dvv), row_spec(dvv), row_spec(d), row_spec(d),
        pl.BlockSpec((tm, w), lambda bi, i: (i, bi)),
        pl.BlockSpec((None, tm, w), lambda bi, i: (0, i, bi)),
        pl.BlockSpec((None, tm, w), lambda bi, i: (1, i, bi)),
        pl.BlockSpec((None, 1, mod.shape[2]), mod_idx),
        _const_spec((1, dvv // N_HEADS)), _const_spec((1, w)),
    ] + [_const_spec(a.shape) for a in weights]
    return pl.pallas_call(
        functools.partial(_mixmlp_kernel, d=d, dv=dvv // N_HEADS, ff_block=1024,
                          final_norm=final_norm),
        grid=(b, t // tm),
        in_specs=in_specs,
        out_specs=row_spec(d),
        out_shape=jax.ShapeDtypeStruct((b, t, d), F32),
        compiler_params=pltpu.CompilerParams(
            dimension_semantics=("parallel", "parallel"), vmem_limit_bytes=VMEM_LIMIT),
        name="mixmlp",
    )(x, o_f, o_b, r, ga, gb, u_tm, y, y, mod, p["gla_norm_g"].reshape(1, -1),
      p["s5_d"].reshape(1, -1), *weights)


def _grid_pos_tables(n_tokens, dim):
    quarter = dim // 4
    omega = 1.0 / (POS_BASE ** (jnp.arange(quarter, dtype=F32) / quarter))
    ar = jnp.arange(n_tokens // GRID_W, dtype=F32)[:, None] * omega
    ac = jnp.arange(GRID_W, dtype=F32)[:, None] * omega
    return (jnp.concatenate([jnp.sin(ar), jnp.cos(ar)], axis=-1),
            jnp.concatenate([jnp.sin(ac), jnp.cos(ac)], axis=-1))


def _split_w_in(w_in, dk_all, dv_all, s5w, d):
    splits = (dk_all, dk_all, dv_all, dv_all, N_DIR * GATE_RANK, s5w, d, d)
    idx = np.cumsum((0,) + splits)
    parts = [w_in[:, idx[j]:idx[j + 1]] for j in range(len(splits))]
    parts[4] = jnp.pad(parts[4], ((0, 0), (0, LANES - splits[4])))
    return [p.astype(BF16) for p in parts]


def _stream(x, pos, mods, gla_s0, s5_s0, layers, s5_prm, final_g, emit_state, tm):
    b, t, d = x.shape
    gla_states, s5_re, s5_im = None, [], []
    for l, p in enumerate(layers):
        res = _inproj(x, pos if l == 0 else None, mods[l], p["norm1_g"], p["w_in"], min(t, 2 * tm))
        if l == 0 and pos is not None:
            x = res[0]
            res = res[1:]
        q, k, v, r, glr, u_tm, ga, gb = res
        s5_out = _s5_scan(u_tm, s5_prm, l, None if s5_s0 is None else s5_s0[l], b, emit_state,
                          tt=min(t, 256))
        gla_out = _gla_scan(q, k, v, glr, p["wup"], p["bg"], l, len(layers), gla_s0, gla_states,
                            emit_state, tt=min(t, 512))
        o_f, o_b = gla_out[:2]
        if emit_state:
            y, f_re, f_im = s5_out
            gla_states = gla_out[2]
            s5_re.append(f_re)
            s5_im.append(f_im)
        else:
            y = s5_out
        last = l == len(layers) - 1
        x = _mixmlp(x, o_f, o_b, r, ga, gb, u_tm, y, mods[l], p, final_g if last else None, tm)
    return x, gla_states, s5_re, s5_im


def kernel(x_prompt, x_sample, c, cache_gla_state, state_s5_re, state_s5_im, c_ctx, w_mod, b_mod,
           norm1_g, w_in, w_gate_up, b_gate, gla_norm_g, w_proj_gla, s5_lam_re, s5_lam_im,
           s5_log_step, s5_b_re, s5_b_im, s5_c_re, s5_c_im, s5_d, w_glu, b_glu, w_proj_s5, w_out,
           norm2_g, w_ff1, w_ff2, final_g):
    depth = w_in.shape[0]
    nb, seq, d = x_prompt.shape
    db, dseq, _ = x_sample.shape
    dk_all = w_gate_up.shape[-1]
    dv_all = w_proj_gla.shape[1]
    s5w = s5_d.shape[-1]
    n_groups, n_state = s5_lam_re.shape[2], s5_lam_re.shape[3]
    dk, dv = dk_all // N_HEADS, dv_all // N_HEADS
    assert nb % SUBLANES == 0 and db % SUBLANES == 0

    n_cond = -(-(db + 1) // SUBLANES) * SUBLANES
    conds = jnp.concatenate([c, c_ctx[None], jnp.zeros((n_cond - db - 1, d), F32)], axis=0)
    mod_all = _modulation(conds, w_mod, b_mod)
    mods_lat = [mod_all[l, :db].reshape(db, 1, N_MOD * d) for l in range(depth)]
    mods_ctx = [mod_all[l, db:db + 1].reshape(1, 1, N_MOD * d) for l in range(depth)]

    s5_prm = _s5_params(s5_lam_re, s5_lam_im, s5_log_step, s5_b_re, s5_b_im, s5_c_re, s5_c_im)
    wup = jnp.stack([jnp.pad(w_gate_up[:, dd], ((0, 0), (dd * GATE_RANK, LANES - (dd + 1) * GATE_RANK),
                                                (0, 0))) for dd in range(N_DIR)], axis=1).astype(BF16)
    layers = []
    for l in range(depth):
        layers.append(dict(
            norm1_g=norm1_g[l], w_in=_split_w_in(w_in[l], dk_all, dv_all, s5w, d),
            wup=wup[l], bg=b_gate[l].reshape(N_DIR, 1, dk_all),
            gla_norm_g=gla_norm_g[l], w_proj_gla=w_proj_gla[l].astype(BF16),
            s5_d=s5_d[l], w_glu=w_glu[l].astype(BF16), b_glu=b_glu[l],
            w_proj_s5=w_proj_s5[l].astype(BF16), w_out=w_out[l].astype(BF16),
            norm2_g=norm2_g[l], w_ff1=w_ff1[l].astype(BF16), w_ff2=w_ff2[l].astype(BF16)))

    y_prompt, new_gla_state, s5_re, s5_im = _stream(
        x_prompt, None, mods_ctx, None, None, layers, s5_prm, final_g, True, tm=min(seq, 256))
    to_state = lambda a: a.transpose(1, 0, 2).reshape(nb, N_DIR, n_groups, n_state)
    new_s5_re = jnp.stack([to_state(a) for a in s5_re], axis=1)
    new_s5_im = jnp.stack([to_state(a) for a in s5_im], axis=1)

    pos = _grid_pos_tables(dseq, d)
    from_state = lambda a: a.reshape(db, N_DIR, n_groups * n_state).transpose(1, 0, 2)
    s5_s0 = [(from_state(state_s5_re[:, l]), from_state(state_s5_im[:, l])) for l in range(depth)]
    y_sample, _, _, _ = _stream(
        x_sample, pos, mods_lat, cache_gla_state, s5_s0, layers, s5_prm, final_g, False,
        tm=min(dseq, 256))

    return (y_prompt, y_sample, new_gla_state, new_s5_re, new_s5_im)
```

```python
import functools
import math

import jax
import jax.numpy as jnp
import numpy as np
from jax import lax
from jax.experimental import pallas as pl
from jax.experimental.pallas import tpu as pltpu

N_DIR = 2
N_HEADS = 4
GATE_RANK = 16
GATE_TAU = 16.0
CHUNK = 64
S5_GROUP = 16
S5_STATE = 64
N_MOD = 6
EPS = 1e-6
GRID_W = 64
POS_BASE = 10000.0

LANES = 128
SUBLANES = 8
VMEM_LIMIT = 56 * 1024 * 1024

F32 = jnp.float32
BF16 = jnp.bfloat16


def _const_spec(shape):
    nd = len(shape)
    return pl.BlockSpec(shape, lambda *_: (0,) * nd, pipeline_mode=pl.Buffered(1))


def _dot(a, b):
    return jnp.dot(a, b, preferred_element_type=F32)


def _rms_mod(x, g, scale, shift):
    ms = jnp.mean(jnp.square(x), axis=-1, keepdims=True)
    return x * lax.rsqrt(ms + EPS) * g * (1.0 + scale) + shift


def _silu(x):
    return x * jax.nn.sigmoid(x)


def _mod_kernel(c_ref, w_ref, b_ref, o_ref):
    a = _silu(c_ref[...]).astype(BF16)
    o_ref[...] = _dot(a, w_ref[...].astype(BF16)) + b_ref[...]


def _modulation(conds, w_mod, b_mod):
    depth, d, dm = w_mod.shape
    rows = conds.shape[0]
    tn = 1024
    return pl.pallas_call(
        _mod_kernel,
        grid=(depth, dm // tn),
        in_specs=[
            pl.BlockSpec((rows, d), lambda l, j: (0, 0)),
            pl.BlockSpec((None, d, tn), lambda l, j: (l, 0, j)),
            pl.BlockSpec((None, 1, tn), lambda l, j: (l, 0, j)),
        ],
        out_specs=pl.BlockSpec((None, rows, tn), lambda l, j: (l, 0, j)),
        out_shape=jax.ShapeDtypeStruct((depth, rows, dm), F32),
        compiler_params=pltpu.CompilerParams(vmem_limit_bytes=VMEM_LIMIT),
        name="modulation",
    )(conds, w_mod, b_mod.reshape(depth, 1, dm))


def _inproj_kernel(*refs, d, add_pos, n_sub):
    if add_pos:
        x_ref, prow_ref, pcol_ref, mod_ref, g_ref = refs[:5]
        rest = refs[5:]
    else:
        x_ref, mod_ref, g_ref = refs[:3]
        rest = refs[3:]
    w_refs = rest[:8]
    out_refs = rest[8:]
    shift = mod_ref[:, 0:d]
    scale = mod_ref[:, d:2 * d]
    sub = x_ref.shape[0] // n_sub
    for si in range(n_sub):
        rows = slice(si * sub, (si + 1) * sub)
        x = x_ref[rows, :]
        proj_refs = out_refs
        if add_pos:
            n_rows = sub // GRID_W
            prow = prow_ref[si * n_rows:(si + 1) * n_rows, :]
            prow = jnp.concatenate([jnp.broadcast_to(prow[r:r + 1], (GRID_W, prow.shape[1]))
                                    for r in range(n_rows)], axis=0)
            pcol = jnp.concatenate([pcol_ref[...]] * n_rows, axis=0)
            x = x + jnp.concatenate([prow, pcol], axis=1)
            out_refs[0][rows, :] = x
            proj_refs = out_refs[1:]
        h = _rms_mod(x, g_ref[...], scale, shift).astype(BF16)
        for w_ref, o_ref in zip(w_refs, proj_refs):
            o_ref[rows, :] = _dot(h, w_ref[...]).astype(o_ref.dtype)


def _inproj(x, pos, mod, g, weights, layer, tm):
    b, t, d = x.shape
    add_pos = pos is not None
    per_batch_mod = mod.shape[0] != 1
    widths = [w.shape[2] for w in weights]
    row_spec = lambda n: pl.BlockSpec((None, tm, n), lambda bi, i: (bi, i, 0))
    in_specs = [row_spec(d)]
    args = [x]
    if add_pos:
        assert tm % (2 * GRID_W) == 0
        in_specs += [pl.BlockSpec((tm // GRID_W, d // 2), lambda bi, i: (i, 0)),
                     _const_spec(pos[1].shape)]
        args += list(pos)
    mod_idx = (lambda bi, i: (bi, 0, 0)) if per_batch_mod else (lambda bi, i: (0, 0, 0))
    in_specs += [pl.BlockSpec((None, 1, mod.shape[2]), mod_idx), _const_spec((1, d))]
    args += [mod, g.reshape(1, d)]
    in_specs += [pl.BlockSpec((None,) + w.shape[1:], lambda bi, i: (layer, 0, 0),
                              pipeline_mode=pl.Buffered(1)) for w in weights]
    args += list(weights)

    out_specs, out_shapes = [], []
    if add_pos:
        out_specs.append(row_spec(d))
        out_shapes.append(jax.ShapeDtypeStruct((b, t, d), F32))
    for idx, n in enumerate(widths):
        if idx == 5:
            out_specs.append(pl.BlockSpec((tm, n), lambda bi, i: (i, bi)))
            out_shapes.append(jax.ShapeDtypeStruct((t, b * n), F32))
        else:
            out_specs.append(row_spec(n))
            out_shapes.append(jax.ShapeDtypeStruct((b, t, n), BF16 if idx in (2, 4) else F32))
    return pl.pallas_call(
        functools.partial(_inproj_kernel, d=d, add_pos=add_pos, n_sub=2),
        grid=(b, t // tm),
        in_specs=in_specs,
        out_specs=out_specs,
        out_shape=out_shapes,
        compiler_params=pltpu.CompilerParams(
            dimension_semantics=("parallel", "parallel"), vmem_limit_bytes=VMEM_LIMIT),
        name="inproj",
    )(*args)


S5_BLOCK = 4


def _s5_kernel(*refs, tt, w, n_p, zero_init, emit_state):
    u_ref, l4r_ref, l4i_ref, wzr_ref, wzi_ref, wor_ref, woi_ref, wt_ref = refs[:8]
    rest = refs[8:]
    if not zero_init:
        s0r_ref, s0i_ref = rest[:2]
        rest = rest[2:]
    if emit_state:
        y_ref, fr_ref, fi_ref = rest[:3]
        rest = rest[3:]
    else:
        y_ref = rest[0]
        rest = rest[1:]
    usc, xsc, zre, zim, ypk, car, cai = rest
    direction = pl.program_id(1)
    i = pl.program_id(2)
    n_slab = w // LANES
    n_blk = tt // S5_BLOCK
    half = LANES // 2
    tw = 2 * LANES
    grp = S5_BLOCK * SUBLANES

    @pl.when(i == 0)
    def _():
        if zero_init:
            car[...] = jnp.zeros_like(car)
            cai[...] = jnp.zeros_like(cai)
        else:
            car[...] = s0r_ref[...]
            cai[...] = s0i_ref[...]

    for b in range(SUBLANES):
        for s in range(n_slab):
            usc[s, pl.ds(b, tt, stride=SUBLANES), :] = u_ref[:, b * w + s * LANES:b * w + (s + 1) * LANES]

    low = lax.broadcasted_iota(jnp.int32, (2 * SUBLANES, LANES), 1) < half
    swap = lambda a: pltpu.roll(a, half, 1)

    def pack(s):
        for r2 in range(n_blk // 2):
            a = [jnp.concatenate([usc[s, pl.ds(base + j * SUBLANES, SUBLANES), :]
                                  for base in (2 * r2 * grp, (2 * r2 + 1) * grp)], axis=0)
                 for j in range(S5_BLOCK)]
            asw = [swap(v) for v in a]
            for h in range(2):
                pick = lambda j, slot_half: a[j] if h == slot_half else asw[j]
                cols = [jnp.where(low, pick(2 * c, 0), pick(2 * c + 1, 1)) for c in range(2)]
                xsc[2 * s + h, pl.ds(r2 * 2 * SUBLANES, 2 * SUBLANES), :] = (
                    jnp.concatenate(cols, axis=1).astype(BF16))

    def block_inputs(p):
        x = xsc[p]
        zre[:, p * tw:(p + 1) * tw] = _dot(x, wzr_ref[p])
        zim[:, p * tw:(p + 1) * tw] = _dot(x, wzi_ref[p])

    def scan(s):
        ls = slice(s * 2 * tw, (s + 1) * 2 * tw)
        lr = l4r_ref[:, ls]
        li = l4i_ref[:, ls]
        cr, ci = car[:, ls], cai[:, ls]
        for j in range(n_blk):
            r = j + direction * (n_blk - 1 - 2 * j)
            r0 = pl.multiple_of(r * SUBLANES, SUBLANES)
            zr = zre[pl.ds(r0, SUBLANES), ls]
            zi = zim[pl.ds(r0, SUBLANES), ls]
            zre[pl.ds(r0, SUBLANES), ls] = cr
            zim[pl.ds(r0, SUBLANES), ls] = ci
            cr, ci = lr * cr - li * ci + zr, lr * ci + li * cr + zi
        car[:, ls] = cr
        cai[:, ls] = ci

    def block_outputs(p):
        ypk[p] = (_dot(zre[:, p * tw:(p + 1) * tw].astype(BF16), wor_ref[p])
                  + _dot(zim[:, p * tw:(p + 1) * tw].astype(BF16), woi_ref[p])
                  + _dot(xsc[p], wt_ref[p]))

    def unpack(s):
        for r2 in range(n_blk // 2):
            rows = pl.ds(r2 * 2 * SUBLANES, 2 * SUBLANES)
            for j in range(S5_BLOCK):
                c, slot_half = divmod(j, 2)
                left = ypk[2 * s, rows, c * LANES:(c + 1) * LANES]
                right = ypk[2 * s + 1, rows, c * LANES:(c + 1) * LANES]
                v = jnp.where(low, left if slot_half == 0 else swap(left),
                              right if slot_half == 1 else swap(right))
                usc[s, pl.ds(2 * r2 * grp + j * SUBLANES, SUBLANES), :] = v[:SUBLANES]
                usc[s, pl.ds((2 * r2 + 1) * grp + j * SUBLANES, SUBLANES), :] = v[SUBLANES:]

    for s in range(n_slab + 1):
        if s < n_slab:
            pack(s)
            block_inputs(2 * s)
            block_inputs(2 * s + 1)
        if s >= 1:
            block_outputs(2 * s - 2)
            block_outputs(2 * s - 1)
            unpack(s - 1)
        if s < n_slab:
            scan(s)

    for b in range(SUBLANES):
        for s in range(n_slab):
            y_ref[:, b * w + s * LANES:b * w + (s + 1) * LANES] = usc[s, pl.ds(b, tt, stride=SUBLANES), :]

    if emit_state:
        @pl.when(i == pl.num_programs(2) - 1)
        def _():
            fr_ref[...] = car[...]
            fi_ref[...] = cai[...]


def _s5_scan(u_tm, prm, layer, s0, b, emit_state, tt):
    t = u_tm.shape[0]
    w = u_tm.shape[1] // b
    l4r, l4i = prm[:2]
    mats = prm[2:]
    n_p = mats[0].shape[2]
    ns = l4r.shape[-1]
    n_t = t // tt
    n_g = b // SUBLANES
    rows = tt // S5_BLOCK * SUBLANES
    zero_init = s0 is None
    tile = lambda g, d, i: i + d * (n_t - 1 - 2 * i)
    dir_spec = lambda shape: pl.BlockSpec((None, None) + shape,
                                          lambda g, d, i: (layer, d) + (0,) * len(shape))
    in_specs = [pl.BlockSpec((tt, SUBLANES * w), lambda g, d, i: (tile(g, d, i), g)),
                dir_spec((SUBLANES, ns)), dir_spec((SUBLANES, ns))]
    in_specs += [dir_spec(m.shape[2:]) for m in mats]
    args = [u_tm, l4r, l4i] + list(mats)
    state_spec = pl.BlockSpec((None, SUBLANES, ns), lambda g, d, i: (d, g, 0))
    if not zero_init:
        in_specs += [state_spec, state_spec]
        args += list(s0)
    out_specs = [pl.BlockSpec((None, tt, SUBLANES * w), lambda g, d, i: (d, tile(g, d, i), g))]
    out_shapes = [jax.ShapeDtypeStruct((N_DIR, t, b * w), F32)]
    if emit_state:
        out_specs += [state_spec, state_spec]
        out_shapes += [jax.ShapeDtypeStruct((N_DIR, b, ns), F32)] * 2
    res = pl.pallas_call(
        functools.partial(_s5_kernel, tt=tt, w=w, n_p=n_p, zero_init=zero_init,
                          emit_state=emit_state),
        grid=(n_g, N_DIR, n_t),
        in_specs=in_specs,
        out_specs=out_specs,
        out_shape=out_shapes,
        scratch_shapes=[pltpu.VMEM((w // LANES, tt * SUBLANES, LANES), F32),
                        pltpu.VMEM((n_p, rows, 2 * LANES), BF16),
                        pltpu.VMEM((rows, ns), F32), pltpu.VMEM((rows, ns), F32),
                        pltpu.VMEM((n_p, rows, 2 * LANES), F32),
                        pltpu.VMEM((SUBLANES, ns), F32), pltpu.VMEM((SUBLANES, ns), F32)],
        compiler_params=pltpu.CompilerParams(
            dimension_semantics=("parallel", "parallel", "arbitrary"), vmem_limit_bytes=VMEM_LIMIT),
        name="s5_scan",
    )(*args)
    if emit_state:
        return res[0], res[1], res[2]
    return res[0]


def _s5_params(lam_re, lam_im, log_step, b_re, b_im, c_re, c_im):
    depth, n_dir, g, p = lam_re.shape
    n = b_re.shape[-1]
    blk = S5_BLOCK
    gpt = 2 * LANES // p
    n_p = g // gpt
    tw = 2 * LANES
    assert gpt * n * blk == tw and n_dir == N_DIR
    hi = lax.Precision.HIGHEST
    step = jnp.exp(log_step)[..., None]
    a = lam_re * step
    th = lam_im * step
    k = jnp.arange(blk + 1, dtype=F32).reshape(-1, 1, 1, 1, 1)
    pw_re = jnp.exp(k * a) * jnp.cos(k * th)
    pw_im = jnp.exp(k * a) * jnp.sin(k * th)
    den = lam_re * lam_re + lam_im * lam_im
    fr = ((pw_re[1] - 1.0) * lam_re + pw_im[1] * lam_im) / den
    fi = (pw_im[1] * lam_re - (pw_re[1] - 1.0) * lam_im) / den
    bbr = fr[..., None] * b_re - fi[..., None] * b_im
    bbi = fr[..., None] * b_im + fi[..., None] * b_re
    lb_re = pw_re[..., None] * bbr - pw_im[..., None] * bbi
    lb_im = pw_re[..., None] * bbi + pw_im[..., None] * bbr
    offs = np.arange(blk)
    pos = np.stack([offs, blk - 1 - offs])

    def per_dir(arr, idx):
        return jnp.stack([arr[idx[d], :, d] for d in range(n_dir)], axis=idx.ndim)

    split = lambda m, ax: m.reshape(m.shape[:ax] + (n_p, gpt) + m.shape[ax + 1:])
    ridx, cidx = np.arange(tw)[:, None], np.arange(tw)[None, :]
    rep_state = jnp.asarray(np.arange(p)[:, None] == cidx % p, BF16)
    chan = lambda c: (c // (gpt * n)) * n + c % n
    rep_chan = jnp.asarray(np.arange(blk * n)[:, None] == chan(cidx), BF16)
    grp_of_chan = lambda c: (c // n) % gpt
    mask_z = jnp.asarray(grp_of_chan(ridx) == cidx // p, BF16)
    mask_o = jnp.asarray(ridx // p == grp_of_chan(cidx), BF16)
    mask_t = jnp.asarray(grp_of_chan(ridx) == grp_of_chan(cidx), BF16)

    def dense(compact, rep, mask):
        full = jnp.einsum("ldprk,kc->ldprc", compact.astype(BF16), rep, preferred_element_type=BF16)
        return full * mask

    wz = [dense(split(per_dir(m, blk - 1 - pos), 3).transpose(1, 2, 3, 0, 4, 6, 5)
                .reshape(depth, n_dir, n_p, tw, p), rep_state, mask_z) for m in (lb_re, lb_im)]
    pr = per_dir(pw_re, pos + 1)[:, :, :, :, None, :]
    pi = per_dir(pw_im, pos + 1)[:, :, :, :, None, :]
    wo = [dense(split(m, 3).transpose(1, 2, 3, 4, 6, 0, 5).reshape(depth, n_dir, n_p, tw, blk * n),
                rep_chan, mask_o) for m in (c_re * pr - c_im * pi, -(c_re * pi + c_im * pr))]
    kern = (jnp.einsum("ldgnp,mldgpq->mldgqn", c_re, lb_re[:blk], precision=hi)
            - jnp.einsum("ldgnp,mldgpq->mldgqn", c_im, lb_im[:blk], precision=hi))
    lag = pos[:, None, :] - pos[:, :, None]
    valid = jnp.asarray((lag >= 0).transpose(1, 2, 0), F32).reshape(blk, blk, 1, n_dir, 1, 1, 1)
    wt = dense(split(per_dir(kern, np.maximum(lag, 0)) * valid, 4).transpose(2, 3, 4, 0, 5, 6, 1, 7)
               .reshape(depth, n_dir, n_p, tw, blk * n), rep_chan, mask_t)
    bcast = lambda v: jnp.broadcast_to(v.reshape(depth, n_dir, 1, g * p),
                                       (depth, n_dir, SUBLANES, g * p))
    return [bcast(pw_re[blk]), bcast(pw_im[blk])] + wz + wo + [wt]


def _log_decay(z):
    return (jnp.minimum(z, 0.0) - jnp.log(1.0 + jnp.exp(-jnp.abs(z)))) * (1.0 / GATE_TAU)


def _split2(x):
    hi = x.astype(BF16)
    return hi, (x - hi.astype(F32)).astype(BF16)


def _gla_kernel(*refs, n_chunks, dk, dv, zero_init, emit_state, aliased_states, layer):
    dir_refs = (refs[0:4], refs[4:8])
    wup_ref, bg_ref = refs[8:10]
    rest = refs[10:]
    if not zero_init:
        s0_ref = rest[0]
        rest = rest[1:]
    if aliased_states:
        rest = rest[1:]
    if emit_state:
        of_ref, ob_ref, sf_ref, st, prep, dec = rest
    else:
        of_ref, ob_ref, st, prep, dec = rest
    o_refs = (of_ref, ob_ref)
    i = pl.program_id(1)
    scale = dk ** -0.5

    @pl.when(i == 0)
    def _():
        for d in range(N_DIR):
            for h in range(N_HEADS):
                if zero_init:
                    st[d, h] = jnp.zeros((dk, dv), F32)
                else:
                    st[d, h] = s0_ref[d, h]

    row = lax.broadcasted_iota(jnp.int32, (CHUNK, CHUNK), 0)
    col = lax.broadcasted_iota(jnp.int32, (CHUNK, CHUNK), 1)
    causal = (col <= row, col >= row)
    row2 = lax.broadcasted_iota(jnp.int32, (CHUNK, 2 * CHUNK), 0)
    col2 = lax.broadcasted_iota(jnp.int32, (CHUNK, 2 * CHUNK), 1) % CHUNK
    causal2 = (col2 <= row2, col2 >= row2)
    ones = tuple(jnp.where(c, 1.0, 0.0).astype(BF16) for c in causal)
    first_head = lax.broadcasted_iota(jnp.int32, (CHUNK, 2 * dk), 1) < dk

    dims = (((1,), (1,)), ((), ()))
    chunk_of = lambda d, cc: cc if d == 0 else n_chunks - 1 - cc
    rows_of = lambda d, cc: pl.ds(chunk_of(d, cc) * CHUNK, CHUNK)

    def gate_z(d, cc):
        glr_ref = dir_refs[d][3]
        return _dot(glr_ref[rows_of(d, cc), :], wup_ref[d]) + bg_ref[d]

    def gate_cum(d, g):
        return sum(_dot(ones[d], part) for part in _split2(g))

    def gate_store(slot, d, cc, g, cum):
        q_ref, k_ref = dir_refs[d][:2]
        rows = rows_of(d, cc)
        total = jnp.sum(g, axis=0, keepdims=True)
        prep[slot, d, 0] = (q_ref[rows, :] * scale * jnp.exp(cum)).astype(BF16)
        kk = k_ref[rows, :]
        prep[slot, d, 1] = (kk * jnp.exp(-cum)).astype(BF16)
        prep[slot, d, 2] = (kk * jnp.exp(total - cum)).astype(BF16)
        decay = jnp.exp(total)
        for h in range(N_HEADS):
            dec[slot, d, h] = jnp.broadcast_to(decay[:, ksl(h)], (dk, dk)).T

    pairs = [(d, h) for d in range(N_DIR) for h in range(N_HEADS)]
    ksl = lambda h: slice(h * dk, (h + 1) * dk)
    vsl = lambda h: slice(h * dv, (h + 1) * dv)

    for d in range(N_DIR):
        g0 = _log_decay(gate_z(d, 0))
        gate_store(0, d, 0, g0, gate_cum(d, g0))

    for cc in range(n_chunks):
        slot = cc % 2
        more = cc + 1 < n_chunks
        if more:
            z_next = [gate_z(d, cc + 1) for d in range(N_DIR)]
        sc = {}
        for d in range(N_DIR):
            for hp in range(N_HEADS // 2):
                ks2 = slice(2 * hp * dk, (2 * hp + 2) * dk)
                k2 = prep[slot, d, 1, :, ks2]
                none = jnp.zeros_like(k2)
                k_blocks = jnp.concatenate([jnp.where(first_head, k2, none),
                                            jnp.where(first_head, none, k2)], axis=0)
                s = lax.dot_general(prep[slot, d, 0, :, ks2], k_blocks, dims,
                                    preferred_element_type=F32)
                sc[d, hp] = jnp.where(causal2[d], s, 0.0).astype(BF16)
        if more:
            g_next = [_log_decay(z) for z in z_next]
        for d, h in pairs:
            v_ref = dir_refs[d][2]
            rows = rows_of(d, cc)
            vh = v_ref[rows, vsl(h)]
            none = jnp.zeros_like(vh)
            lhs = jnp.concatenate([prep[slot, d, 0, :, ksl(h)], sc[d, h // 2]], axis=1)
            rhs = jnp.concatenate([st[d, h].astype(BF16)] + ([vh, none] if h % 2 == 0 else [none, vh]),
                                  axis=0)
            o_refs[d][rows, vsl(h)] = _dot(lhs, rhs)
        if more:
            cum_next = [gate_cum(d, g_next[d]) for d in range(N_DIR)]
        for d, h in pairs:
            v_ref = dir_refs[d][2]
            vh = v_ref[rows_of(d, cc), vsl(h)]
            upd = lax.dot_general(prep[slot, d, 2, :, ksl(h)], vh, (((0,), (0,)), ((), ())),
                                  preferred_element_type=F32)
            decay = jnp.concatenate([dec[slot, d, h]] * (dv // dk), axis=1)
            st[d, h] = decay * st[d, h] + upd
        if more:
            for d in range(N_DIR):
                gate_store(1 - slot, d, cc + 1, g_next[d], cum_next[d])

    if emit_state:
        @pl.when(i == pl.num_programs(1) - 1)
        def _():
            if not aliased_states:
                sf_ref[...] = jnp.zeros_like(sf_ref)
            for d in range(N_DIR):
                for h in range(N_HEADS):
                    if aliased_states:
                        sf_ref[d, h] = st[d, h]
                    else:
                        sf_ref[layer, d, h] = st[d, h]


def _gla_scan(q, k, v, glr, wup, bg, layer, depth, s0, states, emit_state, tt):
    b, t, dkk = q.shape
    dvv = v.shape[-1]
    dk, dv = dkk // N_HEADS, dvv // N_HEADS
    n_t = t // tt
    zero_init = s0 is None
    fwd_spec = lambda n: pl.BlockSpec((None, tt, n), lambda bi, i: (bi, i, 0))
    bwd_spec = lambda n: pl.BlockSpec((None, tt, n), lambda bi, i: (bi, n_t - 1 - i, 0))
    state_spec = pl.BlockSpec((None, None, N_DIR, N_HEADS, dk, dv), lambda bi, i: (bi, layer, 0, 0, 0, 0))
    widths = (dkk, dkk, dvv, glr.shape[-1])
    in_specs = ([fwd_spec(n) for n in widths] + [bwd_spec(n) for n in widths]
                + [_const_spec(wup.shape), _const_spec(bg.shape)])
    args = [q, k, v, glr, q, k, v, glr, wup, bg]
    if not zero_init:
        in_specs.append(state_spec)
        args.append(s0)
    out_specs = [fwd_spec(dvv), bwd_spec(dvv)]
    out_shapes = [jax.ShapeDtypeStruct((b, t, dvv), F32)] * 2
    aliases = {}
    if emit_state:
        out_shapes.append(jax.ShapeDtypeStruct((b, depth, N_DIR, N_HEADS, dk, dv), F32))
        if states is None:
            out_specs.append(pl.BlockSpec((None, depth, N_DIR, N_HEADS, dk, dv),
                                          lambda bi, i: (bi, 0, 0, 0, 0, 0)))
        else:
            out_specs.append(state_spec)
            in_specs.append(pl.BlockSpec(memory_space=pl.ANY))
            args.append(states)
            aliases = {len(args) - 1: 2}
    return pl.pallas_call(
        functools.partial(_gla_kernel, n_chunks=tt // CHUNK, dk=dk, dv=dv, zero_init=zero_init,
                          emit_state=emit_state, aliased_states=bool(aliases), layer=layer),
        grid=(b, n_t),
        in_specs=in_specs,
        out_specs=out_specs,
        out_shape=out_shapes,
        input_output_aliases=aliases,
        scratch_shapes=[pltpu.VMEM((N_DIR, N_HEADS, dk, dv), F32),
                        pltpu.VMEM((2, N_DIR, 3, CHUNK, dkk), BF16),
                        pltpu.VMEM((2, N_DIR, N_HEADS, dk, dk), F32)],
        compiler_params=pltpu.CompilerParams(
            dimension_semantics=("parallel", "arbitrary"), vmem_limit_bytes=VMEM_LIMIT),
        name="gla_scan",
    )(*args)


def _gelu_tanh(x):
    return 0.5 * x * (1.0 + jnp.tanh(math.sqrt(2.0 / math.pi) * (x + 0.044715 * (x * x * x))))


def _mixmlp_kernel(*refs, d, dv, ff_block, final_norm):
    (x_ref, of_ref, ob_ref, r_ref, ga_ref, gb_ref, u_ref, yf_ref, yb_ref, mod_ref, gn_ref, d_ref,
     wglu_ref, bglu_ref, wpg_ref, wps_ref, wout_ref, g2_ref, w1_ref, w2_ref) = refs[:20]
    if final_norm:
        fg_ref, o_ref = refs[20:]
    else:
        o_ref = refs[20]
    o = of_ref[...] + ob_ref[...]
    r = r_ref[...]
    gn = gn_ref[...]
    parts = []
    for h in range(N_HEADS):
        vs = slice(h * dv, (h + 1) * dv)
        oh = o[:, vs]
        ms = jnp.mean(jnp.square(oh), axis=-1, keepdims=True)
        parts.append((oh * lax.rsqrt(ms + EPS) * gn * _silu(r[:, vs])).astype(BF16))
    pg = _dot(jnp.concatenate(parts, axis=-1), wpg_ref[...])

    y = _gelu_tanh(yf_ref[...] + yb_ref[...] + d_ref[...] * u_ref[...])
    y = y * jax.nn.sigmoid(_dot(y.astype(BF16), wglu_ref[...]) + bglu_ref[...])
    ps = _dot(y.astype(BF16), wps_ref[...])

    merged = jax.nn.sigmoid(ga_ref[...]) * pg + jax.nn.sigmoid(gb_ref[...]) * ps
    x = x_ref[...] + mod_ref[:, 2 * d:3 * d] * _dot(merged.astype(BF16), wout_ref[...])

    h = _rms_mod(x, g2_ref[...], mod_ref[:, 4 * d:5 * d], mod_ref[:, 3 * d:4 * d]).astype(BF16)
    n_ff = w1_ref.shape[1] // ff_block
    cols = lambda j: slice(j * ff_block, (j + 1) * ff_block)
    hidden = lambda j: jnp.square(jnp.maximum(_dot(h, w1_ref[:, cols(j)]), 0.0)).astype(BF16)
    a = hidden(0)
    acc = None
    for j in range(n_ff):
        a_next = hidden(j + 1) if j + 1 < n_ff else None
        part = _dot(a, w2_ref[cols(j), :])
        acc = part if acc is None else acc + part
        a = a_next
    x = x + mod_ref[:, 5 * d:6 * d] * acc
    if final_norm:
        ms = jnp.mean(jnp.square(x), axis=-1, keepdims=True)
        x = x * lax.rsqrt(ms + EPS) * fg_ref[...]
    o_ref[...] = x


def _mixmlp(x, o_f, o_b, r, ga, gb, u_tm, y, mod, p, final_g, tm):
    b, t, d = x.shape
    dvv = r.shape[-1]
    w = u_tm.shape[1] // b
    per_batch_mod = mod.shape[0] != 1
    final_norm = final_g is not None
    row_spec = lambda n: pl.BlockSpec((None, tm, n), lambda bi, i: (bi, i, 0))
    mod_idx = (lambda bi, i: (bi, 0, 0)) if per_batch_mod else (lambda bi, i: (0, 0, 0))
    weights = [p["w_glu"], p["b_glu"].reshape(1, -1), p["w_proj_gla"], p["w_proj_s5"], p["w_out"],
               p["norm2_g"].reshape(1, d), p["w_ff1"], p["w_ff2"]]
    if final_norm:
        weights.append(final_g.reshape(1, d))
    in_specs = [
        row_spec(d), row_spec(dvv), row_spec(dvv), row_spec(dvv), row_spec(d), row_spec(d),
        pl.BlockSpec((tm, w), lambda bi, i: (i, bi)),
        pl.BlockSpec((None, tm, w), lambda bi, i: (0, i, bi)),
        pl.BlockSpec((None, tm, w), lambda bi, i: (1, i, bi)),
        pl.BlockSpec((None, 1, mod.shape[2]), mod_idx),
        _const_spec((1, dvv // N_HEADS)), _const_spec((1, w)),
    ] + [_const_spec(a.shape) for a in weights]
    return pl.pallas_call(
        functools.partial(_mixmlp_kernel, d=d, dv=dvv // N_HEADS, ff_block=1024,
                          final_norm=final_norm),
        grid=(b, t // tm),
        in_specs=in_specs,
        out_specs=row_spec(d),
        out_shape=jax.ShapeDtypeStruct((b, t, d), F32),
        compiler_params=pltpu.CompilerParams(
            dimension_semantics=("parallel", "parallel"), vmem_limit_bytes=VMEM_LIMIT),
        name="mixmlp",
    )(x, o_f, o_b, r, ga, gb, u_tm, y, y, mod, p["gla_norm_g"].reshape(1, -1),
      p["s5_d"].reshape(1, -1), *weights)


def _grid_pos_tables(n_tokens, dim):
    quarter = dim // 4
    omega = 1.0 / (POS_BASE ** (jnp.arange(quarter, dtype=F32) / quarter))
    ar = jnp.arange(n_tokens // GRID_W, dtype=F32)[:, None] * omega
    ac = jnp.arange(GRID_W, dtype=F32)[:, None] * omega
    return (jnp.concatenate([jnp.sin(ar), jnp.cos(ar)], axis=-1),
            jnp.concatenate([jnp.sin(ac), jnp.cos(ac)], axis=-1))


def _split_w_in(w_in, dk_all, dv_all, s5w, d):
    splits = (dk_all, dk_all, dv_all, dv_all, N_DIR * GATE_RANK, s5w, d, d)
    idx = np.cumsum((0,) + splits)
    w_in = w_in.astype(BF16)
    parts = [w_in[:, :, idx[j]:idx[j + 1]] for j in range(len(splits))]
    parts[4] = jnp.pad(parts[4], ((0, 0), (0, 0), (0, LANES - splits[4])))
    return parts


def _stream(x, pos, mods, gla_s0, s5_s0, layers, w_in_parts, s5_prm, final_g, emit_state, tm):
    b, t, d = x.shape
    gla_states, s5_re, s5_im = None, [], []
    for l, p in enumerate(layers):
        res = _inproj(x, pos if l == 0 else None, mods[l], p["norm1_g"], w_in_parts, l, min(t, 2 * tm))
        if l == 0 and pos is not None:
            x = res[0]
            res = res[1:]
        q, k, v, r, glr, u_tm, ga, gb = res
        s5_out = _s5_scan(u_tm, s5_prm, l, None if s5_s0 is None else s5_s0[l], b, emit_state,
                          tt=min(t, 256))
        gla_out = _gla_scan(q, k, v, glr, p["wup"], p["bg"], l, len(layers), gla_s0, gla_states,
                            emit_state, tt=min(t, 1024))
        o_f, o_b = gla_out[:2]
        if emit_state:
            y, f_re, f_im = s5_out
            gla_states = gla_out[2]
            s5_re.append(f_re)
            s5_im.append(f_im)
        else:
            y = s5_out
        last = l == len(layers) - 1
        x = _mixmlp(x, o_f, o_b, r, ga, gb, u_tm, y, mods[l], p, final_g if last else None, tm)
    return x, gla_states, s5_re, s5_im


def kernel(x_prompt, x_sample, c, cache_gla_state, state_s5_re, state_s5_im, c_ctx, w_mod, b_mod,
           norm1_g, w_in, w_gate_up, b_gate, gla_norm_g, w_proj_gla, s5_lam_re, s5_lam_im,
           s5_log_step, s5_b_re, s5_b_im, s5_c_re, s5_c_im, s5_d, w_glu, b_glu, w_proj_s5, w_out,
           norm2_g, w_ff1, w_ff2, final_g):
    depth = w_in.shape[0]
    nb, seq, d = x_prompt.shape
    db, dseq, _ = x_sample.shape
    dk_all = w_gate_up.shape[-1]
    dv_all = w_proj_gla.shape[1]
    s5w = s5_d.shape[-1]
    n_groups, n_state = s5_lam_re.shape[2], s5_lam_re.shape[3]
    dk, dv = dk_all // N_HEADS, dv_all // N_HEADS
    assert nb % SUBLANES == 0 and db % SUBLANES == 0

    n_cond = -(-(db + 1) // SUBLANES) * SUBLANES
    conds = jnp.concatenate([c, c_ctx[None], jnp.zeros((n_cond - db - 1, d), F32)], axis=0)
    mod_all = _modulation(conds, w_mod, b_mod)
    mods_lat = [mod_all[l, :db].reshape(db, 1, N_MOD * d) for l in range(depth)]
    mods_ctx = [mod_all[l, db:db + 1].reshape(1, 1, N_MOD * d) for l in range(depth)]

    s5_prm = _s5_params(s5_lam_re, s5_lam_im, s5_log_step, s5_b_re, s5_b_im, s5_c_re, s5_c_im)
    wup = jnp.stack([jnp.pad(w_gate_up[:, dd], ((0, 0), (dd * GATE_RANK, LANES - (dd + 1) * GATE_RANK),
                                                (0, 0))) for dd in range(N_DIR)], axis=1).astype(BF16)
    w_in_parts = _split_w_in(w_in, dk_all, dv_all, s5w, d)
    layers = []
    for l in range(depth):
        layers.append(dict(
            norm1_g=norm1_g[l],
            wup=wup[l], bg=b_gate[l].reshape(N_DIR, 1, dk_all),
            gla_norm_g=gla_norm_g[l], w_proj_gla=w_proj_gla[l].astype(BF16),
            s5_d=s5_d[l], w_glu=w_glu[l].astype(BF16), b_glu=b_glu[l],
            w_proj_s5=w_proj_s5[l].astype(BF16), w_out=w_out[l].astype(BF16),
            norm2_g=norm2_g[l], w_ff1=w_ff1[l].astype(BF16), w_ff2=w_ff2[l].astype(BF16)))

    y_prompt, new_gla_state, s5_re, s5_im = _stream(
        x_prompt, None, mods_ctx, None, None, layers, w_in_parts, s5_prm, final_g, True,
        tm=min(seq, 256))
    to_state = lambda a: a.transpose(1, 0, 2).reshape(nb, N_DIR, n_groups, n_state)
    new_s5_re = jnp.stack([to_state(a) for a in s5_re], axis=1)
    new_s5_im = jnp.stack([to_state(a) for a in s5_im], axis=1)

    pos = _grid_pos_tables(dseq, d)
    from_state = lambda a: a.reshape(db, N_DIR, n_groups * n_state).transpose(1, 0, 2)
    s5_s0 = [(from_state(state_s5_re[:, l]), from_state(state_s5_im[:, l])) for l in range(depth)]
    y_sample, _, _, _ = _stream(
        x_sample, pos, mods_lat, cache_gla_state, s5_s0, layers, w_in_parts, s5_prm, final_g, False,
        tm=min(dseq, 256))

    return (y_prompt, y_sample, new_gla_state, new_s5_re, new_s5_im)
```

```python
import functools
import math

import jax
import jax.numpy as jnp
import numpy as np
from jax import lax
from jax.experimental import pallas as pl
from jax.experimental.pallas import tpu as pltpu

N_DIR = 2
N_HEADS = 4
GATE_RANK = 16
GATE_TAU = 16.0
CHUNK = 64
S5_GROUP = 16
S5_STATE = 64
N_MOD = 6
EPS = 1e-6
GRID_W = 64
POS_BASE = 10000.0

LANES = 128
SUBLANES = 8
VMEM_LIMIT = 56 * 1024 * 1024

F32 = jnp.float32
BF16 = jnp.bfloat16


def _const_spec(shape):
    nd = len(shape)
    return pl.BlockSpec(shape, lambda *_: (0,) * nd, pipeline_mode=pl.Buffered(1))


def _dot(a, b):
    return jnp.dot(a, b, preferred_element_type=F32)


def _rms_mod(x, g, scale, shift):
    ms = jnp.mean(jnp.square(x), axis=-1, keepdims=True)
    return x * lax.rsqrt(ms + EPS) * g * (1.0 + scale) + shift


def _silu(x):
    return x * jax.nn.sigmoid(x)


def _mod_kernel(c_ref, w_ref, b_ref, o_ref):
    a = _silu(c_ref[...]).astype(BF16)
    o_ref[...] = _dot(a, w_ref[...].astype(BF16)) + b_ref[...]


def _modulation(conds, w_mod, b_mod):
    depth, d, dm = w_mod.shape
    rows = conds.shape[0]
    tn = 1024
    return pl.pallas_call(
        _mod_kernel,
        grid=(depth, dm // tn),
        in_specs=[
            pl.BlockSpec((rows, d), lambda l, j: (0, 0)),
            pl.BlockSpec((None, d, tn), lambda l, j: (l, 0, j)),
            pl.BlockSpec((None, 1, tn), lambda l, j: (l, 0, j)),
        ],
        out_specs=pl.BlockSpec((None, rows, tn), lambda l, j: (l, 0, j)),
        out_shape=jax.ShapeDtypeStruct((depth, rows, dm), F32),
        compiler_params=pltpu.CompilerParams(vmem_limit_bytes=VMEM_LIMIT),
        name="modulation",
    )(conds, w_mod, b_mod.reshape(depth, 1, dm))


def _inproj_kernel(*refs, d, add_pos, n_sub):
    if add_pos:
        x_ref, prow_ref, pcol_ref, mod_ref, g_ref = refs[:5]
        rest = refs[5:]
    else:
        x_ref, mod_ref, g_ref = refs[:3]
        rest = refs[3:]
    w_refs = rest[:8]
    out_refs = rest[8:]
    shift = mod_ref[:, 0:d]
    scale = mod_ref[:, d:2 * d]
    sub = x_ref.shape[0] // n_sub
    for si in range(n_sub):
        rows = slice(si * sub, (si + 1) * sub)
        x = x_ref[rows, :]
        proj_refs = out_refs
        if add_pos:
            n_rows = sub // GRID_W
            prow = prow_ref[si * n_rows:(si + 1) * n_rows, :]
            prow = jnp.concatenate([jnp.broadcast_to(prow[r:r + 1], (GRID_W, prow.shape[1]))
                                    for r in range(n_rows)], axis=0)
            pcol = jnp.concatenate([pcol_ref[...]] * n_rows, axis=0)
            x = x + jnp.concatenate([prow, pcol], axis=1)
            out_refs[0][rows, :] = x
            proj_refs = out_refs[1:]
        h = _rms_mod(x, g_ref[...], scale, shift).astype(BF16)
        for w_ref, o_ref in zip(w_refs, proj_refs):
            o_ref[rows, :] = _dot(h, w_ref[...]).astype(o_ref.dtype)


def _inproj(x, pos, mod, g, weights, layer, tm):
    b, t, d = x.shape
    add_pos = pos is not None
    per_batch_mod = mod.shape[0] != 1
    widths = [w.shape[2] for w in weights]
    row_spec = lambda n: pl.BlockSpec((None, tm, n), lambda bi, i: (bi, i, 0))
    in_specs = [row_spec(d)]
    args = [x]
    if add_pos:
        assert tm % (2 * GRID_W) == 0
        in_specs += [pl.BlockSpec((tm // GRID_W, d // 2), lambda bi, i: (i, 0)),
                     _const_spec(pos[1].shape)]
        args += list(pos)
    mod_idx = (lambda bi, i: (bi, 0, 0)) if per_batch_mod else (lambda bi, i: (0, 0, 0))
    in_specs += [pl.BlockSpec((None, 1, mod.shape[2]), mod_idx), _const_spec((1, d))]
    args += [mod, g.reshape(1, d)]
    in_specs += [pl.BlockSpec((None,) + w.shape[1:], lambda bi, i: (layer, 0, 0),
                              pipeline_mode=pl.Buffered(1)) for w in weights]
    args += list(weights)

    out_specs, out_shapes = [], []
    if add_pos:
        out_specs.append(row_spec(d))
        out_shapes.append(jax.ShapeDtypeStruct((b, t, d), F32))
    for idx, n in enumerate(widths):
        if idx == 5:
            out_specs.append(pl.BlockSpec((tm, n), lambda bi, i: (i, bi)))
            out_shapes.append(jax.ShapeDtypeStruct((t, b * n), F32))
        else:
            out_specs.append(row_spec(n))
            out_shapes.append(jax.ShapeDtypeStruct((b, t, n), BF16 if idx in (0, 1, 2, 4) else F32))
    return pl.pallas_call(
        functools.partial(_inproj_kernel, d=d, add_pos=add_pos, n_sub=2),
        grid=(b, t // tm),
        in_specs=in_specs,
        out_specs=out_specs,
        out_shape=out_shapes,
        compiler_params=pltpu.CompilerParams(
            dimension_semantics=("parallel", "parallel"), vmem_limit_bytes=VMEM_LIMIT),
        name="inproj",
    )(*args)


S5_BLOCK = 4


def _s5_kernel(*refs, tt, w, n_p, zero_init, emit_state):
    u_ref, l4r_ref, l4i_ref, wzr_ref, wzi_ref, wor_ref, woi_ref, wt_ref = refs[:8]
    rest = refs[8:]
    if not zero_init:
        s0r_ref, s0i_ref = rest[:2]
        rest = rest[2:]
    if emit_state:
        y_ref, fr_ref, fi_ref = rest[:3]
        rest = rest[3:]
    else:
        y_ref = rest[0]
        rest = rest[1:]
    usc, xsc, zre, zim, ypk, car, cai = rest
    direction = pl.program_id(1)
    i = pl.program_id(2)
    n_slab = w // LANES
    n_blk = tt // S5_BLOCK
    half = LANES // 2
    tw = 2 * LANES
    grp = S5_BLOCK * SUBLANES

    @pl.when(i == 0)
    def _():
        if zero_init:
            car[...] = jnp.zeros_like(car)
            cai[...] = jnp.zeros_like(cai)
        else:
            car[...] = s0r_ref[...]
            cai[...] = s0i_ref[...]

    for b in range(SUBLANES):
        for s in range(n_slab):
            usc[s, pl.ds(b, tt, stride=SUBLANES), :] = u_ref[:, b * w + s * LANES:b * w + (s + 1) * LANES]

    low = lax.broadcasted_iota(jnp.int32, (2 * SUBLANES, LANES), 1) < half
    swap = lambda a: pltpu.roll(a, half, 1)

    def pack(s):
        for r2 in range(n_blk // 2):
            a = [jnp.concatenate([usc[s, pl.ds(base + j * SUBLANES, SUBLANES), :]
                                  for base in (2 * r2 * grp, (2 * r2 + 1) * grp)], axis=0)
                 for j in range(S5_BLOCK)]
            asw = [swap(v) for v in a]
            for h in range(2):
                pick = lambda j, slot_half: a[j] if h == slot_half else asw[j]
                cols = [jnp.where(low, pick(2 * c, 0), pick(2 * c + 1, 1)) for c in range(2)]
                xsc[2 * s + h, pl.ds(r2 * 2 * SUBLANES, 2 * SUBLANES), :] = (
                    jnp.concatenate(cols, axis=1).astype(BF16))

    def block_inputs(p):
        x = xsc[p]
        zre[:, p * tw:(p + 1) * tw] = _dot(x, wzr_ref[p])
        zim[:, p * tw:(p + 1) * tw] = _dot(x, wzi_ref[p])

    def scan(s):
        ls = slice(s * 2 * tw, (s + 1) * 2 * tw)
        lr = l4r_ref[:, ls]
        li = l4i_ref[:, ls]
        cr, ci = car[:, ls], cai[:, ls]
        for j in range(n_blk):
            r = j + direction * (n_blk - 1 - 2 * j)
            r0 = pl.multiple_of(r * SUBLANES, SUBLANES)
            zr = zre[pl.ds(r0, SUBLANES), ls]
            zi = zim[pl.ds(r0, SUBLANES), ls]
            zre[pl.ds(r0, SUBLANES), ls] = cr
            zim[pl.ds(r0, SUBLANES), ls] = ci
            cr, ci = lr * cr - li * ci + zr, lr * ci + li * cr + zi
        car[:, ls] = cr
        cai[:, ls] = ci

    def block_outputs(p):
        ypk[p] = (_dot(zre[:, p * tw:(p + 1) * tw].astype(BF16), wor_ref[p])
                  + _dot(zim[:, p * tw:(p + 1) * tw].astype(BF16), woi_ref[p])
                  + _dot(xsc[p], wt_ref[p]))

    def unpack(s):
        for r2 in range(n_blk // 2):
            rows = pl.ds(r2 * 2 * SUBLANES, 2 * SUBLANES)
            for j in range(S5_BLOCK):
                c, slot_half = divmod(j, 2)
                left = ypk[2 * s, rows, c * LANES:(c + 1) * LANES]
                right = ypk[2 * s + 1, rows, c * LANES:(c + 1) * LANES]
                v = jnp.where(low, left if slot_half == 0 else swap(left),
                              right if slot_half == 1 else swap(right))
                usc[s, pl.ds(2 * r2 * grp + j * SUBLANES, SUBLANES), :] = v[:SUBLANES]
                usc[s, pl.ds((2 * r2 + 1) * grp + j * SUBLANES, SUBLANES), :] = v[SUBLANES:]

    for s in range(n_slab + 1):
        if s < n_slab:
            pack(s)
            block_inputs(2 * s)
            block_inputs(2 * s + 1)
        if s >= 1:
            block_outputs(2 * s - 2)
            block_outputs(2 * s - 1)
            unpack(s - 1)
        if s < n_slab:
            scan(s)

    for b in range(SUBLANES):
        for s in range(n_slab):
            y_ref[:, b * w + s * LANES:b * w + (s + 1) * LANES] = usc[s, pl.ds(b, tt, stride=SUBLANES), :]

    if emit_state:
        @pl.when(i == pl.num_programs(2) - 1)
        def _():
            fr_ref[...] = car[...]
            fi_ref[...] = cai[...]


def _s5_scan(u_tm, prm, layer, s0, b, emit_state, tt):
    t = u_tm.shape[0]
    w = u_tm.shape[1] // b
    l4r, l4i = prm[:2]
    mats = prm[2:]
    n_p = mats[0].shape[2]
    ns = l4r.shape[-1]
    n_t = t // tt
    n_g = b // SUBLANES
    rows = tt // S5_BLOCK * SUBLANES
    zero_init = s0 is None
    tile = lambda g, d, i: i + d * (n_t - 1 - 2 * i)
    dir_spec = lambda shape: pl.BlockSpec((None, None) + shape,
                                          lambda g, d, i: (layer, d) + (0,) * len(shape))
    in_specs = [pl.BlockSpec((tt, SUBLANES * w), lambda g, d, i: (tile(g, d, i), g)),
                dir_spec((SUBLANES, ns)), dir_spec((SUBLANES, ns))]
    in_specs += [dir_spec(m.shape[2:]) for m in mats]
    args = [u_tm, l4r, l4i] + list(mats)
    state_spec = pl.BlockSpec((None, SUBLANES, ns), lambda g, d, i: (d, g, 0))
    if not zero_init:
        in_specs += [state_spec, state_spec]
        args += list(s0)
    out_specs = [pl.BlockSpec((None, tt, SUBLANES * w), lambda g, d, i: (d, tile(g, d, i), g))]
    out_shapes = [jax.ShapeDtypeStruct((N_DIR, t, b * w), F32)]
    if emit_state:
        out_specs += [state_spec, state_spec]
        out_shapes += [jax.ShapeDtypeStruct((N_DIR, b, ns), F32)] * 2
    res = pl.pallas_call(
        functools.partial(_s5_kernel, tt=tt, w=w, n_p=n_p, zero_init=zero_init,
                          emit_state=emit_state),
        grid=(n_g, N_DIR, n_t),
        in_specs=in_specs,
        out_specs=out_specs,
        out_shape=out_shapes,
        scratch_shapes=[pltpu.VMEM((w // LANES, tt * SUBLANES, LANES), F32),
                        pltpu.VMEM((n_p, rows, 2 * LANES), BF16),
                        pltpu.VMEM((rows, ns), F32), pltpu.VMEM((rows, ns), F32),
                        pltpu.VMEM((n_p, rows, 2 * LANES), F32),
                        pltpu.VMEM((SUBLANES, ns), F32), pltpu.VMEM((SUBLANES, ns), F32)],
        compiler_params=pltpu.CompilerParams(
            dimension_semantics=("parallel", "parallel", "arbitrary"), vmem_limit_bytes=VMEM_LIMIT),
        name="s5_scan",
    )(*args)
    if emit_state:
        return res[0], res[1], res[2]
    return res[0]


def _s5_params(lam_re, lam_im, log_step, b_re, b_im, c_re, c_im):
    depth, n_dir, g, p = lam_re.shape
    n = b_re.shape[-1]
    blk = S5_BLOCK
    gpt = 2 * LANES // p
    n_p = g // gpt
    tw = 2 * LANES
    assert gpt * n * blk == tw and n_dir == N_DIR
    hi = lax.Precision.HIGHEST
    step = jnp.exp(log_step)[..., None]
    a = lam_re * step
    th = lam_im * step
    k = jnp.arange(blk + 1, dtype=F32).reshape(-1, 1, 1, 1, 1)
    pw_re = jnp.exp(k * a) * jnp.cos(k * th)
    pw_im = jnp.exp(k * a) * jnp.sin(k * th)
    den = lam_re * lam_re + lam_im * lam_im
    fr = ((pw_re[1] - 1.0) * lam_re + pw_im[1] * lam_im) / den
    fi = (pw_im[1] * lam_re - (pw_re[1] - 1.0) * lam_im) / den
    bbr = fr[..., None] * b_re - fi[..., None] * b_im
    bbi = fr[..., None] * b_im + fi[..., None] * b_re
    lb_re = pw_re[..., None] * bbr - pw_im[..., None] * bbi
    lb_im = pw_re[..., None] * bbi + pw_im[..., None] * bbr
    offs = np.arange(blk)
    pos = np.stack([offs, blk - 1 - offs])

    def per_dir(arr, idx):
        return jnp.stack([arr[idx[d], :, d] for d in range(n_dir)], axis=idx.ndim)

    split = lambda m, ax: m.reshape(m.shape[:ax] + (n_p, gpt) + m.shape[ax + 1:])
    ridx, cidx = np.arange(tw)[:, None], np.arange(tw)[None, :]
    rep_state = jnp.asarray(np.arange(p)[:, None] == cidx % p, BF16)
    chan = lambda c: (c // (gpt * n)) * n + c % n
    rep_chan = jnp.asarray(np.arange(blk * n)[:, None] == chan(cidx), BF16)
    grp_of_chan = lambda c: (c // n) % gpt
    mask_z = jnp.asarray(grp_of_chan(ridx) == cidx // p, BF16)
    mask_o = jnp.asarray(ridx // p == grp_of_chan(cidx), BF16)
    mask_t = jnp.asarray(grp_of_chan(ridx) == grp_of_chan(cidx), BF16)

    def dense(compact, rep, mask):
        full = jnp.einsum("ldprk,kc->ldprc", compact.astype(BF16), rep, preferred_element_type=BF16)
        return full * mask

    wz = [dense(split(per_dir(m, blk - 1 - pos), 3).transpose(1, 2, 3, 0, 4, 6, 5)
                .reshape(depth, n_dir, n_p, tw, p), rep_state, mask_z) for m in (lb_re, lb_im)]
    pr = per_dir(pw_re, pos + 1)[:, :, :, :, None, :]
    pi = per_dir(pw_im, pos + 1)[:, :, :, :, None, :]
    wo = [dense(split(m, 3).transpose(1, 2, 3, 4, 6, 0, 5).reshape(depth, n_dir, n_p, tw, blk * n),
                rep_chan, mask_o) for m in (c_re * pr - c_im * pi, -(c_re * pi + c_im * pr))]
    kern = (jnp.einsum("ldgnp,mldgpq->mldgqn", c_re, lb_re[:blk], precision=hi)
            - jnp.einsum("ldgnp,mldgpq->mldgqn", c_im, lb_im[:blk], precision=hi))
    lag = pos[:, None, :] - pos[:, :, None]
    valid = jnp.asarray((lag >= 0).transpose(1, 2, 0), F32).reshape(blk, blk, 1, n_dir, 1, 1, 1)
    wt = dense(split(per_dir(kern, np.maximum(lag, 0)) * valid, 4).transpose(2, 3, 4, 0, 5, 6, 1, 7)
               .reshape(depth, n_dir, n_p, tw, blk * n), rep_chan, mask_t)
    bcast = lambda v: jnp.broadcast_to(v.reshape(depth, n_dir, 1, g * p),
                                       (depth, n_dir, SUBLANES, g * p))
    return [bcast(pw_re[blk]), bcast(pw_im[blk])] + wz + wo + [wt]


def _log_decay(z):
    return (jnp.minimum(z, 0.0) - jnp.log(1.0 + jnp.exp(-jnp.abs(z)))) * (1.0 / GATE_TAU)


def _split2(x):
    hi = x.astype(BF16)
    return hi, (x - hi.astype(F32)).astype(BF16)


def _gla_kernel(*refs, n_chunks, dk, dv, zero_init, emit_state, aliased_states, layer):
    dir_refs = (refs[0:4], refs[4:8])
    wup_ref, bg_ref = refs[8:10]
    rest = refs[10:]
    if not zero_init:
        s0_ref = rest[0]
        rest = rest[1:]
    if aliased_states:
        rest = rest[1:]
    if emit_state:
        of_ref, ob_ref, sf_ref, st, prep, dec = rest
    else:
        of_ref, ob_ref, st, prep, dec = rest
    o_refs = (of_ref, ob_ref)
    i = pl.program_id(1)
    scale = dk ** -0.5

    @pl.when(i == 0)
    def _():
        for d in range(N_DIR):
            for h in range(N_HEADS):
                if zero_init:
                    st[d, h] = jnp.zeros((dk, dv), F32)
                else:
                    st[d, h] = s0_ref[d, h]

    row = lax.broadcasted_iota(jnp.int32, (CHUNK, CHUNK), 0)
    col = lax.broadcasted_iota(jnp.int32, (CHUNK, CHUNK), 1)
    causal = (col <= row, col >= row)
    row2 = lax.broadcasted_iota(jnp.int32, (CHUNK, 2 * CHUNK), 0)
    col2 = lax.broadcasted_iota(jnp.int32, (CHUNK, 2 * CHUNK), 1) % CHUNK
    causal2 = (col2 <= row2, col2 >= row2)
    ones = tuple(jnp.where(c, 1.0, 0.0).astype(BF16) for c in causal)
    first_head = lax.broadcasted_iota(jnp.int32, (CHUNK, 2 * dk), 1) < dk

    dims = (((1,), (1,)), ((), ()))
    chunk_of = lambda d, cc: cc if d == 0 else n_chunks - 1 - cc
    rows_of = lambda d, cc: pl.ds(chunk_of(d, cc) * CHUNK, CHUNK)

    def gate_z(d, cc):
        glr_ref = dir_refs[d][3]
        return _dot(glr_ref[rows_of(d, cc), :], wup_ref[d]) + bg_ref[d]

    def gate_cum(d, g):
        return sum(_dot(ones[d], part) for part in _split2(g))

    def gate_store(slot, d, cc, g, cum):
        q_ref, k_ref = dir_refs[d][:2]
        rows = rows_of(d, cc)
        total = jnp.sum(g, axis=0, keepdims=True)
        prep[slot, d, 0] = (q_ref[rows, :].astype(F32) * scale * jnp.exp(cum)).astype(BF16)
        kk = k_ref[rows, :].astype(F32)
        prep[slot, d, 1] = (kk * jnp.exp(-cum)).astype(BF16)
        prep[slot, d, 2] = (kk * jnp.exp(total - cum)).astype(BF16)
        decay = jnp.exp(total)
        for h in range(N_HEADS):
            dec[slot, d, h] = jnp.broadcast_to(decay[:, ksl(h)], (dk, dk)).T

    pairs = [(d, h) for d in range(N_DIR) for h in range(N_HEADS)]
    ksl = lambda h: slice(h * dk, (h + 1) * dk)
    vsl = lambda h: slice(h * dv, (h + 1) * dv)

    for d in range(N_DIR):
        g0 = _log_decay(gate_z(d, 0))
        gate_store(0, d, 0, g0, gate_cum(d, g0))

    for cc in range(n_chunks):
        slot = cc % 2
        more = cc + 1 < n_chunks
        if more:
            z_next = [gate_z(d, cc + 1) for d in range(N_DIR)]
        sc = {}
        for d in range(N_DIR):
            for hp in range(N_HEADS // 2):
                ks2 = slice(2 * hp * dk, (2 * hp + 2) * dk)
                k2 = prep[slot, d, 1, :, ks2]
                none = jnp.zeros_like(k2)
                k_blocks = jnp.concatenate([jnp.where(first_head, k2, none),
                                            jnp.where(first_head, none, k2)], axis=0)
                s = lax.dot_general(prep[slot, d, 0, :, ks2], k_blocks, dims,
                                    preferred_element_type=F32)
                sc[d, hp] = jnp.where(causal2[d], s, 0.0).astype(BF16)
        if more:
            g_next = [_log_decay(z) for z in z_next]
        for d, h in pairs:
            v_ref = dir_refs[d][2]
            rows = rows_of(d, cc)
            vh = v_ref[rows, vsl(h)]
            none = jnp.zeros_like(vh)
            lhs = jnp.concatenate([prep[slot, d, 0, :, ksl(h)], sc[d, h // 2]], axis=1)
            rhs = jnp.concatenate([st[d, h].astype(BF16)] + ([vh, none] if h % 2 == 0 else [none, vh]),
                                  axis=0)
            o_refs[d][rows, vsl(h)] = _dot(lhs, rhs).astype(BF16)
        if more:
            cum_next = [gate_cum(d, g_next[d]) for d in range(N_DIR)]
        for d, h in pairs:
            v_ref = dir_refs[d][2]
            vh = v_ref[rows_of(d, cc), vsl(h)]
            upd = lax.dot_general(prep[slot, d, 2, :, ksl(h)], vh, (((0,), (0,)), ((), ())),
                                  preferred_element_type=F32)
            decay = jnp.concatenate([dec[slot, d, h]] * (dv // dk), axis=1)
            st[d, h] = decay * st[d, h] + upd
        if more:
            for d in range(N_DIR):
                gate_store(1 - slot, d, cc + 1, g_next[d], cum_next[d])

    if emit_state:
        @pl.when(i == pl.num_programs(1) - 1)
        def _():
            if not aliased_states:
                sf_ref[...] = jnp.zeros_like(sf_ref)
            for d in range(N_DIR):
                for h in range(N_HEADS):
                    if aliased_states:
                        sf_ref[d, h] = st[d, h]
                    else:
                        sf_ref[layer, d, h] = st[d, h]


def _gla_scan(q, k, v, glr, wup, bg, layer, depth, s0, states, emit_state, tt):
    b, t, dkk = q.shape
    dvv = v.shape[-1]
    dk, dv = dkk // N_HEADS, dvv // N_HEADS
    n_t = t // tt
    zero_init = s0 is None
    fwd_spec = lambda n: pl.BlockSpec((None, tt, n), lambda bi, i: (bi, i, 0))
    bwd_spec = lambda n: pl.BlockSpec((None, tt, n), lambda bi, i: (bi, n_t - 1 - i, 0))
    state_spec = pl.BlockSpec((None, None, N_DIR, N_HEADS, dk, dv), lambda bi, i: (bi, layer, 0, 0, 0, 0))
    widths = (dkk, dkk, dvv, glr.shape[-1])
    in_specs = ([fwd_spec(n) for n in widths] + [bwd_spec(n) for n in widths]
                + [_const_spec(wup.shape), _const_spec(bg.shape)])
    args = [q, k, v, glr, q, k, v, glr, wup, bg]
    if not zero_init:
        in_specs.append(state_spec)
        args.append(s0)
    out_specs = [fwd_spec(dvv), bwd_spec(dvv)]
    out_shapes = [jax.ShapeDtypeStruct((b, t, dvv), BF16)] * 2
    aliases = {}
    if emit_state:
        out_shapes.append(jax.ShapeDtypeStruct((b, depth, N_DIR, N_HEADS, dk, dv), F32))
        if states is None:
            out_specs.append(pl.BlockSpec((None, depth, N_DIR, N_HEADS, dk, dv),
                                          lambda bi, i: (bi, 0, 0, 0, 0, 0)))
        else:
            out_specs.append(state_spec)
            in_specs.append(pl.BlockSpec(memory_space=pl.ANY))
            args.append(states)
            aliases = {len(args) - 1: 2}
    return pl.pallas_call(
        functools.partial(_gla_kernel, n_chunks=tt // CHUNK, dk=dk, dv=dv, zero_init=zero_init,
                          emit_state=emit_state, aliased_states=bool(aliases), layer=layer),
        grid=(b, n_t),
        in_specs=in_specs,
        out_specs=out_specs,
        out_shape=out_shapes,
        input_output_aliases=aliases,
        scratch_shapes=[pltpu.VMEM((N_DIR, N_HEADS, dk, dv), F32),
                        pltpu.VMEM((2, N_DIR, 3, CHUNK, dkk), BF16),
                        pltpu.VMEM((2, N_DIR, N_HEADS, dk, dk), F32)],
        compiler_params=pltpu.CompilerParams(
            dimension_semantics=("parallel", "arbitrary"), vmem_limit_bytes=VMEM_LIMIT),
        name="gla_scan",
    )(*args)


def _gelu_tanh(x):
    return 0.5 * x * (1.0 + jnp.tanh(math.sqrt(2.0 / math.pi) * (x + 0.044715 * (x * x * x))))


def _mixmlp_kernel(*refs, d, dv, ff_block, final_norm):
    (x_ref, of_ref, ob_ref, r_ref, ga_ref, gb_ref, u_ref, yf_ref, yb_ref, mod_ref, gn_ref, d_ref,
     wglu_ref, bglu_ref, wpg_ref, wps_ref, wout_ref, g2_ref, w1_ref, w2_ref) = refs[:20]
    if final_norm:
        fg_ref, o_ref = refs[20:]
    else:
        o_ref = refs[20]
    o = of_ref[...].astype(F32) + ob_ref[...].astype(F32)
    r = r_ref[...]
    gn = gn_ref[...]
    parts = []
    for h in range(N_HEADS):
        vs = slice(h * dv, (h + 1) * dv)
        oh = o[:, vs]
        ms = jnp.mean(jnp.square(oh), axis=-1, keepdims=True)
        parts.append((oh * lax.rsqrt(ms + EPS) * gn * _silu(r[:, vs])).astype(BF16))
    pg = _dot(jnp.concatenate(parts, axis=-1), wpg_ref[...])

    y = _gelu_tanh(yf_ref[...] + yb_ref[...] + d_ref[...] * u_ref[...])
    y = y * jax.nn.sigmoid(_dot(y.astype(BF16), wglu_ref[...]) + bglu_ref[...])
    ps = _dot(y.astype(BF16), wps_ref[...])

    merged = jax.nn.sigmoid(ga_ref[...]) * pg + jax.nn.sigmoid(gb_ref[...]) * ps
    x = x_ref[...] + mod_ref[:, 2 * d:3 * d] * _dot(merged.astype(BF16), wout_ref[...])

    h = _rms_mod(x, g2_ref[...], mod_ref[:, 4 * d:5 * d], mod_ref[:, 3 * d:4 * d]).astype(BF16)
    n_ff = w1_ref.shape[1] // ff_block
    cols = lambda j: slice(j * ff_block, (j + 1) * ff_block)
    hidden = lambda j: jnp.square(jnp.maximum(_dot(h, w1_ref[:, cols(j)]), 0.0)).astype(BF16)
    a = hidden(0)
    acc = None
    for j in range(n_ff):
        a_next = hidden(j + 1) if j + 1 < n_ff else None
        part = _dot(a, w2_ref[cols(j), :])
        acc = part if acc is None else acc + part
        a = a_next
    x = x + mod_ref[:, 5 * d:6 * d] * acc
    if final_norm:
        ms = jnp.mean(jnp.square(x), axis=-1, keepdims=True)
        x = x * lax.rsqrt(ms + EPS) * fg_ref[...]
    o_ref[...] = x


def _mixmlp(x, o_f, o_b, r, ga, gb, u_tm, y, mod, p, final_g, tm):
    b, t, d = x.shape
    dvv = r.shape[-1]
    w = u_tm.shape[1] // b
    per_batch_mod = mod.shape[0] != 1
    final_norm = final_g is not None
    row_spec = lambda n: pl.BlockSpec((None, tm, n), lambda bi, i: (bi, i, 0))
    mod_idx = (lambda bi, i: (bi, 0, 0)) if per_batch_mod else (lambda bi, i: (0, 0, 0))
    weights = [p["w_glu"], p["b_glu"].reshape(1, -1), p["w_proj_gla"], p["w_proj_s5"], p["w_out"],
               p["norm2_g"].reshape(1, d), p["w_ff1"], p["w_ff2"]]
    if final_norm:
        weights.append(final_g.reshape(1, d))
    in_specs = [
        row_spec(d), row_spec(dvv), row_spec(dvv), row_spec(dvv), row_spec(d), row_spec(d),
        pl.BlockSpec((tm, w), lambda bi, i: (i, bi)),
        pl.BlockSpec((None, tm, w), lambda bi, i: (0, i, bi)),
        pl.BlockSpec((None, tm, w), lambda bi, i: (1, i, bi)),
        pl.BlockSpec((None, 1, mod.shape[2]), mod_idx),
        _const_spec((1, dvv // N_HEADS)), _const_spec((1, w)),
    ] + [_const_spec(a.shape) for a in weights]
    return pl.pallas_call(
        functools.partial(_mixmlp_kernel, d=d, dv=dvv // N_HEADS, ff_block=1024,
                          final_norm=final_norm),
        grid=(b, t // tm),
        in_specs=in_specs,
        out_specs=row_spec(d),
        out_shape=jax.ShapeDtypeStruct((b, t, d), F32),
        compiler_params=pltpu.CompilerParams(
            dimension_semantics=("parallel", "parallel"), vmem_limit_bytes=VMEM_LIMIT),
        name="mixmlp",
    )(x, o_f, o_b, r, ga, gb, u_tm, y, y, mod, p["gla_norm_g"].reshape(1, -1),
      p["s5_d"].reshape(1, -1), *weights)


def _grid_pos_tables(n_tokens, dim):
    quarter = dim // 4
    omega = 1.0 / (POS_BASE ** (jnp.arange(quarter, dtype=F32) / quarter))
    ar = jnp.arange(n_tokens // GRID_W, dtype=F32)[:, None] * omega
    ac = jnp.arange(GRID_W, dtype=F32)[:, None] * omega
    return (jnp.concatenate([jnp.sin(ar), jnp.cos(ar)], axis=-1),
            jnp.concatenate([jnp.sin(ac), jnp.cos(ac)], axis=-1))


def _split_w_in(w_in, dk_all, dv_all, s5w, d):
    splits = (dk_all, dk_all, dv_all, dv_all, N_DIR * GATE_RANK, s5w, d, d)
    idx = np.cumsum((0,) + splits)
    w_in = w_in.astype(BF16)
    parts = [w_in[:, :, idx[j]:idx[j + 1]] for j in range(len(splits))]
    parts[4] = jnp.pad(parts[4], ((0, 0), (0, 0), (0, LANES - splits[4])))
    return parts


def _stream(x, pos, mods, gla_s0, s5_s0, layers, w_in_parts, s5_prm, final_g, emit_state, tm):
    b, t, d = x.shape
    gla_states, s5_re, s5_im = None, [], []
    for l, p in enumerate(layers):
        res = _inproj(x, pos if l == 0 else None, mods[l], p["norm1_g"], w_in_parts, l, min(t, 2 * tm))
        if l == 0 and pos is not None:
            x = res[0]
            res = res[1:]
        q, k, v, r, glr, u_tm, ga, gb = res
        s5_out = _s5_scan(u_tm, s5_prm, l, None if s5_s0 is None else s5_s0[l], b, emit_state,
                          tt=min(t, 256))
        gla_out = _gla_scan(q, k, v, glr, p["wup"], p["bg"], l, len(layers), gla_s0, gla_states,
                            emit_state, tt=min(t, 1024))
        o_f, o_b = gla_out[:2]
        if emit_state:
            y, f_re, f_im = s5_out
            gla_states = gla_out[2]
            s5_re.append(f_re)
            s5_im.append(f_im)
        else:
            y = s5_out
        last = l == len(layers) - 1
        x = _mixmlp(x, o_f, o_b, r, ga, gb, u_tm, y, mods[l], p, final_g if last else None, tm)
    return x, gla_states, s5_re, s5_im


def kernel(x_prompt, x_sample, c, cache_gla_state, state_s5_re, state_s5_im, c_ctx, w_mod, b_mod,
           norm1_g, w_in, w_gate_up, b_gate, gla_norm_g, w_proj_gla, s5_lam_re, s5_lam_im,
           s5_log_step, s5_b_re, s5_b_im, s5_c_re, s5_c_im, s5_d, w_glu, b_glu, w_proj_s5, w_out,
           norm2_g, w_ff1, w_ff2, final_g):
    depth = w_in.shape[0]
    nb, seq, d = x_prompt.shape
    db, dseq, _ = x_sample.shape
    dk_all = w_gate_up.shape[-1]
    dv_all = w_proj_gla.shape[1]
    s5w = s5_d.shape[-1]
    n_groups, n_state = s5_lam_re.shape[2], s5_lam_re.shape[3]
    dk, dv = dk_all // N_HEADS, dv_all // N_HEADS
    assert nb % SUBLANES == 0 and db % SUBLANES == 0

    n_cond = -(-(db + 1) // SUBLANES) * SUBLANES
    conds = jnp.concatenate([c, c_ctx[None], jnp.zeros((n_cond - db - 1, d), F32)], axis=0)
    mod_all = _modulation(conds, w_mod, b_mod)
    mods_lat = [mod_all[l, :db].reshape(db, 1, N_MOD * d) for l in range(depth)]
    mods_ctx = [mod_all[l, db:db + 1].reshape(1, 1, N_MOD * d) for l in range(depth)]

    s5_prm = _s5_params(s5_lam_re, s5_lam_im, s5_log_step, s5_b_re, s5_b_im, s5_c_re, s5_c_im)
    wup = jnp.stack([jnp.pad(w_gate_up[:, dd], ((0, 0), (dd * GATE_RANK, LANES - (dd + 1) * GATE_RANK),
                                                (0, 0))) for dd in range(N_DIR)], axis=1).astype(BF16)
    w_in_parts = _split_w_in(w_in, dk_all, dv_all, s5w, d)
    layers = []
    for l in range(depth):
        layers.append(dict(
            norm1_g=norm1_g[l],
            wup=wup[l], bg=b_gate[l].reshape(N_DIR, 1, dk_all),
            gla_norm_g=gla_norm_g[l], w_proj_gla=w_proj_gla[l].astype(BF16),
            s5_d=s5_d[l], w_glu=w_glu[l].astype(BF16), b_glu=b_glu[l],
            w_proj_s5=w_proj_s5[l].astype(BF16), w_out=w_out[l].astype(BF16),
            norm2_g=norm2_g[l], w_ff1=w_ff1[l].astype(BF16), w_ff2=w_ff2[l].astype(BF16)))

    y_prompt, new_gla_state, s5_re, s5_im = _stream(
        x_prompt, None, mods_ctx, None, None, layers, w_in_parts, s5_prm, final_g, True,
        tm=min(seq, 256))
    to_state = lambda a: a.transpose(1, 0, 2).reshape(nb, N_DIR, n_groups, n_state)
    new_s5_re = jnp.stack([to_state(a) for a in s5_re], axis=1)
    new_s5_im = jnp.stack([to_state(a) for a in s5_im], axis=1)

    pos = _grid_pos_tables(dseq, d)
    from_state = lambda a: a.reshape(db, N_DIR, n_groups * n_state).transpose(1, 0, 2)
    s5_s0 = [(from_state(state_s5_re[:, l]), from_state(state_s5_im[:, l])) for l in range(depth)]
    y_sample, _, _, _ = _stream(
        x_sample, pos, mods_lat, cache_gla_state, s5_s0, layers, w_in_parts, s5_prm, final_g, False,
        tm=min(dseq, 256))

    return (y_prompt, y_sample, new_gla_state, new_s5_re, new_s5_im)
```
